```python
import math
import jax, jax.numpy as jnp
from jax import lax
import numpy as np

D_MODEL = 1024
BATCH = 4
SEQ = 8192
DEPTH = 2

N_EVEN = (DEPTH + 1) // 2
N_ODD = DEPTH // 2
D_FF = 2816
EPS = 1e-6
ROPE_THETA = 500000.0
HEAD_DIM = 64
ROT_DIM = HEAD_DIM // 4
Q_BLOCK = 128
NEG = -1e30
BIG = 1e30

LRU_WIDTH = D_MODEL // 2
LRU_BLOCKS = 8
LRU_BLOCK_DIM = LRU_WIDTH // LRU_BLOCKS
CONV_WIDTH = 4
LRU_C = 8.0
MOBA_HEADS = (D_MODEL // 2) // HEAD_DIM
MOBA_WIDTH = MOBA_HEADS * HEAD_DIM
MOBA_BLOCK = 256
MOBA_TOPK = 3
IN0_WIDTH = 2 * LRU_WIDTH + 3 * MOBA_WIDTH
MIX0_WIDTH = LRU_WIDTH + MOBA_WIDTH

NSA_HEADS = D_MODEL // HEAD_DIM
NSA_KV_GROUPS = 2
NSA_HPG = NSA_HEADS // NSA_KV_GROUPS
CMP_LEN = 32
CMP_STRIDE = 16
SEL_BLOCK = 64
SEL_TOPK = 16
WINDOW = 512
CMP_HIDDEN = 128
NSA_QD = NSA_HEADS * HEAD_DIM
NSA_KVD = NSA_KV_GROUPS * HEAD_DIM
IN1_WIDTH = NSA_QD + 6 * NSA_KVD + 3 * NSA_HEADS
MIX1_WIDTH = NSA_QD

kernel_name = "hybrid_lru_moba_nsa_macaron_adaln"


def rmsnorm(x, g):
    xf = x.astype(jnp.float32)
    y = xf * lax.rsqrt(jnp.mean(xf * xf, axis=-1, keepdims=True) + EPS)
    return (y * g.astype(jnp.float32)).astype(x.dtype)


def modulate(h, shift, scale):
    return h * (1.0 + scale[:, None, :]) + shift[:, None, :]


def swiglu(h, w1, w2):
    a, b = jnp.split(h @ w1, 2, axis=-1)
    return (jax.nn.silu(a) * b) @ w2


def rope_partial(x, pos):
    half = ROT_DIM // 2
    freqs = ROPE_THETA ** (-jnp.arange(half, dtype=jnp.float32) * 2.0 / ROT_DIM)
    ang = pos[:, None] * freqs[None, :]
    cos = jnp.cos(ang).astype(x.dtype)
    sin = jnp.sin(ang).astype(x.dtype)
    x1 = x[..., :half]
    x2 = x[..., half:ROT_DIM]
    return jnp.concatenate([x1 * cos - x2 * sin, x2 * cos + x1 * sin, x[..., ROT_DIM:]], axis=-1)


def masked_softmax(logits, mask):
    p = jax.nn.softmax(jnp.where(mask, logits, NEG), axis=-1)
    return jnp.where(mask, p, 0.0)


def causal_depthwise_conv(x, w, b):
    y = lax.conv_general_dilated(x, w[:, None, :], window_strides=(1,),
                                 padding=((CONV_WIDTH - 1, 0),),
                                 dimension_numbers=('NWC', 'WIO', 'NWC'),
                                 feature_group_count=x.shape[-1])
    return y + b


def rg_lru(x, wa, ba, wx, bx, lam):
    B, S, C = x.shape
    xb = x.reshape(B, S, LRU_BLOCKS, LRU_BLOCK_DIM)
    r = jax.nn.sigmoid(jnp.einsum('bsni,nij->bsnj', xb, wa) + ba).reshape(B, S, C)
    i = jax.nn.sigmoid(jnp.einsum('bsni,nij->bsnj', xb, wx) + bx).reshape(B, S, C)
    log_a = -LRU_C * r.astype(jnp.float32) * jax.nn.softplus(-lam.astype(jnp.float32))
    a = jnp.exp(log_a)
    mult = jnp.sqrt(-jnp.expm1(2.0 * log_a))
    bterm = mult * (i * x).astype(jnp.float32)

    def combine(left, right):
        a_l, b_l = left
        a_r, b_r = right
        return a_l * a_r, a_r * b_l + b_r

    _, h = lax.associative_scan(combine, (a, bterm), axis=1)
    return h.astype(x.dtype)


def moba_attention(q, k, v):
    B, H, S, dh = q.shape
    scale = dh ** -0.5
    nb = -(-S // MOBA_BLOCK)
    pad = nb * MOBA_BLOCK - S
    kp = jnp.pad(k, ((0, 0), (0, 0), (0, pad), (0, 0))).reshape(B, H, nb, MOBA_BLOCK, dh)
    vp = jnp.pad(v, ((0, 0), (0, 0), (0, pad), (0, 0))).reshape(B, H, nb, MOBA_BLOCK, dh)
    centroid = jnp.mean(kp.astype(jnp.float32), axis=3)
    t = jnp.arange(S)
    qblk = t // MOBA_BLOCK
    gate = jnp.einsum('bhsd,bhnd->bhsn', q.astype(jnp.float32), centroid)
    past = jnp.arange(nb)[None, :] < qblk[:, None]
    gate = jnp.where(past, gate, -jnp.inf)
    _, top_idx = lax.top_k(gate, min(MOBA_TOPK, nb))
    own = jnp.broadcast_to(qblk[:, None], (B, H, S, 1)).astype(top_idx.dtype)
    sel_idx = jnp.concatenate([top_idx, own], axis=-1)
    sel_valid = jnp.concatenate([top_idx < qblk[:, None], jnp.ones((B, H, S, 1), bool)], axis=-1)

    nq = S // Q_BLOCK

    def to_blocks(a):
        return jnp.moveaxis(a.reshape(B, H, nq, Q_BLOCK, *a.shape[3:]), 2, 0)

    b_ix = jnp.arange(B)[:, None, None, None]
    h_ix = jnp.arange(H)[None, :, None, None]
    offs = jnp.arange(MOBA_BLOCK)

    def step(args):
        qb, idx, valid, t0 = args
        kg = kp[b_ix, h_ix, idx]
        vg = vp[b_ix, h_ix, idx]
        s = jnp.einsum('bhqd,bhqnkd->bhqnk', qb, kg).astype(jnp.float32) * scale
        tq = t0 + jnp.arange(Q_BLOCK)
        kpos = idx[..., None] * MOBA_BLOCK + offs
        mask = valid[..., None] & (kpos <= tq[:, None, None])
        p = masked_softmax(s.reshape(B, H, Q_BLOCK, -1), mask.reshape(B, H, Q_BLOCK, -1)).reshape(s.shape)
        return jnp.einsum('bhqnk,bhqnkd->bhqd', p.astype(v.dtype), vg)

    out = lax.map(step, (to_blocks(q), to_blocks(sel_idx), to_blocks(sel_valid), jnp.arange(nq) * Q_BLOCK))
    return jnp.moveaxis(out, 0, 2).reshape(B, H, S, dh)


def mixer_lru_moba(h, in_w, conv_w, conv_b, wa, ba, wx, bx, lam, out_w, pos):
    B, S, _ = h.shape
    u = h @ in_w
    o1 = LRU_WIDTH
    o2 = 2 * LRU_WIDTH
    x_lru, g_lru, q, k, v = jnp.split(u, [o1, o2, o2 + MOBA_WIDTH, o2 + 2 * MOBA_WIDTH], axis=-1)
    xc = causal_depthwise_conv(x_lru, conv_w, conv_b)
    y_lru = rg_lru(xc, wa, ba, wx, bx, lam) * jax.nn.gelu(g_lru)

    def heads(a):
        return a.reshape(B, S, MOBA_HEADS, HEAD_DIM).transpose(0, 2, 1, 3)

    q = rope_partial(heads(q), pos)
    k = rope_partial(heads(k), pos)
    y_att = moba_attention(q, k, heads(v)).transpose(0, 2, 1, 3).reshape(B, S, MOBA_WIDTH)
    return jnp.concatenate([y_lru, y_att], axis=-1) @ out_w


def compress(kv, pos_emb, w1, w2):
    B, G, S, dh = kv.shape
    c = kv.reshape(B, G, S // CMP_STRIDE, CMP_STRIDE, dh)
    blocks = jnp.concatenate([c[:, :, :-1], c[:, :, 1:]], axis=3)
    nc = blocks.shape[2]
    blocks = (blocks + pos_emb).reshape(B, G, nc, CMP_LEN * dh)
    return jax.nn.gelu(blocks @ w1) @ w2


def mixer_nsa(h, in_w, cmp_pos, cmp_w1, cmp_w2, out_w, pos):
    B, S, _ = h.shape
    G, R, dh = NSA_KV_GROUPS, NSA_HPG, HEAD_DIM
    scale = dh ** -0.5
    u = h @ in_w
    cuts = [NSA_QD + i * NSA_KVD for i in range(7)]
    q, kc, vc, ks, vs, kw, vw, gl = jnp.split(u, cuts, axis=-1)
    q = q.reshape(B, S, G, R, dh).transpose(0, 2, 3, 1, 4)

    def kvh(a):
        return a.reshape(B, S, G, dh).transpose(0, 2, 1, 3)

    q_rope = rope_partial(q, pos)
    kcmp = compress(kvh(kc), cmp_pos[0], cmp_w1[0], cmp_w2[0])
    vcmp = compress(kvh(vc), cmp_pos[1], cmp_w1[1], cmp_w2[1])
    nc = kcmp.shape[2]
    ns = S // SEL_BLOCK
    ks_blocks = rope_partial(kvh(ks), pos).reshape(B, G, ns, SEL_BLOCK, dh)
    vs_blocks = kvh(vs).reshape(B, G, ns, SEL_BLOCK, dh)
    kw_pad = jnp.pad(rope_partial(kvh(kw), pos), ((0, 0), (0, 0), (WINDOW, 0), (0, 0)))
    vw_pad = jnp.pad(kvh(vw), ((0, 0), (0, 0), (WINDOW, 0), (0, 0)))
    gates = jax.nn.sigmoid(gl).reshape(B, S, NSA_HEADS, 3)

    cmp_end = jnp.arange(nc) * CMP_STRIDE + CMP_LEN - 1
    ci = jnp.arange(nc)[:, None] * CMP_STRIDE
    sj = jnp.arange(ns)[None, :] * SEL_BLOCK
    overlap = jnp.clip(jnp.minimum(ci + CMP_LEN, sj + SEL_BLOCK) - jnp.maximum(ci, sj), 0, None).astype(jnp.float32) / CMP_LEN
    n_sel = min(SEL_TOPK, ns)
    b_ix = jnp.arange(B)[:, None, None, None]
    g_ix = jnp.arange(G)[None, :, None, None]
    offs = jnp.arange(SEL_BLOCK)
    jn = jnp.arange(ns)
    nq = S // Q_BLOCK

    def to_blocks(a):
        return jnp.moveaxis(a.reshape(B, G, R, nq, Q_BLOCK, dh), 3, 0)

    def step(args):
        qn, qr, t0 = args
        tq = t0 + jnp.arange(Q_BLOCK)
        s_c = jnp.einsum('bgrqd,bgcd->bgrqc', qn, kcmp).astype(jnp.float32) * scale
        p_c = masked_softmax(s_c, cmp_end[None, :] <= tq[:, None])
        o_c = jnp.einsum('bgrqc,bgcd->bgrqd', p_c.astype(vcmp.dtype), vcmp)
        imp = jnp.einsum('bgrqc,cn->bgqn', p_c, overlap)
        blk_t = tq // SEL_BLOCK
        forced = (jn[None, :] == 0) | (jn[None, :] == blk_t[:, None]) | (jn[None, :] == blk_t[:, None] - 1)
        imp = jnp.where(forced, BIG, imp)
        imp = jnp.where(jn[None, :] <= blk_t[:, None], imp, -jnp.inf)
        _, idx = lax.top_k(imp, n_sel)
        kg = ks_blocks[b_ix, g_ix, idx]
        vg = vs_blocks[b_ix, g_ix, idx]
        s_s = jnp.einsum('bgrqd,bgqnkd->bgrqnk', qr, kg).astype(jnp.float32) * scale
        kpos = idx[..., None] * SEL_BLOCK + offs
        m_s = (kpos <= tq[:, None, None])[:, :, None]
        p_s = masked_softmax(s_s.reshape(B, G, R, Q_BLOCK, -1), m_s.reshape(B, G, 1, Q_BLOCK, -1)).reshape(s_s.shape)
        o_s = jnp.einsum('bgrqnk,bgqnkd->bgrqd', p_s.astype(vg.dtype), vg)
        kwb = lax.dynamic_slice_in_dim(kw_pad, t0, Q_BLOCK + WINDOW, axis=2)
        vwb = lax.dynamic_slice_in_dim(vw_pad, t0, Q_BLOCK + WINDOW, axis=2)
        s_w = jnp.einsum('bgrqd,bgkd->bgrqk', qr, kwb).astype(jnp.float32) * scale
        kpos_w = t0 - WINDOW + jnp.arange(Q_BLOCK + WINDOW)
        diff = tq[:, None] - kpos_w[None, :]
        m_w = (diff >= 0) & (diff < WINDOW) & (kpos_w[None, :] >= 0)
        p_w = masked_softmax(s_w, m_w)
        o_w = jnp.einsum('bgrqk,bgkd->bgrqd', p_w.astype(vwb.dtype), vwb)
        return jnp.stack([o_c, o_s, o_w], axis=-2)

    out = lax.map(step, (to_blocks(q), to_blocks(q_rope), jnp.arange(nq) * Q_BLOCK))
    out = out.transpose(1, 0, 4, 2, 3, 5, 6).reshape(B, S, NSA_HEADS, 3, dh)
    y = jnp.einsum('bshc,bshcd->bshd', gates, out).reshape(B, S, MIX1_WIDTH)
    return y @ out_w


def setup_inputs(seed: int = 0) -> dict:
    key = jax.random.key(seed)
    ks = jax.random.split(key, 24)
    f32 = jnp.float32
    D = D_MODEL

    def nrm(k, shape, s):
        return s * jax.random.normal(k, shape, f32)

    a8 = jax.random.uniform(ks[14], (N_EVEN, LRU_WIDTH), f32, 0.9, 0.999)
    sig = a8 ** (1.0 / LRU_C)
    return {
        "x": nrm(ks[0], (BATCH, SEQ, D), 1.0),
        "c": nrm(ks[1], (BATCH, D), 1.0),
        "mod_w": nrm(ks[2], (DEPTH, D, 9 * D), 0.5 * D ** -0.5),
        "mod_b": nrm(ks[3], (DEPTH, 9 * D), 0.02),
        "norm_g": 1.0 + nrm(ks[4], (DEPTH, 3, D), 0.05),
        "ffn_w1": nrm(ks[5], (DEPTH, 2, D, 2 * D_FF), D ** -0.5),
        "ffn_w2": nrm(ks[6], (DEPTH, 2, D_FF, D), D_FF ** -0.5),
        "mix0_in_w": nrm(ks[7], (N_EVEN, D, IN0_WIDTH), D ** -0.5),
        "lru_conv_w": nrm(ks[8], (N_EVEN, CONV_WIDTH, LRU_WIDTH), CONV_WIDTH ** -0.5),
        "lru_conv_b": nrm(ks[9], (N_EVEN, LRU_WIDTH), 0.02),
        "lru_wa": nrm(ks[10], (N_EVEN, LRU_BLOCKS, LRU_BLOCK_DIM, LRU_BLOCK_DIM), LRU_BLOCK_DIM ** -0.5),
        "lru_ba": nrm(ks[11], (N_EVEN, LRU_BLOCKS, LRU_BLOCK_DIM), 0.02),
        "lru_wx": nrm(ks[12], (N_EVEN, LRU_BLOCKS, LRU_BLOCK_DIM, LRU_BLOCK_DIM), LRU_BLOCK_DIM ** -0.5),
        "lru_bx": nrm(ks[13], (N_EVEN, LRU_BLOCKS, LRU_BLOCK_DIM), 0.02),
        "lru_lambda": jnp.log(sig) - jnp.log1p(-sig),
        "mix0_out_w": nrm(ks[15], (N_EVEN, MIX0_WIDTH, D), MIX0_WIDTH ** -0.5),
        "mix1_in_w": nrm(ks[16], (N_ODD, D, IN1_WIDTH), D ** -0.5),
        "cmp_pos": nrm(ks[17], (N_ODD, 2, CMP_LEN, HEAD_DIM), 0.1),
        "cmp_w1": nrm(ks[18], (N_ODD, 2, CMP_LEN * HEAD_DIM, CMP_HIDDEN), (CMP_LEN * HEAD_DIM) ** -0.5),
        "cmp_w2": nrm(ks[19], (N_ODD, 2, CMP_HIDDEN, HEAD_DIM), CMP_HIDDEN ** -0.5),
        "mix1_out_w": nrm(ks[20], (N_ODD, MIX1_WIDTH, D), MIX1_WIDTH ** -0.5),
        "final_norm_g": 1.0 + nrm(ks[21], (D,), 0.05),
    }


def reference(x, c, mod_w, mod_b, norm_g, ffn_w1, ffn_w2, mix0_in_w, lru_conv_w, lru_conv_b,
              lru_wa, lru_ba, lru_wx, lru_bx, lru_lambda, mix0_out_w, mix1_in_w, cmp_pos,
              cmp_w1, cmp_w2, mix1_out_w, final_norm_g):
    B, S, D = x.shape
    pos = jnp.arange(S, dtype=jnp.float32)
    cond = jax.nn.silu(c)
    for l in range(DEPTH):
        mod = (cond @ mod_w[l] + mod_b[l]).reshape(B, 9, D)
        h = modulate(rmsnorm(x, norm_g[l, 0]), mod[:, 0], mod[:, 1])
        x = x + 0.5 * mod[:, 2][:, None, :] * swiglu(h, ffn_w1[l, 0], ffn_w2[l, 0])
        h = modulate(rmsnorm(x, norm_g[l, 1]), mod[:, 3], mod[:, 4])
        j = l // 2
        if l % 2 == 0:
            y = mixer_lru_moba(h, mix0_in_w[j], lru_conv_w[j], lru_conv_b[j], lru_wa[j], lru_ba[j],
                               lru_wx[j], lru_bx[j], lru_lambda[j], mix0_out_w[j], pos)
        else:
            y = mixer_nsa(h, mix1_in_w[j], cmp_pos[j], cmp_w1[j], cmp_w2[j], mix1_out_w[j], pos)
        x = x + mod[:, 5][:, None, :] * y
        h = modulate(rmsnorm(x, norm_g[l, 2]), mod[:, 6], mod[:, 7])
        x = x + 0.5 * mod[:, 8][:, None, :] * swiglu(h, ffn_w1[l, 1], ffn_w2[l, 1])
    return rmsnorm(x, final_norm_g)
```

```python
import functools
import math

import jax
import jax.numpy as jnp
from jax import lax
from jax.experimental import pallas as pl
from jax.experimental.pallas import tpu as pltpu

F32 = jnp.float32
BF16 = jnp.bfloat16

D_MODEL = 1024
D_FF = 2816
EPS = 1e-6
ROPE_THETA = 500000.0
HEAD_DIM = 64
ROT_DIM = HEAD_DIM // 4
NEG = -1e30
BIG = 1e30
LANES = 128

LRU_WIDTH = D_MODEL // 2
LRU_BLOCKS = 8
CONV_WIDTH = 4
LRU_C = 8.0
MOBA_WIDTH = D_MODEL // 2
MOBA_BLOCK = 256
MOBA_TOPK = 3

NSA_HEADS = D_MODEL // HEAD_DIM
NSA_KV_GROUPS = 2
NSA_HPG = NSA_HEADS // NSA_KV_GROUPS
CMP_LEN = 32
CMP_STRIDE = 16
SEL_BLOCK = 64
SEL_TOPK = 16
WINDOW = 512
NSA_QD = NSA_HEADS * HEAD_DIM
NSA_KVD = NSA_KV_GROUPS * HEAD_DIM
IN1_PAD = NSA_QD + 6 * NSA_KVD + LANES

VMEM_LIMIT = 52 * 1024 * 1024

NT_DIMS = (((1,), (1,)), ((), ()))


def _params(sem):
    return pltpu.CompilerParams(dimension_semantics=sem, vmem_limit_bytes=VMEM_LIMIT)


def _resident(shape):
    nd = len(shape)
    return pl.BlockSpec(shape, lambda *_: (0,) * nd, pipeline_mode=pl.Buffered(1))


def _silu(a):
    return a * jax.nn.sigmoid(a)


def _gelu_tanh(x):
    cdf = 0.5 * (1.0 + jnp.tanh(math.sqrt(2.0 / math.pi) * (x + 0.044715 * (x * x * x))))
    return x * cdf


def _rmsnorm(x, g):
    return x * lax.rsqrt(jnp.mean(x * x, axis=-1, keepdims=True) + EPS) * g


def _norm_mod(x, g, shift, scale):
    return _rmsnorm(x, g) * (1.0 + scale) + shift


def _rope(u, cos, sa, sb):
    outs = []
    for j in range(u.shape[1] // LANES):
        c = u[:, j * LANES:(j + 1) * LANES]
        up = pltpu.roll(c, LANES - ROT_DIM // 2, axis=1)
        dn = pltpu.roll(c, ROT_DIM // 2, axis=1)
        outs.append(c * cos + up * sa + dn * sb)
    return jnp.concatenate(outs, axis=1)


def _mod_kernel(c_ref, w_ref, b_ref, o_ref):
    cond = _silu(c_ref[...])
    o_ref[0] = jnp.dot(cond.astype(BF16), w_ref[0].astype(BF16), preferred_element_type=F32) + b_ref[0]


def _modulation(c, mod_w, mod_b):
    depth, d, n = mod_w.shape
    bsz = c.shape[0]
    bp = -(-bsz // 8) * 8
    cp = jnp.pad(c, ((0, bp - bsz), (0, 0)))
    tn = 1152
    out = pl.pallas_call(
        _mod_kernel,
        out_shape=jax.ShapeDtypeStruct((depth, bp, n), F32),
        grid=(depth, n // tn),
        in_specs=[
            pl.BlockSpec((bp, d), lambda l, j: (0, 0)),
            pl.BlockSpec((1, d, tn), lambda l, j: (l, 0, j)),
            pl.BlockSpec((1, 1, tn), lambda l, j: (l, 0, j)),
        ],
        out_specs=pl.BlockSpec((1, bp, tn), lambda l, j: (l, 0, j)),
        compiler_params=_params(("parallel", "parallel")),
        name="adaln_mod",
    )(cp, mod_w, mod_b.reshape(depth, 1, n))
    return out[:, :bsz].reshape(depth, bsz, 9, d)


def _ffn_kernel(x_ref, mod_ref, g_ref, w1a_ref, w1b_ref, w2_ref, gf_ref, o_ref, *, row0, final):
    x = x_ref[0]
    m = mod_ref[0]
    h = _norm_mod(x, g_ref[...], m[row0:row0 + 1], m[row0 + 1:row0 + 2]).astype(BF16)
    a = jnp.dot(h, w1a_ref[...], preferred_element_type=F32)
    b = jnp.dot(h, w1b_ref[...], preferred_element_type=F32)
    act = (_silu(a) * b).astype(BF16)
    y = jnp.dot(act, w2_ref[...], preferred_element_type=F32)
    out = x + (0.5 * m[row0 + 2:row0 + 3]) * y
    if final:
        out = _rmsnorm(out, gf_ref[...])
    o_ref[0] = out


def _ffn(x, mod_l, g, w1a, w1b, w2, gf, *, row0, final, tm):
    bsz, s, d = x.shape
    kern = functools.partial(_ffn_kernel, row0=row0, final=final)
    return pl.pallas_call(
        kern,
        out_shape=jax.ShapeDtypeStruct(x.shape, F32),
        grid=(bsz, s // tm),
        in_specs=[
            pl.BlockSpec((1, tm, d), lambda b, i: (b, i, 0)),
            pl.BlockSpec((1, 9, d), lambda b, i: (b, 0, 0)),
            _resident((1, d)),
            _resident(w1a.shape),
            _resident(w1b.shape),
            _resident(w2.shape),
            _resident((1, d)),
        ],
        out_specs=pl.BlockSpec((1, tm, d), lambda b, i: (b, i, 0)),
        compiler_params=_params(("parallel", "parallel")),
        name="ffn_final" if final else "ffn",
    )(x, mod_l, g.reshape(1, d), w1a, w1b, w2, gf.reshape(1, d))


def _inproj0_kernel(x_ref, mod_ref, g_ref, w_ref, cos_ref, sa_ref, sb_ref,
                    xg_ref, q_ref, k_ref, v_ref, cent_ref):
    m = mod_ref[0]
    h = _norm_mod(x_ref[0], g_ref[...], m[3:4], m[4:5]).astype(BF16)
    u = jnp.dot(h, w_ref[...], preferred_element_type=F32)
    o1 = 2 * LRU_WIDTH
    xg_ref[0] = u[:, :o1]
    cos, sa, sb = cos_ref[...], sa_ref[...], sb_ref[...]
    q = _rope(u[:, o1:o1 + MOBA_WIDTH], cos, sa, sb)
    k = _rope(u[:, o1 + MOBA_WIDTH:o1 + 2 * MOBA_WIDTH], cos, sa, sb)
    q_ref[0] = q.astype(BF16)
    k_ref[0] = k.astype(BF16)
    v_ref[0] = u[:, o1 + 2 * MOBA_WIDTH:].astype(BF16)
    cent_ref[0, 0] = jnp.mean(k, axis=0, keepdims=True)


def _inproj0(x, mod_l, g, w, rope_tabs):
    bsz, s, d = x.shape
    tm = MOBA_BLOCK
    nb = s // tm
    n = w.shape[1]
    row = lambda b, i: (b, i, 0)
    tab = pl.BlockSpec((tm, LANES), lambda b, i: (i, 0))
    return pl.pallas_call(
        _inproj0_kernel,
        out_shape=(
            jax.ShapeDtypeStruct((bsz, s, 2 * LRU_WIDTH), F32),
            jax.ShapeDtypeStruct((bsz, s, MOBA_WIDTH), BF16),
            jax.ShapeDtypeStruct((bsz, s, MOBA_WIDTH), BF16),
            jax.ShapeDtypeStruct((bsz, s, MOBA_WIDTH), BF16),
            jax.ShapeDtypeStruct((bsz, nb, 1, MOBA_WIDTH), F32),
        ),
        grid=(bsz, nb),
        in_specs=[
            pl.BlockSpec((1, tm, d), row),
            pl.BlockSpec((1, 9, d), lambda b, i: (b, 0, 0)),
            _resident((1, d)),
            _resident((d, n)),
            tab, tab, tab,
        ],
        out_specs=(
            pl.BlockSpec((1, tm, 2 * LRU_WIDTH), row),
            pl.BlockSpec((1, tm, MOBA_WIDTH), row),
            pl.BlockSpec((1, tm, MOBA_WIDTH), row),
            pl.BlockSpec((1, tm, MOBA_WIDTH), row),
            pl.BlockSpec((1, 1, 1, MOBA_WIDTH), lambda b, i: (b, i, 0, 0)),
        ),
        compiler_params=_params(("parallel", "parallel")),
        name="inproj0",
    )(x, mod_l, g.reshape(1, d), w, *rope_tabs)


def _lru_kernel(xg_ref, cw_ref, cb_ref, wa_ref, ba_ref, wx_ref, bx_ref, lam_ref, y_ref, xbuf, hprev, *, tm):
    c = LRU_WIDTH
    halo = 8

    @pl.when(pl.program_id(1) == 0)
    def _():
        xbuf[0:halo, :] = jnp.zeros((halo, c), F32)
        hprev[...] = jnp.zeros(hprev.shape, F32)

    x = xg_ref[0, :, 0:c]
    gate_in = xg_ref[0, :, c:2 * c]
    xbuf[halo:halo + tm, :] = x
    cw = cw_ref[...]
    xc = cb_ref[...] + cw[CONV_WIDTH - 1:CONV_WIDTH] * x
    for kk in range(CONV_WIDTH - 1):
        back = CONV_WIDTH - 1 - kk
        xc = xc + cw[kk:kk + 1] * xbuf[halo - back:halo - back + tm, :]
    xbuf[0:halo, :] = x[tm - halo:tm, :]

    xcb = xc.astype(BF16)
    r = jax.nn.sigmoid(jnp.dot(xcb, wa_ref[...], preferred_element_type=F32) + ba_ref[...])
    ig = jax.nn.sigmoid(jnp.dot(xcb, wx_ref[...], preferred_element_type=F32) + bx_ref[...])
    nlam = -lam_ref[...]
    softplus = jnp.maximum(nlam, 0.0) + jnp.log1p(jnp.exp(-jnp.abs(nlam)))
    log_a = (-LRU_C * r) * softplus
    a = jnp.exp(log_a)
    mult = jnp.sqrt(-jnp.tanh(log_a) * (a * a + 1.0))
    bt = mult * (ig * xc)

    row = lax.broadcasted_iota(jnp.int32, (tm, c), 0)
    step = 1
    while step < tm:
        a_sh = pltpu.roll(a, step, axis=0)
        b_sh = pltpu.roll(bt, step, axis=0)
        valid = row >= step
        bt = jnp.where(valid, a * b_sh + bt, bt)
        a = jnp.where(valid, a * a_sh, a)
        step *= 2
    h = bt + a * hprev[0:1, :]
    hprev[0:1, :] = h[tm - 1:tm, :]
    y_ref[0] = (h * _gelu_tanh(gate_in)).astype(BF16)


def _lru(xg, cw, cb, wa_bd, ba, wx_bd, bx, lam, *, tm):
    bsz, s, _ = xg.shape
    c = LRU_WIDTH
    vec = _resident((1, c))
    return pl.pallas_call(
        functools.partial(_lru_kernel, tm=tm),
        out_shape=jax.ShapeDtypeStruct((bsz, s, c), BF16),
        grid=(bsz, s // tm),
        in_specs=[
            pl.BlockSpec((1, tm, 2 * c), lambda b, i: (b, i, 0)),
            _resident((CONV_WIDTH, c)), vec,
            _resident((c, c)), vec,
            _resident((c, c)), vec,
            vec,
        ],
        out_specs=pl.BlockSpec((1, tm, c), lambda b, i: (b, i, 0)),
        scratch_shapes=[pltpu.VMEM((tm + 8, c), F32), pltpu.VMEM((8, c), F32)],
        compiler_params=_params(("parallel", "arbitrary")),
        name="rg_lru",
    )(xg, cw, cb.reshape(1, c), wa_bd, ba.reshape(1, c), wx_bd, bx.reshape(1, c), lam.reshape(1, c))


def _moba_kernel(q_ref, k_ref, v_ref, cent_ref, o_ref, *, nb):
    tq = MOBA_BLOCK
    i = pl.program_id(2)
    scale = HEAD_DIM ** -0.5
    q = q_ref[0]
    lane = lax.broadcasted_iota(jnp.int32, (tq, LANES), 1)
    col = lax.broadcasted_iota(jnp.int32, (tq, nb), 1)
    clane = lax.broadcasted_iota(jnp.int32, (nb, LANES), 1)
    cent = cent_ref[0]
    qpos = lax.broadcasted_iota(jnp.int32, (tq, tq), 0)
    kpos = lax.broadcasted_iota(jnp.int32, (tq, tq), 1)
    causal = kpos <= qpos
    outs = []
    for hh in range(2):
        in_head = (lane >= hh * HEAD_DIM) & (lane < (hh + 1) * HEAD_DIM)
        qh = jnp.where(in_head, q, jnp.zeros_like(q))
        centh = jnp.where((clane >= hh * HEAD_DIM) & (clane < (hh + 1) * HEAD_DIM), cent, 0.0)
        gate = lax.dot_general(qh.astype(F32), centh, NT_DIMS, precision=lax.Precision.HIGHEST,
                               preferred_element_type=F32)
        g = jnp.where(col < i, gate, -jnp.inf)
        sel = jnp.zeros((tq, nb), F32)
        for _ in range(MOBA_TOPK):
            mx = jnp.max(g, axis=1, keepdims=True)
            idx = jnp.min(jnp.where(g == mx, col, nb), axis=1, keepdims=True)
            pick = col == idx
            sel = jnp.where(pick & (col < i), 1.0, sel)
            g = jnp.where(pick, -jnp.inf, g)

        def attend(carry, kj, vj, mask):
            m_i, l_i, acc = carry
            s = lax.dot_general(qh, kj, NT_DIMS, preferred_element_type=F32) * scale
            s = jnp.where(mask, s, NEG)
            m_new = jnp.maximum(m_i, jnp.max(s, axis=1, keepdims=True))
            alpha = jnp.exp(m_i - m_new)
            p = jnp.where(mask, jnp.exp(s - m_new), 0.0)
            l_new = alpha * l_i + jnp.sum(p, axis=1, keepdims=True)
            acc = alpha * acc + jnp.dot(p.astype(BF16), vj, preferred_element_type=F32)
            return m_new, l_new, acc

        def body(j, carry):
            k0 = pl.multiple_of(j * tq, tq)
            kj = k_ref[0, pl.ds(k0, tq), :]
            vj = v_ref[0, pl.ds(k0, tq), :]
            selcol = jnp.sum(jnp.where(col == j, sel, 0.0), axis=1, keepdims=True) > 0.5
            return attend(carry, kj, vj, jnp.broadcast_to(selcol, (tq, tq)))

        init = (jnp.full((tq, 1), NEG, F32), jnp.zeros((tq, 1), F32), jnp.zeros((tq, LANES), F32))
        carry = lax.fori_loop(0, i, body, init)
        k0 = pl.multiple_of(i * tq, tq)
        _, l_i, acc = attend(carry, k_ref[0, pl.ds(k0, tq), :], v_ref[0, pl.ds(k0, tq), :], causal)
        outs.append(acc / l_i)
    o_ref[0] = jnp.where(lane < HEAD_DIM, outs[0], outs[1]).astype(BF16)


def _moba(q, k, v, cent):
    bsz, s, w = q.shape
    tq = MOBA_BLOCK
    nb = s // tq
    return pl.pallas_call(
        functools.partial(_moba_kernel, nb=nb),
        out_shape=jax.ShapeDtypeStruct((bsz, s, w), BF16),
        grid=(bsz, w // LANES, nb),
        in_specs=[
            pl.BlockSpec((1, tq, LANES), lambda b, hp, i: (b, i, hp)),
            pl.BlockSpec((1, s, LANES), lambda b, hp, i: (b, 0, hp)),
            pl.BlockSpec((1, s, LANES), lambda b, hp, i: (b, 0, hp)),
            pl.BlockSpec((1, nb, LANES), lambda b, hp, i: (b, 0, hp)),
        ],
        out_specs=pl.BlockSpec((1, tq, LANES), lambda b, hp, i: (b, i, hp)),
        compiler_params=_params(("parallel", "parallel", "parallel")),
        name="moba_attn",
    )(q, k, v, cent)


def _outproj0_kernel(x_ref, mod_ref, ya_ref, yb_ref, wa_ref, wb_ref, o_ref):
    y = jnp.dot(ya_ref[0], wa_ref[...], preferred_element_type=F32)
    y = y + jnp.dot(yb_ref[0], wb_ref[...], preferred_element_type=F32)
    o_ref[0] = x_ref[0] + mod_ref[0][5:6] * y


def _outproj1_kernel(x_ref, mod_ref, y_ref, w_ref, o_ref):
    y = jnp.dot(y_ref[0], w_ref[...], preferred_element_type=F32)
    o_ref[0] = x_ref[0] + mod_ref[0][5:6] * y


def _outproj(x, mod_l, ys, ws, *, tm):
    bsz, s, d = x.shape
    row = lambda b, i: (b, i, 0)
    kern = _outproj0_kernel if len(ys) == 2 else _outproj1_kernel
    return pl.pallas_call(
        kern,
        out_shape=jax.ShapeDtypeStruct(x.shape, F32),
        grid=(bsz, s // tm),
        in_specs=[pl.BlockSpec((1, tm, d), row), pl.BlockSpec((1, 9, d), lambda b, i: (b, 0, 0))]
        + [pl.BlockSpec((1, tm, y.shape[2]), row) for y in ys]
        + [_resident(w.shape) for w in ws],
        out_specs=pl.BlockSpec((1, tm, d), row),
        compiler_params=_params(("parallel", "parallel")),
        name="outproj",
    )(x, mod_l, *ys, *ws)


def _inproj1_kernel(x_ref, mod_ref, g_ref, w_ref, cos_ref, sa_ref, sb_ref,
                    qn_ref, qr_ref, kvc_ref, kv4_ref, gate_ref):
    m = mod_ref[0]
    h = _norm_mod(x_ref[0], g_ref[...], m[3:4], m[4:5]).astype(BF16)
    u = jnp.dot(h, w_ref[...], preferred_element_type=F32)
    cos, sa, sb = cos_ref[...], sa_ref[...], sb_ref[...]
    q = u[:, :NSA_QD]
    qn_ref[0] = q.astype(BF16)
    qr_ref[0] = _rope(q, cos, sa, sb).astype(BF16)
    o = NSA_QD
    kvc_ref[0] = u[:, o:o + 2 * NSA_KVD]
    o += 2 * NSA_KVD
    ks = _rope(u[:, o:o + NSA_KVD], cos, sa, sb)
    vs = u[:, o + NSA_KVD:o + 2 * NSA_KVD]
    kw = _rope(u[:, o + 2 * NSA_KVD:o + 3 * NSA_KVD], cos, sa, sb)
    vw = u[:, o + 3 * NSA_KVD:o + 4 * NSA_KVD]
    kv4_ref[0] = jnp.concatenate([ks, vs, kw, vw], axis=1).astype(BF16)
    gate_ref[0] = jax.nn.sigmoid(u[:, o + 4 * NSA_KVD:])


def _inproj1(x, mod_l, g, w, rope_tabs, *, tm):
    bsz, s, d = x.shape
    n = w.shape[1]
    row = lambda b, i: (b, i, 0)
    tab = pl.BlockSpec((tm, LANES), lambda b, i: (i, 0))
    widths = (NSA_QD, NSA_QD, 2 * NSA_KVD, 4 * NSA_KVD, LANES)
    dtypes = (BF16, BF16, F32, BF16, F32)
    return pl.pallas_call(
        _inproj1_kernel,
        out_shape=tuple(jax.ShapeDtypeStruct((bsz, s, wd), dt) for wd, dt in zip(widths, dtypes)),
        grid=(bsz, s // tm),
        in_specs=[
            pl.BlockSpec((1, tm, d), row),
            pl.BlockSpec((1, 9, d), lambda b, i: (b, 0, 0)),
            _resident((1, d)),
            _resident((d, n)),
            tab, tab, tab,
        ],
        out_specs=tuple(pl.BlockSpec((1, tm, wd), row) for wd in widths),
        compiler_params=_params(("parallel", "parallel")),
        name="inproj1",
    )(x, mod_l, g.reshape(1, d), w, *rope_tabs)


def _compress_kernel(c_ref, pos_ref, w1_ref, w2_ref, o_ref):
    nrow = c_ref.shape[3]
    for kv in range(2):
        outs = []
        for g in range(NSA_KV_GROUPS):
            blk = c_ref[0, kv, g]
            top = (blk + pos_ref[kv, 0:1]).astype(BF16)
            bot = (blk + pos_ref[kv, 1:2]).astype(BF16)
            pa = jnp.dot(top, w1_ref[kv, 0], preferred_element_type=F32)
            pb = jnp.dot(bot, w1_ref[kv, 1], preferred_element_type=F32)
            pre = pa + pltpu.roll(pb, nrow - 1, axis=0)
            hid = _gelu_tanh(pre).astype(BF16)
            outs.append(jnp.dot(hid, w2_ref[kv], preferred_element_type=F32))
        o_ref[0, kv] = jnp.concatenate(outs, axis=1).astype(BF16)


def _compress(cblk, pos, w1, w2):
    bsz, _, _, nrow, flat = cblk.shape
    return pl.pallas_call(
        _compress_kernel,
        out_shape=jax.ShapeDtypeStruct((bsz, 2, nrow, NSA_KVD), BF16),
        grid=(bsz,),
        in_specs=[
            pl.BlockSpec((1, 2, NSA_KV_GROUPS, nrow, flat), lambda b: (b, 0, 0, 0, 0)),
            _resident(pos.shape),
            _resident(w1.shape),
            _resident(w2.shape),
        ],
        out_specs=pl.BlockSpec((1, 2, nrow, NSA_KVD), lambda b: (b, 0, 0, 0)),
        compiler_params=_params(("parallel",)),
        name="nsa_compress",
    )(cblk, pos, w1, w2)


def _nsa_kernel(qn_ref, qr_ref, kv4_ref, cmp_ref, gate_ref, ovl_ref, o_ref, *, seq, tk):
    tq = 128
    nrow = cmp_ref.shape[2]
    ns = seq // SEL_BLOCK
    r_heads = NSA_HPG
    rows = r_heads * tq
    scale = HEAD_DIM ** -0.5
    t0 = pl.program_id(1) * tq
    lane = lax.broadcasted_iota(jnp.int32, (tq, LANES), 1)
    gates = gate_ref[0]

    def softmax_rows(s3, mask):
        s3 = jnp.where(mask[None], s3, NEG)
        e = jnp.exp(s3 - jnp.max(s3, axis=-1, keepdims=True))
        p = e / jnp.sum(e, axis=-1, keepdims=True)
        return jnp.where(mask[None], p, 0.0)

    per_group = []
    for g in range(NSA_KV_GROUPS):
        in_grp = (lane >= g * HEAD_DIM) & (lane < (g + 1) * HEAD_DIM)

        def stack(q_ref):
            parts = []
            for r in range(r_heads):
                ch = q_ref[0, :, r * LANES:(r + 1) * LANES]
                parts.append(jnp.where(in_grp, ch, jnp.zeros_like(ch)))
            return jnp.concatenate(parts, axis=0)

        qn = stack(qn_ref)
        qr = stack(qr_ref)

        tq_c = t0 + lax.broadcasted_iota(jnp.int32, (tq, nrow), 0)
        cend = lax.broadcasted_iota(jnp.int32, (tq, nrow), 1) * CMP_STRIDE + (CMP_LEN - 1)
        cmask = cend <= tq_c
        s_c = lax.dot_general(qn, cmp_ref[0, 0], NT_DIMS, preferred_element_type=F32) * scale
        p_c = softmax_rows(s_c.reshape(r_heads, tq, nrow), cmask)
        o_c = jnp.dot(p_c.reshape(rows, nrow).astype(BF16), cmp_ref[0, 1], preferred_element_type=F32)

        imp = jnp.dot(jnp.sum(p_c, axis=0), ovl_ref[...], precision=lax.Precision.HIGHEST,
                      preferred_element_type=F32)
        jn = lax.broadcasted_iota(jnp.int32, (tq, ns), 1)
        blk_t = (t0 + lax.broadcasted_iota(jnp.int32, (tq, ns), 0)) // SEL_BLOCK
        forced = (jn == 0) | (jn == blk_t) | (jn == blk_t - 1)
        imp = jnp.where(forced, BIG, imp)
        imp = jnp.where(jn <= blk_t, imp, -jnp.inf)
        sel = jnp.zeros((tq, ns), F32)
        for _ in range(min(SEL_TOPK, ns)):
            mx = jnp.max(imp, axis=1, keepdims=True)
            idx = jnp.min(jnp.where(imp == mx, jn, ns), axis=1, keepdims=True)
            pick = jn == idx
            sel = jnp.where(pick, 1.0, sel)
            imp = jnp.where(pick, -jnp.inf, imp)
        sel = jnp.where(jn <= blk_t, sel, 0.0).astype(BF16)

        tq_s = t0 + lax.broadcasted_iota(jnp.int32, (tq, tk), 0)
        kcol = lax.broadcasted_iota(jnp.int32, (tq, tk), 1)
        jrow = lax.broadcasted_iota(jnp.int32, (ns, tk), 0)
        kblk = lax.broadcasted_iota(jnp.int32, (ns, tk), 1) // SEL_BLOCK

        def sel_body(kt, carry):
            m_i, l_i, acc = carry
            k0 = pl.multiple_of(kt * tk, tk)
            ks = kv4_ref[0, pl.ds(k0, tk), 0:LANES]
            vs = kv4_ref[0, pl.ds(k0, tk), LANES:2 * LANES]
            expand = jnp.where(jrow == kblk + kt * (tk // SEL_BLOCK), 1.0, 0.0).astype(BF16)
            member = jnp.dot(sel, expand, preferred_element_type=F32)
            mask = (member > 0.5) & (kcol + k0 <= tq_s)
            s = lax.dot_general(qr, ks, NT_DIMS, preferred_element_type=F32) * scale
            s = jnp.where(mask[None], s.reshape(r_heads, tq, tk), NEG)
            m_new = jnp.maximum(m_i, jnp.max(s, axis=-1, keepdims=True))
            alpha = jnp.exp(m_i - m_new)
            p = jnp.where(mask[None], jnp.exp(s - m_new), 0.0)
            l_new = alpha * l_i + jnp.sum(p, axis=-1, keepdims=True)
            pv = jnp.dot(p.reshape(rows, tk).astype(BF16), vs, preferred_element_type=F32)
            acc = alpha.reshape(rows, 1) * acc + pv
            return m_new, l_new, acc

        init = (jnp.full((r_heads, tq, 1), NEG, F32), jnp.zeros((r_heads, tq, 1), F32),
                jnp.zeros((rows, LANES), F32))
        _, l_s, acc_s = lax.fori_loop(0, t0 // tk + 1, sel_body, init)
        o_s = acc_s / l_s.reshape(rows, 1)

        wk = WINDOW + tq
        kstart = pl.multiple_of(jnp.clip(t0 - WINDOW, 0, seq - wk), tq)
        kw = kv4_ref[0, pl.ds(kstart, wk), 2 * LANES:3 * LANES]
        vw = kv4_ref[0, pl.ds(kstart, wk), 3 * LANES:4 * LANES]
        diff = (t0 + lax.broadcasted_iota(jnp.int32, (tq, wk), 0)) - (
            kstart + lax.broadcasted_iota(jnp.int32, (tq, wk), 1))
        wmask = (diff >= 0) & (diff < WINDOW)
        s_w = lax.dot_general(qr, kw, NT_DIMS, preferred_element_type=F32) * scale
        p_w = softmax_rows(s_w.reshape(r_heads, tq, wk), wmask)
        o_w = jnp.dot(p_w.reshape(rows, wk).astype(BF16), vw, preferred_element_type=F32)

        heads = []
        for r in range(r_heads):
            hcol = 3 * (g * r_heads + r)
            sl = slice(r * tq, (r + 1) * tq)
            heads.append(gates[:, hcol:hcol + 1] * o_c[sl] + gates[:, hcol + 1:hcol + 2] * o_s[sl]
                         + gates[:, hcol + 2:hcol + 3] * o_w[sl])
        per_group.append(heads)

    for r in range(r_heads):
        o_ref[0, :, r * LANES:(r + 1) * LANES] = jnp.where(
            lane < HEAD_DIM, per_group[0][r], per_group[1][r]).astype(BF16)


def _nsa(qn, qr, kv4, cmp, gates, ovl, *, tk):
    bsz, s, _ = qn.shape
    tq = 128
    row = lambda b, i: (b, i, 0)
    return pl.pallas_call(
        functools.partial(_nsa_kernel, seq=s, tk=tk),
        out_shape=jax.ShapeDtypeStruct((bsz, s, NSA_QD), BF16),
        grid=(bsz, s // tq),
        in_specs=[
            pl.BlockSpec((1, tq, NSA_QD), row),
            pl.BlockSpec((1, tq, NSA_QD), row),
            pl.BlockSpec((1, s, 4 * NSA_KVD), lambda b, i: (b, 0, 0)),
            pl.BlockSpec((1, 2) + cmp.shape[2:], lambda b, i: (b, 0, 0, 0)),
            pl.BlockSpec((1, tq, LANES), row),
            _resident(ovl.shape),
        ],
        out_specs=pl.BlockSpec((1, tq, NSA_QD), row),
        compiler_params=_params(("parallel", "parallel")),
        name="nsa_attn",
    )(qn, qr, kv4, cmp, gates, ovl)


def _rope_tables(s):
    half = ROT_DIM // 2
    pos = jnp.arange(s, dtype=F32)
    freqs = ROPE_THETA ** (-jnp.arange(half, dtype=F32) * 2.0 / ROT_DIM)
    ang = pos[:, None] * freqs[None, :]
    cos, sin = jnp.cos(ang), jnp.sin(ang)
    ones = jnp.ones((s, HEAD_DIM - ROT_DIM), F32)
    zeros = jnp.zeros((s, HEAD_DIM - ROT_DIM), F32)
    zh = jnp.zeros((s, half), F32)
    cos_t = jnp.concatenate([cos, cos, ones], axis=1)
    sa_t = jnp.concatenate([-sin, zh, zeros], axis=1)
    sb_t = jnp.concatenate([zh, sin, zeros], axis=1)
    return tuple(jnp.tile(t, (1, LANES // HEAD_DIM)) for t in (cos_t, sa_t, sb_t))


def _block_diag(w):
    n, a, b = w.shape
    eye = jnp.eye(n, dtype=w.dtype)
    return (eye[:, None, :, None] * w[:, :, None, :]).reshape(n * a, n * b)


def _nsa_head_perm():
    r = jnp.arange(NSA_HPG)[:, None, None]
    g = jnp.arange(NSA_KV_GROUPS)[None, :, None]
    dd = jnp.arange(HEAD_DIM)[None, None, :]
    return (HEAD_DIM * (NSA_HPG * g + r) + dd).reshape(-1)


def _overlap(nrow, ns):
    ci = jnp.arange(nrow)[:, None] * CMP_STRIDE
    sj = jnp.arange(ns)[None, :] * SEL_BLOCK
    ov = jnp.clip(jnp.minimum(ci + CMP_LEN, sj + SEL_BLOCK) - jnp.maximum(ci, sj), 0, None)
    return ov.astype(F32) / CMP_LEN


def kernel(x, c, mod_w, mod_b, norm_g, ffn_w1, ffn_w2, mix0_in_w, lru_conv_w, lru_conv_b, lru_wa, lru_ba,
           lru_wx, lru_bx, lru_lambda, mix0_out_w, mix1_in_w, cmp_pos, cmp_w1, cmp_w2, mix1_out_w, final_norm_g):
    bsz, s, d = x.shape
    depth = mod_w.shape[0]
    assert d == D_MODEL and s % 512 == 0 and s >= WINDOW + 128
    tm = 512

    mod = _modulation(c, mod_w, mod_b)
    tabs = _rope_tables(s)
    w1a = ffn_w1[..., :D_FF].astype(BF16)
    w1b = ffn_w1[..., D_FF:].astype(BF16)
    w2 = ffn_w2.astype(BF16)

    for l in range(depth):
        j = l // 2
        x = _ffn(x, mod[l], norm_g[l, 0], w1a[l, 0], w1b[l, 0], w2[l, 0], final_norm_g,
                 row0=0, final=False, tm=tm)
        if l % 2 == 0:
            xg, q, k, v, cent = _inproj0(x, mod[l], norm_g[l, 1], mix0_in_w[j].astype(BF16), tabs)
            y_lru = _lru(xg, lru_conv_w[j], lru_conv_b[j],
                         _block_diag(lru_wa[j]).astype(BF16), lru_ba[j].reshape(-1),
                         _block_diag(lru_wx[j]).astype(BF16), lru_bx[j].reshape(-1), lru_lambda[j], tm=256)
            y_att = _moba(q, k, v, cent.reshape(bsz, -1, MOBA_WIDTH))
            wo = mix0_out_w[j].astype(BF16)
            x = _outproj(x, mod[l], (y_lru, y_att), (wo[:LRU_WIDTH], wo[LRU_WIDTH:]), tm=tm)
        else:
            perm = _nsa_head_perm()
            w_in = mix1_in_w[j]
            w_in = jnp.concatenate([w_in[:, :NSA_QD][:, perm], w_in[:, NSA_QD:]], axis=1)
            w_in = jnp.pad(w_in, ((0, 0), (0, IN1_PAD - w_in.shape[1]))).astype(BF16)
            qn, qr, kvc, kv4, gates = _inproj1(x, mod[l], norm_g[l, 1], w_in, tabs, tm=tm)
            nrow = s // CMP_STRIDE
            cblk = kvc.reshape(bsz, nrow, CMP_STRIDE, 2, NSA_KV_GROUPS, HEAD_DIM)
            cblk = cblk.transpose(0, 3, 4, 1, 2, 5).reshape(bsz, 2, NSA_KV_GROUPS, nrow, CMP_STRIDE * HEAD_DIM)
            half = CMP_STRIDE * HEAD_DIM
            cmp = _compress(cblk, cmp_pos[j].reshape(2, 2, half),
                            cmp_w1[j].reshape(2, 2, half, -1).astype(BF16), cmp_w2[j].astype(BF16))
            y = _nsa(qn, qr, kv4, cmp, gates, _overlap(nrow, s // SEL_BLOCK), tk=512)
            x = _outproj(x, mod[l], (y,), (mix1_out_w[j][perm].astype(BF16),), tm=tm)
        x = _ffn(x, mod[l], norm_g[l, 2], w1a[l, 1], w1b[l, 1], w2[l, 1], final_norm_g,
                 row0=6, final=(l == depth - 1), tm=tm)
    return x
```

```python
import functools
import math

import jax
import jax.numpy as jnp
from jax import lax
from jax.experimental import pallas as pl
from jax.experimental.pallas import tpu as pltpu

F32 = jnp.float32
BF16 = jnp.bfloat16

D_MODEL = 1024
D_FF = 2816
EPS = 1e-6
ROPE_THETA = 500000.0
HEAD_DIM = 64
ROT_DIM = HEAD_DIM // 4
NEG = -1e30
BIG = 1e30
LANES = 128

LRU_WIDTH = D_MODEL // 2
LRU_BLOCKS = 8
CONV_WIDTH = 4
LRU_C = 8.0
MOBA_WIDTH = D_MODEL // 2
MOBA_BLOCK = 256
MOBA_TOPK = 3

NSA_HEADS = D_MODEL // HEAD_DIM
NSA_KV_GROUPS = 2
NSA_HPG = NSA_HEADS // NSA_KV_GROUPS
CMP_LEN = 32
CMP_STRIDE = 16
SEL_BLOCK = 64
SEL_TOPK = 16
WINDOW = 512
NSA_QD = NSA_HEADS * HEAD_DIM
NSA_KVD = NSA_KV_GROUPS * HEAD_DIM
IN1_PAD = NSA_QD + 6 * NSA_KVD + LANES

VMEM_LIMIT = 52 * 1024 * 1024

NT_DIMS = (((1,), (1,)), ((), ()))
QK_SCALE = HEAD_DIM ** -0.5
assert math.frexp(QK_SCALE)[0] == 0.5


def _params(sem):
    return pltpu.CompilerParams(dimension_semantics=sem, vmem_limit_bytes=VMEM_LIMIT)


def _resident(shape):
    nd = len(shape)
    return pl.BlockSpec(shape, lambda *_: (0,) * nd, pipeline_mode=pl.Buffered(1))


def _silu(a):
    return a * jax.nn.sigmoid(a)


def _gelu_tanh(x):
    cdf = 0.5 * (1.0 + jnp.tanh(math.sqrt(2.0 / math.pi) * (x + 0.044715 * (x * x * x))))
    return x * cdf


def _rmsnorm(x, g):
    return x * lax.rsqrt(jnp.mean(x * x, axis=-1, keepdims=True) + EPS) * g


def _norm_mod(x, g, shift, scale):
    return _rmsnorm(x, g) * (1.0 + scale) + shift


def _rope(u, cos, sa, sb):
    outs = []
    for j in range(u.shape[1] // LANES):
        c = u[:, j * LANES:(j + 1) * LANES]
        up = pltpu.roll(c, LANES - ROT_DIM // 2, axis=1)
        dn = pltpu.roll(c, ROT_DIM // 2, axis=1)
        outs.append(c * cos + up * sa + dn * sb)
    return jnp.concatenate(outs, axis=1)


def _mod_kernel(c_ref, w_ref, b_ref, o_ref):
    cond = _silu(c_ref[...])
    o_ref[0] = jnp.dot(cond.astype(BF16), w_ref[0].astype(BF16), preferred_element_type=F32) + b_ref[0]


def _modulation(c, mod_w, mod_b):
    depth, d, n = mod_w.shape
    bsz = c.shape[0]
    bp = -(-bsz // 8) * 8
    cp = jnp.pad(c, ((0, bp - bsz), (0, 0)))
    tn = 1152
    out = pl.pallas_call(
        _mod_kernel,
        out_shape=jax.ShapeDtypeStruct((depth, bp, n), F32),
        grid=(depth, n // tn),
        in_specs=[
            pl.BlockSpec((bp, d), lambda l, j: (0, 0)),
            pl.BlockSpec((1, d, tn), lambda l, j: (l, 0, j)),
            pl.BlockSpec((1, 1, tn), lambda l, j: (l, 0, j)),
        ],
        out_specs=pl.BlockSpec((1, bp, tn), lambda l, j: (l, 0, j)),
        compiler_params=_params(("parallel", "parallel")),
        name="adaln_mod",
    )(cp, mod_w, mod_b.reshape(depth, 1, n))
    return out[:, :bsz].reshape(depth, bsz, 9, d)


def _ffn_kernel(x_ref, mod_ref, g_ref, w1a_ref, w1b_ref, w2_ref, gf_ref, o_ref, *, row0, final):
    x = x_ref[0]
    m = mod_ref[0]
    h = _norm_mod(x, g_ref[...], m[row0:row0 + 1], m[row0 + 1:row0 + 2]).astype(BF16)
    a = jnp.dot(h, w1a_ref[...], preferred_element_type=F32)
    b = jnp.dot(h, w1b_ref[...], preferred_element_type=F32)
    act = (_silu(a) * b).astype(BF16)
    y = jnp.dot(act, w2_ref[...], preferred_element_type=F32)
    out = x + (0.5 * m[row0 + 2:row0 + 3]) * y
    if final:
        out = _rmsnorm(out, gf_ref[...])
    o_ref[0] = out


def _ffn(x, mod_l, g, w1a, w1b, w2, gf, *, row0, final, tm):
    bsz, s, d = x.shape
    kern = functools.partial(_ffn_kernel, row0=row0, final=final)
    return pl.pallas_call(
        kern,
        out_shape=jax.ShapeDtypeStruct(x.shape, F32),
        grid=(bsz, s // tm),
        in_specs=[
            pl.BlockSpec((1, tm, d), lambda b, i: (b, i, 0)),
            pl.BlockSpec((1, 9, d), lambda b, i: (b, 0, 0)),
            _resident((1, d)),
            _resident(w1a.shape),
            _resident(w1b.shape),
            _resident(w2.shape),
            _resident((1, d)),
        ],
        out_specs=pl.BlockSpec((1, tm, d), lambda b, i: (b, i, 0)),
        compiler_params=_params(("parallel", "parallel")),
        name="ffn_final" if final else "ffn",
    )(x, mod_l, g.reshape(1, d), w1a, w1b, w2, gf.reshape(1, d))


def _inproj0_kernel(x_ref, mod_ref, g_ref, w_ref, cos_ref, sa_ref, sb_ref,
                    xg_ref, q_ref, k_ref, v_ref, cent_ref):
    m = mod_ref[0]
    h = _norm_mod(x_ref[0], g_ref[...], m[3:4], m[4:5]).astype(BF16)
    u = jnp.dot(h, w_ref[...], preferred_element_type=F32)
    o1 = 2 * LRU_WIDTH
    xg_ref[0] = u[:, :o1]
    cos, sa, sb = cos_ref[...], sa_ref[...], sb_ref[...]
    q = _rope(u[:, o1:o1 + MOBA_WIDTH], cos, sa, sb)
    k = _rope(u[:, o1 + MOBA_WIDTH:o1 + 2 * MOBA_WIDTH], cos, sa, sb)
    q_ref[0] = (q * QK_SCALE).astype(BF16)
    k_ref[0] = k.astype(BF16)
    v_ref[0] = u[:, o1 + 2 * MOBA_WIDTH:].astype(BF16)
    cent_ref[0, 0] = jnp.mean(k, axis=0, keepdims=True)


def _inproj0(x, mod_l, g, w, rope_tabs):
    bsz, s, d = x.shape
    tm = MOBA_BLOCK
    nb = s // tm
    n = w.shape[1]
    row = lambda b, i: (b, i, 0)
    tab = pl.BlockSpec((tm, LANES), lambda b, i: (i, 0))
    return pl.pallas_call(
        _inproj0_kernel,
        out_shape=(
            jax.ShapeDtypeStruct((bsz, s, 2 * LRU_WIDTH), F32),
            jax.ShapeDtypeStruct((bsz, s, MOBA_WIDTH), BF16),
            jax.ShapeDtypeStruct((bsz, s, MOBA_WIDTH), BF16),
            jax.ShapeDtypeStruct((bsz, s, MOBA_WIDTH), BF16),
            jax.ShapeDtypeStruct((bsz, nb, 1, MOBA_WIDTH), F32),
        ),
        grid=(bsz, nb),
        in_specs=[
            pl.BlockSpec((1, tm, d), row),
            pl.BlockSpec((1, 9, d), lambda b, i: (b, 0, 0)),
            _resident((1, d)),
            _resident((d, n)),
            tab, tab, tab,
        ],
        out_specs=(
            pl.BlockSpec((1, tm, 2 * LRU_WIDTH), row),
            pl.BlockSpec((1, tm, MOBA_WIDTH), row),
            pl.BlockSpec((1, tm, MOBA_WIDTH), row),
            pl.BlockSpec((1, tm, MOBA_WIDTH), row),
            pl.BlockSpec((1, 1, 1, MOBA_WIDTH), lambda b, i: (b, i, 0, 0)),
        ),
        compiler_params=_params(("parallel", "parallel")),
        name="inproj0",
    )(x, mod_l, g.reshape(1, d), w, *rope_tabs)


def _lru_kernel(xg_ref, cw_ref, cb_ref, wa_ref, ba_ref, wx_ref, bx_ref, lam_ref, y_ref, xbuf, hprev, *, tm):
    c = LRU_WIDTH
    halo = 8

    @pl.when(pl.program_id(1) == 0)
    def _():
        xbuf[0:halo, :] = jnp.zeros((halo, c), F32)
        hprev[...] = jnp.zeros(hprev.shape, F32)

    x = xg_ref[0, :, 0:c]
    gate_in = xg_ref[0, :, c:2 * c]
    xbuf[halo:halo + tm, :] = x
    cw = cw_ref[...]
    xc = cb_ref[...] + cw[CONV_WIDTH - 1:CONV_WIDTH] * x
    for kk in range(CONV_WIDTH - 1):
        back = CONV_WIDTH - 1 - kk
        xc = xc + cw[kk:kk + 1] * xbuf[halo - back:halo - back + tm, :]
    xbuf[0:halo, :] = x[tm - halo:tm, :]

    xcb = xc.astype(BF16)
    r = jax.nn.sigmoid(jnp.dot(xcb, wa_ref[...], preferred_element_type=F32) + ba_ref[...])
    ig = jax.nn.sigmoid(jnp.dot(xcb, wx_ref[...], preferred_element_type=F32) + bx_ref[...])
    nlam = -lam_ref[...]
    softplus = jnp.maximum(nlam, 0.0) + jnp.log1p(jnp.exp(-jnp.abs(nlam)))
    log_a = (-LRU_C * r) * softplus
    a = jnp.exp(log_a)
    mult = jnp.sqrt(-jnp.tanh(log_a) * (a * a + 1.0))
    bt = mult * (ig * xc)

    row = lax.broadcasted_iota(jnp.int32, (tm, c), 0)
    step = 1
    while step < tm:
        a_sh = pltpu.roll(a, step, axis=0)
        b_sh = pltpu.roll(bt, step, axis=0)
        valid = row >= step
        bt = jnp.where(valid, a * b_sh + bt, bt)
        a = jnp.where(valid, a * a_sh, a)
        step *= 2
    h = bt + a * hprev[0:1, :]
    hprev[0:1, :] = h[tm - 1:tm, :]
    y_ref[0] = (h * _gelu_tanh(gate_in)).astype(BF16)


def _lru(xg, cw, cb, wa_bd, ba, wx_bd, bx, lam, *, tm):
    bsz, s, _ = xg.shape
    c = LRU_WIDTH
    vec = _resident((1, c))
    return pl.pallas_call(
        functools.partial(_lru_kernel, tm=tm),
        out_shape=jax.ShapeDtypeStruct((bsz, s, c), BF16),
        grid=(bsz, s // tm),
        in_specs=[
            pl.BlockSpec((1, tm, 2 * c), lambda b, i: (b, i, 0)),
            _resident((CONV_WIDTH, c)), vec,
            _resident((c, c)), vec,
            _resident((c, c)), vec,
            vec,
        ],
        out_specs=pl.BlockSpec((1, tm, c), lambda b, i: (b, i, 0)),
        scratch_shapes=[pltpu.VMEM((tm + 8, c), F32), pltpu.VMEM((8, c), F32)],
        compiler_params=_params(("parallel", "arbitrary")),
        name="rg_lru",
    )(xg, cw, cb.reshape(1, c), wa_bd, ba.reshape(1, c), wx_bd, bx.reshape(1, c), lam.reshape(1, c))


def _moba_kernel(q_ref, k_ref, v_ref, cent_ref, o_ref, *, nb):
    tq = MOBA_BLOCK
    rows = 2 * tq
    i = pl.program_id(2)
    q = q_ref[0]
    lane = lax.broadcasted_iota(jnp.int32, (tq, LANES), 1)
    zero = jnp.zeros_like(q)
    q2 = jnp.concatenate([jnp.where(lane < HEAD_DIM, q, zero), jnp.where(lane >= HEAD_DIM, q, zero)], axis=0)

    col = lax.broadcasted_iota(jnp.int32, (rows, nb), 1)
    gate = lax.dot_general(q2.astype(F32), cent_ref[0], NT_DIMS, precision=lax.Precision.HIGHEST,
                           preferred_element_type=F32)
    g = jnp.where(col < i, gate, -jnp.inf)
    sel = jnp.zeros((rows, nb), F32)
    for _ in range(MOBA_TOPK):
        mx = jnp.max(g, axis=1, keepdims=True)
        idx = jnp.min(jnp.where(g == mx, col, nb), axis=1, keepdims=True)
        pick = col == idx
        sel = jnp.where(pick & (col < i), 1.0, sel)
        g = jnp.where(pick, -jnp.inf, g)

    def pair_body(jj, carry):
        m_i, l_i, acc = carry
        k0 = pl.multiple_of(jj * rows, rows)
        s = lax.dot_general(q2, k_ref[0, pl.ds(k0, rows), :], NT_DIMS, preferred_element_type=F32)
        halves = []
        for hb in range(2):
            sh = s[:, hb * tq:(hb + 1) * tq]
            chosen = jnp.sum(jnp.where(col == 2 * jj + hb, sel, 0.0), axis=1, keepdims=True) > 0.5
            halves.append((sh, chosen, jnp.where(chosen, jnp.max(sh, axis=1, keepdims=True), NEG)))
        m_new = jnp.maximum(m_i, jnp.maximum(halves[0][2], halves[1][2]))
        alpha = jnp.exp(m_i - m_new)
        ps = [jnp.exp(sh - jnp.where(chosen, m_new, BIG)) for sh, chosen, _ in halves]
        l_new = alpha * l_i + jnp.sum(ps[0], axis=1, keepdims=True) + jnp.sum(ps[1], axis=1, keepdims=True)
        p = jnp.concatenate(ps, axis=1).astype(BF16)
        acc = alpha * acc + jnp.dot(p, v_ref[0, pl.ds(k0, rows), :], preferred_element_type=F32)
        return m_new, l_new, acc

    init = (jnp.full((rows, 1), NEG, F32), jnp.zeros((rows, 1), F32), jnp.zeros((rows, LANES), F32))
    m_i, l_i, acc = lax.fori_loop(0, (i + 1) // 2, pair_body, init)

    k0 = pl.multiple_of(i * tq, tq)
    qpos = lax.broadcasted_iota(jnp.int32, (tq, tq), 0)
    kpos = lax.broadcasted_iota(jnp.int32, (tq, tq), 1)
    bias = jnp.where(kpos <= qpos, 0.0, NEG)
    s = lax.dot_general(q2, k_ref[0, pl.ds(k0, tq), :], NT_DIMS, preferred_element_type=F32)
    s = s + jnp.concatenate([bias, bias], axis=0)
    m_new = jnp.maximum(m_i, jnp.max(s, axis=1, keepdims=True))
    alpha = jnp.exp(m_i - m_new)
    p = jnp.exp(s - m_new)
    l_i = alpha * l_i + jnp.sum(p, axis=1, keepdims=True)
    acc = alpha * acc + jnp.dot(p.astype(BF16), v_ref[0, pl.ds(k0, tq), :], preferred_element_type=F32)
    out = acc / l_i
    o_ref[0] = jnp.where(lane < HEAD_DIM, out[:tq], out[tq:]).astype(BF16)


def _moba(q, k, v, cent):
    bsz, s, w = q.shape
    tq = MOBA_BLOCK
    nb = s // tq
    return pl.pallas_call(
        functools.partial(_moba_kernel, nb=nb),
        out_shape=jax.ShapeDtypeStruct((bsz, s, w), BF16),
        grid=(bsz, w // LANES, nb),
        in_specs=[
            pl.BlockSpec((1, tq, LANES), lambda b, hp, i: (b, i, hp)),
            pl.BlockSpec((1, s, LANES), lambda b, hp, i: (b, 0, hp)),
            pl.BlockSpec((1, s, LANES), lambda b, hp, i: (b, 0, hp)),
            pl.BlockSpec((1, nb, LANES), lambda b, hp, i: (b, 0, hp)),
        ],
        out_specs=pl.BlockSpec((1, tq, LANES), lambda b, hp, i: (b, i, hp)),
        compiler_params=_params(("parallel", "parallel", "parallel")),
        name="moba_attn",
    )(q, k, v, cent)


def _outproj0_kernel(x_ref, mod_ref, ya_ref, yb_ref, wa_ref, wb_ref, o_ref):
    y = jnp.dot(ya_ref[0], wa_ref[...], preferred_element_type=F32)
    y = y + jnp.dot(yb_ref[0], wb_ref[...], preferred_element_type=F32)
    o_ref[0] = x_ref[0] + mod_ref[0][5:6] * y


def _outproj1_kernel(x_ref, mod_ref, y_ref, w_ref, o_ref):
    y = jnp.dot(y_ref[0], w_ref[...], preferred_element_type=F32)
    o_ref[0] = x_ref[0] + mod_ref[0][5:6] * y


def _outproj(x, mod_l, ys, ws, *, tm):
    bsz, s, d = x.shape
    row = lambda b, i: (b, i, 0)
    kern = _outproj0_kernel if len(ys) == 2 else _outproj1_kernel
    return pl.pallas_call(
        kern,
        out_shape=jax.ShapeDtypeStruct(x.shape, F32),
        grid=(bsz, s // tm),
        in_specs=[pl.BlockSpec((1, tm, d), row), pl.BlockSpec((1, 9, d), lambda b, i: (b, 0, 0))]
        + [pl.BlockSpec((1, tm, y.shape[2]), row) for y in ys]
        + [_resident(w.shape) for w in ws],
        out_specs=pl.BlockSpec((1, tm, d), row),
        compiler_params=_params(("parallel", "parallel")),
        name="outproj",
    )(x, mod_l, *ys, *ws)


def _inproj1_kernel(x_ref, mod_ref, g_ref, w_ref, cos_ref, sa_ref, sb_ref,
                    qn_ref, qr_ref, kvc_ref, kv4_ref, gate_ref):
    m = mod_ref[0]
    h = _norm_mod(x_ref[0], g_ref[...], m[3:4], m[4:5]).astype(BF16)
    u = jnp.dot(h, w_ref[...], preferred_element_type=F32)
    cos, sa, sb = cos_ref[...], sa_ref[...], sb_ref[...]
    q = u[:, :NSA_QD] * QK_SCALE
    qn_ref[0] = q.astype(BF16)
    qr_ref[0] = _rope(q, cos, sa, sb).astype(BF16)
    o = NSA_QD
    kvc_ref[0] = u[:, o:o + 2 * NSA_KVD]
    o += 2 * NSA_KVD
    ks = _rope(u[:, o:o + NSA_KVD], cos, sa, sb)
    vs = u[:, o + NSA_KVD:o + 2 * NSA_KVD]
    kw = _rope(u[:, o + 2 * NSA_KVD:o + 3 * NSA_KVD], cos, sa, sb)
    vw = u[:, o + 3 * NSA_KVD:o + 4 * NSA_KVD]
    kv4_ref[0] = jnp.concatenate([ks, vs, kw, vw], axis=1).astype(BF16)
    gate_ref[0] = jax.nn.sigmoid(u[:, o + 4 * NSA_KVD:])


def _inproj1(x, mod_l, g, w, rope_tabs, *, tm):
    bsz, s, d = x.shape
    n = w.shape[1]
    row = lambda b, i: (b, i, 0)
    tab = pl.BlockSpec((tm, LANES), lambda b, i: (i, 0))
    widths = (NSA_QD, NSA_QD, 2 * NSA_KVD, 4 * NSA_KVD, LANES)
    dtypes = (BF16, BF16, F32, BF16, F32)
    return pl.pallas_call(
        _inproj1_kernel,
        out_shape=tuple(jax.ShapeDtypeStruct((bsz, s, wd), dt) for wd, dt in zip(widths, dtypes)),
        grid=(bsz, s // tm),
        in_specs=[
            pl.BlockSpec((1, tm, d), row),
            pl.BlockSpec((1, 9, d), lambda b, i: (b, 0, 0)),
            _resident((1, d)),
            _resident((d, n)),
            tab, tab, tab,
        ],
        out_specs=tuple(pl.BlockSpec((1, tm, wd), row) for wd in widths),
        compiler_params=_params(("parallel", "parallel")),
        name="inproj1",
    )(x, mod_l, g.reshape(1, d), w, *rope_tabs)


def _compress_kernel(c_ref, pos_ref, w1_ref, w2_ref, o_ref):
    nrow = c_ref.shape[3]
    for kv in range(2):
        outs = []
        for g in range(NSA_KV_GROUPS):
            blk = c_ref[0, kv, g]
            top = (blk + pos_ref[kv, 0:1]).astype(BF16)
            bot = (blk + pos_ref[kv, 1:2]).astype(BF16)
            pa = jnp.dot(top, w1_ref[kv, 0], preferred_element_type=F32)
            pb = jnp.dot(bot, w1_ref[kv, 1], preferred_element_type=F32)
            pre = pa + pltpu.roll(pb, nrow - 1, axis=0)
            hid = _gelu_tanh(pre).astype(BF16)
            outs.append(jnp.dot(hid, w2_ref[kv], preferred_element_type=F32))
        o_ref[0, kv] = jnp.concatenate(outs, axis=1).astype(BF16)


def _compress(cblk, pos, w1, w2):
    bsz, _, _, nrow, flat = cblk.shape
    return pl.pallas_call(
        _compress_kernel,
        out_shape=jax.ShapeDtypeStruct((bsz, 2, nrow, NSA_KVD), BF16),
        grid=(bsz,),
        in_specs=[
            pl.BlockSpec((1, 2, NSA_KV_GROUPS, nrow, flat), lambda b: (b, 0, 0, 0, 0)),
            _resident(pos.shape),
            _resident(w1.shape),
            _resident(w2.shape),
        ],
        out_specs=pl.BlockSpec((1, 2, nrow, NSA_KVD), lambda b: (b, 0, 0, 0)),
        compiler_params=_params(("parallel",)),
        name="nsa_compress",
    )(cblk, pos, w1, w2)


def _nsa_kernel(qn_ref, qr_ref, kv4_ref, cmp_ref, gate_ref, ovl_ref, o_ref, *, seq, tk):
    tq = 128
    nrow = cmp_ref.shape[2]
    ns = seq // SEL_BLOCK
    n_g, n_r = NSA_KV_GROUPS, NSA_HPG
    n_h = n_g * n_r
    rows = n_h * tq
    t0 = pl.program_id(1) * tq
    lane = lax.broadcasted_iota(jnp.int32, (tq, LANES), 1)
    gates = gate_ref[0]

    def stack(q_ref):
        parts = []
        for g in range(n_g):
            in_grp = (lane >= g * HEAD_DIM) & (lane < (g + 1) * HEAD_DIM)
            for r in range(n_r):
                ch = q_ref[0, :, r * LANES:(r + 1) * LANES]
                parts.append(jnp.where(in_grp, ch, jnp.zeros_like(ch)))
        return jnp.concatenate(parts, axis=0)

    qn = stack(qn_ref)
    qr = stack(qr_ref)

    tq_c = t0 + lax.broadcasted_iota(jnp.int32, (tq, nrow), 0)
    cend = lax.broadcasted_iota(jnp.int32, (tq, nrow), 1) * CMP_STRIDE + (CMP_LEN - 1)
    bias_c = jnp.where(cend <= tq_c, 0.0, NEG)
    s_c = lax.dot_general(qn, cmp_ref[0, 0], NT_DIMS, preferred_element_type=F32)
    s_c = s_c.reshape(n_h, tq, nrow) + bias_c[None]
    e_c = jnp.exp(s_c - jnp.max(s_c, axis=-1, keepdims=True))
    has_block = jnp.where(t0 + lax.broadcasted_iota(jnp.int32, (tq, 1), 0) >= CMP_LEN - 1, 1.0, 0.0)
    p_c = e_c * (has_block[None] / jnp.sum(e_c, axis=-1, keepdims=True))
    o_c = jnp.dot(p_c.reshape(rows, nrow).astype(BF16), cmp_ref[0, 1], preferred_element_type=F32)
    p_grp = jnp.concatenate([jnp.sum(p_c[g * n_r:(g + 1) * n_r], axis=0) for g in range(n_g)], axis=0)

    wk = WINDOW + tq
    kstart = pl.multiple_of(jnp.clip(t0 - WINDOW, 0, seq - wk), tq)
    diff = (t0 + lax.broadcasted_iota(jnp.int32, (tq, wk), 0)) - (
        kstart + lax.broadcasted_iota(jnp.int32, (tq, wk), 1))
    bias_w = jnp.where((diff >= 0) & (diff < WINDOW), 0.0, NEG)
    s_w = lax.dot_general(qr, kv4_ref[0, pl.ds(kstart, wk), 2 * LANES:3 * LANES], NT_DIMS,
                          preferred_element_type=F32)
    s_w = s_w.reshape(n_h, tq, wk) + bias_w[None]
    e_w = jnp.exp(s_w - jnp.max(s_w, axis=-1, keepdims=True))
    l_w = jnp.sum(e_w, axis=-1, keepdims=True).reshape(rows, 1)
    acc_w = jnp.dot(e_w.reshape(rows, wk).astype(BF16), kv4_ref[0, pl.ds(kstart, wk), 3 * LANES:4 * LANES],
                    preferred_element_type=F32)

    imp = jnp.dot(p_grp, ovl_ref[...], precision=lax.Precision.HIGHEST, preferred_element_type=F32)
    jn = lax.broadcasted_iota(jnp.int32, (n_g * tq, ns), 1)
    tpos = t0 + lax.broadcasted_iota(jnp.int32, (tq, ns), 0)
    blk_t = jnp.concatenate([tpos] * n_g, axis=0) // SEL_BLOCK
    forced = (jn == 0) | (jn == blk_t) | (jn == blk_t - 1)
    imp = jnp.where(forced, BIG, imp)
    imp = jnp.where(jn <= blk_t, imp, -jnp.inf)
    sel = jnp.zeros((n_g * tq, ns), F32)
    for _ in range(min(SEL_TOPK, ns)):
        mx = jnp.max(imp, axis=1, keepdims=True)
        idx = jnp.min(jnp.where(imp == mx, jn, ns), axis=1, keepdims=True)
        pick = jn == idx
        sel = jnp.where(pick, 1.0, sel)
        imp = jnp.where(pick, -jnp.inf, imp)
    sel = jnp.where(jn <= blk_t, sel, 0.0).astype(BF16)

    tq_s = t0 + lax.broadcasted_iota(jnp.int32, (tq, tk), 0)
    kcol = lax.broadcasted_iota(jnp.int32, (tq, tk), 1)
    jrow = lax.broadcasted_iota(jnp.int32, (ns, tk), 0)
    kblk = lax.broadcasted_iota(jnp.int32, (ns, tk), 1) // SEL_BLOCK

    def sel_body(kt, carry):
        m_i, l_i, acc = carry
        k0 = pl.multiple_of(kt * tk, tk)
        expand = jnp.where(jrow == kblk + kt * (tk // SEL_BLOCK), 1.0, 0.0).astype(BF16)
        member = jnp.dot(sel, expand, preferred_element_type=F32)
        causal = jnp.where(kcol + k0 <= tq_s, 0.0, NEG)
        bias = (member * BIG - BIG).reshape(n_g, tq, tk) + causal[None]
        s = lax.dot_general(qr, kv4_ref[0, pl.ds(k0, tk), 0:LANES], NT_DIMS, preferred_element_type=F32)
        s = s.reshape(n_g, n_r, tq, tk) + bias[:, None]
        m_new = jnp.maximum(m_i, jnp.max(s, axis=-1, keepdims=True))
        alpha = jnp.exp(m_i - m_new)
        p = jnp.exp(s - m_new)
        l_new = alpha * l_i + jnp.sum(p, axis=-1, keepdims=True)
        pv = jnp.dot(p.reshape(rows, tk).astype(BF16), kv4_ref[0, pl.ds(k0, tk), LANES:2 * LANES],
                     preferred_element_type=F32)
        return m_new, l_new, alpha.reshape(rows, 1) * acc + pv

    init = (jnp.full((n_g, n_r, tq, 1), NEG, F32), jnp.zeros((n_g, n_r, tq, 1), F32),
            jnp.zeros((rows, LANES), F32))
    _, l_s, acc_s = lax.fori_loop(0, t0 // tk + 1, sel_body, init)
    inv_s = 1.0 / l_s.reshape(rows, 1)
    inv_w = 1.0 / l_w

    for r in range(n_r):
        halves = []
        for g in range(n_g):
            h = g * n_r + r
            sl = slice(h * tq, (h + 1) * tq)
            halves.append(gates[:, 3 * h:3 * h + 1] * o_c[sl]
                          + (gates[:, 3 * h + 1:3 * h + 2] * inv_s[sl]) * acc_s[sl]
                          + (gates[:, 3 * h + 2:3 * h + 3] * inv_w[sl]) * acc_w[sl])
        o_ref[0, :, r * LANES:(r + 1) * LANES] = jnp.where(lane < HEAD_DIM, halves[0], halves[1]).astype(BF16)


def _nsa(qn, qr, kv4, cmp, gates, ovl, *, tk):
    bsz, s, _ = qn.shape
    tq = 128
    row = lambda b, i: (b, i, 0)
    return pl.pallas_call(
        functools.partial(_nsa_kernel, seq=s, tk=tk),
        out_shape=jax.ShapeDtypeStruct((bsz, s, NSA_QD), BF16),
        grid=(bsz, s // tq),
        in_specs=[
            pl.BlockSpec((1, tq, NSA_QD), row),
            pl.BlockSpec((1, tq, NSA_QD), row),
            pl.BlockSpec((1, s, 4 * NSA_KVD), lambda b, i: (b, 0, 0)),
            pl.BlockSpec((1, 2) + cmp.shape[2:], lambda b, i: (b, 0, 0, 0)),
            pl.BlockSpec((1, tq, LANES), row),
            _resident(ovl.shape),
        ],
        out_specs=pl.BlockSpec((1, tq, NSA_QD), row),
        compiler_params=_params(("parallel", "parallel")),
        name="nsa_attn",
    )(qn, qr, kv4, cmp, gates, ovl)


def _rope_tables(s):
    half = ROT_DIM // 2
    pos = jnp.arange(s, dtype=F32)
    freqs = ROPE_THETA ** (-jnp.arange(half, dtype=F32) * 2.0 / ROT_DIM)
    ang = pos[:, None] * freqs[None, :]
    cos, sin = jnp.cos(ang), jnp.sin(ang)
    ones = jnp.ones((s, HEAD_DIM - ROT_DIM), F32)
    zeros = jnp.zeros((s, HEAD_DIM - ROT_DIM), F32)
    zh = jnp.zeros((s, half), F32)
    cos_t = jnp.concatenate([cos, cos, ones], axis=1)
    sa_t = jnp.concatenate([-sin, zh, zeros], axis=1)
    sb_t = jnp.concatenate([zh, sin, zeros], axis=1)
    return tuple(jnp.tile(t, (1, LANES // HEAD_DIM)) for t in (cos_t, sa_t, sb_t))


def _block_diag(w):
    n, a, b = w.shape
    eye = jnp.eye(n, dtype=w.dtype)
    return (eye[:, None, :, None] * w[:, :, None, :]).reshape(n * a, n * b)


def _nsa_head_perm():
    r = jnp.arange(NSA_HPG)[:, None, None]
    g = jnp.arange(NSA_KV_GROUPS)[None, :, None]
    dd = jnp.arange(HEAD_DIM)[None, None, :]
    return (HEAD_DIM * (NSA_HPG * g + r) + dd).reshape(-1)


def _overlap(nrow, ns):
    ci = jnp.arange(nrow)[:, None] * CMP_STRIDE
    sj = jnp.arange(ns)[None, :] * SEL_BLOCK
    ov = jnp.clip(jnp.minimum(ci + CMP_LEN, sj + SEL_BLOCK) - jnp.maximum(ci, sj), 0, None)
    return ov.astype(F32) / CMP_LEN


def kernel(x, c, mod_w, mod_b, norm_g, ffn_w1, ffn_w2, mix0_in_w, lru_conv_w, lru_conv_b, lru_wa, lru_ba,
           lru_wx, lru_bx, lru_lambda, mix0_out_w, mix1_in_w, cmp_pos, cmp_w1, cmp_w2, mix1_out_w, final_norm_g):
    bsz, s, d = x.shape
    depth = mod_w.shape[0]
    assert d == D_MODEL and s % 512 == 0 and s >= WINDOW + 128
    tm = 512

    mod = _modulation(c, mod_w, mod_b)
    tabs = _rope_tables(s)
    w1a = ffn_w1[..., :D_FF].astype(BF16)
    w1b = ffn_w1[..., D_FF:].astype(BF16)
    w2 = ffn_w2.astype(BF16)

    for l in range(depth):
        j = l // 2
        x = _ffn(x, mod[l], norm_g[l, 0], w1a[l, 0], w1b[l, 0], w2[l, 0], final_norm_g,
                 row0=0, final=False, tm=tm)
        if l % 2 == 0:
            xg, q, k, v, cent = _inproj0(x, mod[l], norm_g[l, 1], mix0_in_w[j].astype(BF16), tabs)
            y_lru = _lru(xg, lru_conv_w[j], lru_conv_b[j],
                         _block_diag(lru_wa[j]).astype(BF16), lru_ba[j].reshape(-1),
                         _block_diag(lru_wx[j]).astype(BF16), lru_bx[j].reshape(-1), lru_lambda[j], tm=256)
            y_att = _moba(q, k, v, cent.reshape(bsz, -1, MOBA_WIDTH))
            wo = mix0_out_w[j].astype(BF16)
            x = _outproj(x, mod[l], (y_lru, y_att), (wo[:LRU_WIDTH], wo[LRU_WIDTH:]), tm=tm)
        else:
            perm = _nsa_head_perm()
            w_in = mix1_in_w[j]
            w_in = jnp.concatenate([w_in[:, :NSA_QD][:, perm], w_in[:, NSA_QD:]], axis=1)
            w_in = jnp.pad(w_in, ((0, 0), (0, IN1_PAD - w_in.shape[1]))).astype(BF16)
            qn, qr, kvc, kv4, gates = _inproj1(x, mod[l], norm_g[l, 1], w_in, tabs, tm=tm)
            nrow = s // CMP_STRIDE
            cblk = kvc.reshape(bsz, nrow, CMP_STRIDE, 2, NSA_KV_GROUPS, HEAD_DIM)
            cblk = cblk.transpose(0, 3, 4, 1, 2, 5).reshape(bsz, 2, NSA_KV_GROUPS, nrow, CMP_STRIDE * HEAD_DIM)
            half = CMP_STRIDE * HEAD_DIM
            cmp = _compress(cblk, cmp_pos[j].reshape(2, 2, half),
                            cmp_w1[j].reshape(2, 2, half, -1).astype(BF16), cmp_w2[j].astype(BF16))
            y = _nsa(qn, qr, kv4, cmp, gates, _overlap(nrow, s // SEL_BLOCK), tk=512)
            x = _outproj(x, mod[l], (y,), (mix1_out_w[j][perm].astype(BF16),), tm=tm)
        x = _ffn(x, mod[l], norm_g[l, 2], w1a[l, 1], w1b[l, 1], w2[l, 1], final_norm_g,
                 row0=6, final=(l == depth - 1), tm=tm)
    return x
```

```python
import functools
import math

import jax
import jax.numpy as jnp
from jax import lax
from jax.experimental import pallas as pl
from jax.experimental.pallas import tpu as pltpu

F32 = jnp.float32
BF16 = jnp.bfloat16

D_MODEL = 1024
D_FF = 2816
EPS = 1e-6
ROPE_THETA = 500000.0
HEAD_DIM = 64
ROT_DIM = HEAD_DIM // 4
NEG = -1e30
BIG = 1e30
LANES = 128

LRU_WIDTH = D_MODEL // 2
LRU_BLOCKS = 8
CONV_WIDTH = 4
LRU_C = 8.0
MOBA_WIDTH = D_MODEL // 2
MOBA_BLOCK = 256
MOBA_TOPK = 3
MOBA_TQ = 2 * MOBA_BLOCK

NSA_HEADS = D_MODEL // HEAD_DIM
NSA_KV_GROUPS = 2
NSA_HPG = NSA_HEADS // NSA_KV_GROUPS
CMP_LEN = 32
CMP_STRIDE = 16
SEL_BLOCK = 64
SEL_TOPK = 16
WINDOW = 512
NSA_QD = NSA_HEADS * HEAD_DIM
NSA_KVD = NSA_KV_GROUPS * HEAD_DIM
IN1_PAD = NSA_QD + 6 * NSA_KVD + LANES

VMEM_LIMIT = 52 * 1024 * 1024

NT_DIMS = (((1,), (1,)), ((), ()))
Q_PRESCALE = HEAD_DIM ** -0.5 * math.log2(math.e)


def _params(sem):
    return pltpu.CompilerParams(dimension_semantics=sem, vmem_limit_bytes=VMEM_LIMIT)


def _resident(shape):
    nd = len(shape)
    return pl.BlockSpec(shape, lambda *_: (0,) * nd, pipeline_mode=pl.Buffered(1))


def _silu(a):
    return a * jax.nn.sigmoid(a)


def _gelu_tanh(x):
    cdf = 0.5 * (1.0 + jnp.tanh(math.sqrt(2.0 / math.pi) * (x + 0.044715 * (x * x * x))))
    return x * cdf


def _rmsnorm(x, g):
    return x * lax.rsqrt(jnp.mean(x * x, axis=-1, keepdims=True) + EPS) * g


def _norm_mod(x, g, shift, scale):
    return _rmsnorm(x, g) * (1.0 + scale) + shift


def _rope(u, cos, sa, sb):
    outs = []
    for j in range(u.shape[1] // LANES):
        c = u[:, j * LANES:(j + 1) * LANES]
        up = pltpu.roll(c, LANES - ROT_DIM // 2, axis=1)
        dn = pltpu.roll(c, ROT_DIM // 2, axis=1)
        outs.append(c * cos + up * sa + dn * sb)
    return jnp.concatenate(outs, axis=1)


def _pick_topk_rows(g, k):
    n_ids = g.shape[0]
    ids = lax.broadcasted_iota(jnp.int32, g.shape, 0).astype(F32)
    picked = jnp.zeros(g.shape, F32)
    for _ in range(k):
        mx = jnp.max(g, axis=0, keepdims=True)
        first = jnp.min(jnp.where(g == mx, ids, float(n_ids)), axis=0, keepdims=True)
        hit = ids == first
        picked = jnp.where(hit, 1.0, picked)
        g = jnp.where(hit, -jnp.inf, g)
    return picked


def _mod_kernel(c_ref, w_ref, b_ref, o_ref):
    cond = _silu(c_ref[...])
    o_ref[0] = jnp.dot(cond.astype(BF16), w_ref[0].astype(BF16), preferred_element_type=F32) + b_ref[0]


def _modulation(c, mod_w, mod_b):
    depth, d, n = mod_w.shape
    bsz = c.shape[0]
    bp = -(-bsz // 8) * 8
    cp = jnp.pad(c, ((0, bp - bsz), (0, 0)))
    tn = 1152
    out = pl.pallas_call(
        _mod_kernel,
        out_shape=jax.ShapeDtypeStruct((depth, bp, n), F32),
        grid=(depth, n // tn),
        in_specs=[
            pl.BlockSpec((bp, d), lambda l, j: (0, 0)),
            pl.BlockSpec((1, d, tn), lambda l, j: (l, 0, j)),
            pl.BlockSpec((1, 1, tn), lambda l, j: (l, 0, j)),
        ],
        out_specs=pl.BlockSpec((1, bp, tn), lambda l, j: (l, 0, j)),
        compiler_params=_params(("parallel", "parallel")),
        name="adaln_mod",
    )(cp, mod_w, mod_b.reshape(depth, 1, n))
    return out[:, :bsz].reshape(depth, bsz, 9, d)


def _ffn_kernel(x_ref, mod_ref, g_ref, w1a_ref, w1b_ref, w2_ref, gf_ref, o_ref, *, row0, final):
    x = x_ref[0]
    m = mod_ref[0]
    h = _norm_mod(x, g_ref[...], m[row0:row0 + 1], m[row0 + 1:row0 + 2]).astype(BF16)
    a = jnp.dot(h, w1a_ref[...], preferred_element_type=F32)
    b = jnp.dot(h, w1b_ref[...], preferred_element_type=F32)
    act = (_silu(a) * b).astype(BF16)
    y = jnp.dot(act, w2_ref[...], preferred_element_type=F32)
    out = x + (0.5 * m[row0 + 2:row0 + 3]) * y
    if final:
        out = _rmsnorm(out, gf_ref[...])
    o_ref[0] = out


def _ffn(x, mod_l, g, w1a, w1b, w2, gf, *, row0, final, tm):
    bsz, s, d = x.shape
    kern = functools.partial(_ffn_kernel, row0=row0, final=final)
    return pl.pallas_call(
        kern,
        out_shape=jax.ShapeDtypeStruct(x.shape, F32),
        grid=(bsz, s // tm),
        in_specs=[
            pl.BlockSpec((1, tm, d), lambda b, i: (b, i, 0)),
            pl.BlockSpec((1, 9, d), lambda b, i: (b, 0, 0)),
            _resident((1, d)),
            _resident(w1a.shape),
            _resident(w1b.shape),
            _resident(w2.shape),
            _resident((1, d)),
        ],
        out_specs=pl.BlockSpec((1, tm, d), lambda b, i: (b, i, 0)),
        compiler_params=_params(("parallel", "parallel")),
        name="ffn_final" if final else "ffn",
    )(x, mod_l, g.reshape(1, d), w1a, w1b, w2, gf.reshape(1, d))


def _inproj0_kernel(x_ref, mod_ref, g_ref, w_ref, cos_ref, sa_ref, sb_ref,
                    xg_ref, q_ref, k_ref, v_ref, cent_ref):
    m = mod_ref[0]
    h = _norm_mod(x_ref[0], g_ref[...], m[3:4], m[4:5]).astype(BF16)
    u = jnp.dot(h, w_ref[...], preferred_element_type=F32)
    o1 = 2 * LRU_WIDTH
    xg_ref[0] = u[:, :o1]
    cos, sa, sb = cos_ref[...], sa_ref[...], sb_ref[...]
    q = _rope(u[:, o1:o1 + MOBA_WIDTH], cos, sa, sb)
    k = _rope(u[:, o1 + MOBA_WIDTH:o1 + 2 * MOBA_WIDTH], cos, sa, sb)
    q_ref[0] = (q * Q_PRESCALE).astype(BF16)
    k_ref[0] = k.astype(BF16)
    v_ref[0] = u[:, o1 + 2 * MOBA_WIDTH:].astype(BF16)
    cent_ref[0, 0] = jnp.mean(k, axis=0, keepdims=True)


def _inproj0(x, mod_l, g, w, rope_tabs):
    bsz, s, d = x.shape
    tm = MOBA_BLOCK
    nb = s // tm
    n = w.shape[1]
    row = lambda b, i: (b, i, 0)
    tab = pl.BlockSpec((tm, LANES), lambda b, i: (i, 0))
    return pl.pallas_call(
        _inproj0_kernel,
        out_shape=(
            jax.ShapeDtypeStruct((bsz, s, 2 * LRU_WIDTH), F32),
            jax.ShapeDtypeStruct((bsz, s, MOBA_WIDTH), BF16),
            jax.ShapeDtypeStruct((bsz, s, MOBA_WIDTH), BF16),
            jax.ShapeDtypeStruct((bsz, s, MOBA_WIDTH), BF16),
            jax.ShapeDtypeStruct((bsz, nb, 1, MOBA_WIDTH), F32),
        ),
        grid=(bsz, nb),
        in_specs=[
            pl.BlockSpec((1, tm, d), row),
            pl.BlockSpec((1, 9, d), lambda b, i: (b, 0, 0)),
            _resident((1, d)),
            _resident((d, n)),
            tab, tab, tab,
        ],
        out_specs=(
            pl.BlockSpec((1, tm, 2 * LRU_WIDTH), row),
            pl.BlockSpec((1, tm, MOBA_WIDTH), row),
            pl.BlockSpec((1, tm, MOBA_WIDTH), row),
            pl.BlockSpec((1, tm, MOBA_WIDTH), row),
            pl.BlockSpec((1, 1, 1, MOBA_WIDTH), lambda b, i: (b, i, 0, 0)),
        ),
        compiler_params=_params(("parallel", "parallel")),
        name="inproj0",
    )(x, mod_l, g.reshape(1, d), w, *rope_tabs)


def _lru_kernel(xg_ref, cw_ref, cb_ref, wa_ref, ba_ref, wx_ref, bx_ref, lam_ref, y_ref, xbuf, hprev, *, tm):
    c = LRU_WIDTH
    halo = 8

    @pl.when(pl.program_id(1) == 0)
    def _():
        xbuf[0:halo, :] = jnp.zeros((halo, c), F32)
        hprev[...] = jnp.zeros(hprev.shape, F32)

    x = xg_ref[0, :, 0:c]
    gate_in = xg_ref[0, :, c:2 * c]
    xbuf[halo:halo + tm, :] = x
    cw = cw_ref[...]
    xc = cb_ref[...] + cw[CONV_WIDTH - 1:CONV_WIDTH] * x
    for kk in range(CONV_WIDTH - 1):
        back = CONV_WIDTH - 1 - kk
        xc = xc + cw[kk:kk + 1] * xbuf[halo - back:halo - back + tm, :]
    xbuf[0:halo, :] = x[tm - halo:tm, :]

    xcb = xc.astype(BF16)
    r = jax.nn.sigmoid(jnp.dot(xcb, wa_ref[...], preferred_element_type=F32) + ba_ref[...])
    ig = jax.nn.sigmoid(jnp.dot(xcb, wx_ref[...], preferred_element_type=F32) + bx_ref[...])
    nlam = -lam_ref[...]
    softplus = jnp.maximum(nlam, 0.0) + jnp.log1p(jnp.exp(-jnp.abs(nlam)))
    log_a = (-LRU_C * r) * softplus
    a = jnp.exp(log_a)
    mult = jnp.sqrt(-jnp.tanh(log_a) * (a * a + 1.0))
    bt = mult * (ig * xc)

    row = lax.broadcasted_iota(jnp.int32, (tm, c), 0)
    step = 1
    while step < tm:
        a_sh = pltpu.roll(a, step, axis=0)
        b_sh = pltpu.roll(bt, step, axis=0)
        valid = row >= step
        bt = jnp.where(valid, a * b_sh + bt, bt)
        a = jnp.where(valid, a * a_sh, a)
        step *= 2
    h = bt + a * hprev[0:1, :]
    hprev[0:1, :] = h[tm - 1:tm, :]
    y_ref[0] = (h * _gelu_tanh(gate_in)).astype(BF16)


def _lru(xg, cw, cb, wa_bd, ba, wx_bd, bx, lam, *, tm):
    bsz, s, _ = xg.shape
    c = LRU_WIDTH
    vec = _resident((1, c))
    return pl.pallas_call(
        functools.partial(_lru_kernel, tm=tm),
        out_shape=jax.ShapeDtypeStruct((bsz, s, c), BF16),
        grid=(bsz, s // tm),
        in_specs=[
            pl.BlockSpec((1, tm, 2 * c), lambda b, i: (b, i, 0)),
            _resident((CONV_WIDTH, c)), vec,
            _resident((c, c)), vec,
            _resident((c, c)), vec,
            vec,
        ],
        out_specs=pl.BlockSpec((1, tm, c), lambda b, i: (b, i, 0)),
        scratch_shapes=[pltpu.VMEM((tm + 8, c), F32), pltpu.VMEM((8, c), F32)],
        compiler_params=_params(("parallel", "arbitrary")),
        name="rg_lru",
    )(xg, cw, cb.reshape(1, c), wa_bd, ba.reshape(1, c), wx_bd, bx.reshape(1, c), lam.reshape(1, c))


def _moba_kernel(q_ref, k_ref, v_ref, cent_ref, hot_ref, o_ref):
    tq = MOBA_TQ
    rows = 2 * tq
    i = pl.program_id(2)
    q = q_ref[0]
    lane = lax.broadcasted_iota(jnp.int32, (tq, LANES), 1)
    zero = jnp.zeros_like(q)
    q2 = jnp.concatenate([jnp.where(lane < HEAD_DIM, q, zero), jnp.where(lane >= HEAD_DIM, q, zero)], axis=0)

    gate_t = lax.dot_general(cent_ref[0], q2.astype(F32), NT_DIMS, precision=lax.Precision.HIGHEST,
                             preferred_element_type=F32)
    bid = lax.broadcasted_iota(jnp.int32, (LANES, rows), 0)
    qidx = jnp.concatenate([lax.broadcasted_iota(jnp.int32, (LANES, tq), 1)] * 2, axis=1)
    own = (tq // MOBA_BLOCK) * i + qidx // MOBA_BLOCK
    past = bid < own
    sel_t = _pick_topk_rows(jnp.where(past, gate_t, -jnp.inf), MOBA_TOPK)
    drop = jnp.where((past & (sel_t > 0.5)) | (bid == own), 0.0, -BIG).T.astype(BF16)
    lhs = jnp.concatenate([q2, drop], axis=1)

    def tile(kt, carry, bias):
        m_i, l_i, acc = carry
        k0 = pl.multiple_of(kt * tq, tq)
        rhs = jnp.concatenate([k_ref[0, pl.ds(k0, tq), :], hot_ref[pl.ds(k0, tq), :]], axis=1)
        s = lax.dot_general(lhs, rhs, NT_DIMS, preferred_element_type=F32)
        if bias is not None:
            s = s + bias
        m_new = jnp.maximum(m_i, jnp.max(s, axis=1, keepdims=True))
        alpha = jnp.exp2(m_i - m_new)
        p = jnp.exp2(s - m_new)
        l_new = alpha * l_i + jnp.sum(p, axis=1, keepdims=True)
        acc = alpha * acc + jnp.dot(p.astype(BF16), v_ref[0, pl.ds(k0, tq), :], preferred_element_type=F32)
        return m_new, l_new, acc

    qpos = lax.broadcasted_iota(jnp.int32, (tq, tq), 0)
    kpos = lax.broadcasted_iota(jnp.int32, (tq, tq), 1)
    causal = jnp.where(kpos <= qpos, 0.0, NEG)
    init = (jnp.full((rows, 1), NEG, F32), jnp.zeros((rows, 1), F32), jnp.zeros((rows, LANES), F32))
    carry = tile(i, init, jnp.concatenate([causal, causal], axis=0))
    _, l_i, acc = lax.fori_loop(0, i, lambda kt, c: tile(kt, c, None), carry)
    out = acc / l_i
    o_ref[0] = jnp.where(lane < HEAD_DIM, out[:tq], out[tq:]).astype(BF16)


def _moba(q, k, v, cent):
    bsz, s, w = q.shape
    tq = MOBA_TQ
    nb = s // MOBA_BLOCK
    assert nb <= LANES and s % tq == 0
    cent = jnp.pad(cent, ((0, 0), (0, LANES - nb), (0, 0)))
    hot = _block_one_hot(s, MOBA_BLOCK)
    return pl.pallas_call(
        _moba_kernel,
        out_shape=jax.ShapeDtypeStruct((bsz, s, w), BF16),
        grid=(bsz, w // LANES, s // tq),
        in_specs=[
            pl.BlockSpec((1, tq, LANES), lambda b, hp, i: (b, i, hp)),
            pl.BlockSpec((1, s, LANES), lambda b, hp, i: (b, 0, hp)),
            pl.BlockSpec((1, s, LANES), lambda b, hp, i: (b, 0, hp)),
            pl.BlockSpec((1, LANES, LANES), lambda b, hp, i: (b, 0, hp)),
            _resident(hot.shape),
        ],
        out_specs=pl.BlockSpec((1, tq, LANES), lambda b, hp, i: (b, i, hp)),
        compiler_params=_params(("parallel", "parallel", "parallel")),
        name="moba_attn",
    )(q, k, v, cent, hot)


def _outproj0_kernel(x_ref, mod_ref, ya_ref, yb_ref, wa_ref, wb_ref, o_ref):
    y = jnp.dot(ya_ref[0], wa_ref[...], preferred_element_type=F32)
    y = y + jnp.dot(yb_ref[0], wb_ref[...], preferred_element_type=F32)
    o_ref[0] = x_ref[0] + mod_ref[0][5:6] * y


def _outproj1_kernel(x_ref, mod_ref, y_ref, w_ref, o_ref):
    y = jnp.dot(y_ref[0], w_ref[...], preferred_element_type=F32)
    o_ref[0] = x_ref[0] + mod_ref[0][5:6] * y


def _outproj(x, mod_l, ys, ws, *, tm):
    bsz, s, d = x.shape
    row = lambda b, i: (b, i, 0)
    kern = _outproj0_kernel if len(ys) == 2 else _outproj1_kernel
    return pl.pallas_call(
        kern,
        out_shape=jax.ShapeDtypeStruct(x.shape, F32),
        grid=(bsz, s // tm),
        in_specs=[pl.BlockSpec((1, tm, d), row), pl.BlockSpec((1, 9, d), lambda b, i: (b, 0, 0))]
        + [pl.BlockSpec((1, tm, y.shape[2]), row) for y in ys]
        + [_resident(w.shape) for w in ws],
        out_specs=pl.BlockSpec((1, tm, d), row),
        compiler_params=_params(("parallel", "parallel")),
        name="outproj",
    )(x, mod_l, *ys, *ws)


def _inproj1_kernel(x_ref, mod_ref, g_ref, w_ref, cos_ref, sa_ref, sb_ref,
                    qn_ref, qr_ref, kvc_ref, kv4_ref, gate_ref):
    m = mod_ref[0]
    h = _norm_mod(x_ref[0], g_ref[...], m[3:4], m[4:5]).astype(BF16)
    u = jnp.dot(h, w_ref[...], preferred_element_type=F32)
    cos, sa, sb = cos_ref[...], sa_ref[...], sb_ref[...]
    q = u[:, :NSA_QD] * Q_PRESCALE
    qn_ref[0] = q.astype(BF16)
    qr_ref[0] = _rope(q, cos, sa, sb).astype(BF16)
    o = NSA_QD
    kvc_ref[0] = u[:, o:o + 2 * NSA_KVD]
    o += 2 * NSA_KVD
    ks = _rope(u[:, o:o + NSA_KVD], cos, sa, sb)
    vs = u[:, o + NSA_KVD:o + 2 * NSA_KVD]
    kw = _rope(u[:, o + 2 * NSA_KVD:o + 3 * NSA_KVD], cos, sa, sb)
    vw = u[:, o + 3 * NSA_KVD:o + 4 * NSA_KVD]
    kv4_ref[0] = jnp.concatenate([ks, vs, kw, vw], axis=1).astype(BF16)
    gate_ref[0] = jax.nn.sigmoid(u[:, o + 4 * NSA_KVD:])


def _inproj1(x, mod_l, g, w, rope_tabs, *, tm):
    bsz, s, d = x.shape
    n = w.shape[1]
    row = lambda b, i: (b, i, 0)
    tab = pl.BlockSpec((tm, LANES), lambda b, i: (i, 0))
    widths = (NSA_QD, NSA_QD, 2 * NSA_KVD, 4 * NSA_KVD, LANES)
    dtypes = (BF16, BF16, F32, BF16, F32)
    return pl.pallas_call(
        _inproj1_kernel,
        out_shape=tuple(jax.ShapeDtypeStruct((bsz, s, wd), dt) for wd, dt in zip(widths, dtypes)),
        grid=(bsz, s // tm),
        in_specs=[
            pl.BlockSpec((1, tm, d), row),
            pl.BlockSpec((1, 9, d), lambda b, i: (b, 0, 0)),
            _resident((1, d)),
            _resident((d, n)),
            tab, tab, tab,
        ],
        out_specs=tuple(pl.BlockSpec((1, tm, wd), row) for wd in widths),
        compiler_params=_params(("parallel", "parallel")),
        name="inproj1",
    )(x, mod_l, g.reshape(1, d), w, *rope_tabs)


def _compress_kernel(c_ref, pos_ref, w1_ref, w2_ref, o_ref):
    nrow = c_ref.shape[3]
    for kv in range(2):
        outs = []
        for g in range(NSA_KV_GROUPS):
            blk = c_ref[0, kv, g]
            top = (blk + pos_ref[kv, 0:1]).astype(BF16)
            bot = (blk + pos_ref[kv, 1:2]).astype(BF16)
            pa = jnp.dot(top, w1_ref[kv, 0], preferred_element_type=F32)
            pb = jnp.dot(bot, w1_ref[kv, 1], preferred_element_type=F32)
            pre = pa + pltpu.roll(pb, nrow - 1, axis=0)
            hid = _gelu_tanh(pre).astype(BF16)
            outs.append(jnp.dot(hid, w2_ref[kv], preferred_element_type=F32))
        o_ref[0, kv] = jnp.concatenate(outs, axis=1).astype(BF16)


def _compress(cblk, pos, w1, w2):
    bsz, _, _, nrow, flat = cblk.shape
    return pl.pallas_call(
        _compress_kernel,
        out_shape=jax.ShapeDtypeStruct((bsz, 2, nrow, NSA_KVD), BF16),
        grid=(bsz,),
        in_specs=[
            pl.BlockSpec((1, 2, NSA_KV_GROUPS, nrow, flat), lambda b: (b, 0, 0, 0, 0)),
            _resident(pos.shape),
            _resident(w1.shape),
            _resident(w2.shape),
        ],
        out_specs=pl.BlockSpec((1, 2, nrow, NSA_KVD), lambda b: (b, 0, 0, 0)),
        compiler_params=_params(("parallel",)),
        name="nsa_compress",
    )(cblk, pos, w1, w2)


def _nsa_kernel(qn_ref, qr_ref, kv4_ref, cmp_ref, gate_ref, ovl_ref, hot_ref, o_ref, *, seq, tk):
    tq = 128
    nrow = cmp_ref.shape[2]
    ns = seq // SEL_BLOCK
    n_g, n_r = NSA_KV_GROUPS, NSA_HPG
    n_h = n_g * n_r
    rows = n_h * tq
    t0 = pl.program_id(1) * tq
    lane = lax.broadcasted_iota(jnp.int32, (tq, LANES), 1)
    gates = gate_ref[0]

    def stack(q_ref):
        parts = []
        for g in range(n_g):
            in_grp = (lane >= g * HEAD_DIM) & (lane < (g + 1) * HEAD_DIM)
            for r in range(n_r):
                ch = q_ref[0, :, r * LANES:(r + 1) * LANES]
                parts.append(jnp.where(in_grp, ch, jnp.zeros_like(ch)))
        return jnp.concatenate(parts, axis=0)

    qn = stack(qn_ref)
    qr = stack(qr_ref)

    def times_values(p, v):
        vl = lax.broadcasted_iota(jnp.int32, v.shape, 1)
        one = jnp.ones_like(v)
        half = rows // n_g
        return jnp.concatenate(
            [jnp.dot(p[g * half:(g + 1) * half], jnp.where((vl >= g * HEAD_DIM) & (vl < (g + 1) * HEAD_DIM), v, one),
                     preferred_element_type=F32) for g in range(n_g)], axis=0)

    tq_c = t0 + lax.broadcasted_iota(jnp.int32, (tq, nrow), 0)
    cend = lax.broadcasted_iota(jnp.int32, (tq, nrow), 1) * CMP_STRIDE + (CMP_LEN - 1)
    bias_c = jnp.where(cend <= tq_c, 0.0, NEG)
    s_c = lax.dot_general(qn, cmp_ref[0, 0], NT_DIMS, preferred_element_type=F32)
    s_c = s_c.reshape(n_h, tq, nrow) + bias_c[None]
    e_c = jnp.exp2(s_c - jnp.max(s_c, axis=-1, keepdims=True))
    has_block = jnp.where(t0 + lax.broadcasted_iota(jnp.int32, (tq, 1), 0) >= CMP_LEN - 1, 1.0, 0.0)
    p_c = e_c * (has_block[None] / jnp.sum(e_c, axis=-1, keepdims=True))
    o_c = jnp.dot(p_c.reshape(rows, nrow).astype(BF16), cmp_ref[0, 1], preferred_element_type=F32)
    p_grp = jnp.concatenate([jnp.sum(p_c[g * n_r:(g + 1) * n_r], axis=0) for g in range(n_g)], axis=0)

    wk = WINDOW + tq
    kstart = pl.multiple_of(jnp.clip(t0 - WINDOW, 0, seq - wk), tq)
    diff = (t0 + lax.broadcasted_iota(jnp.int32, (tq, wk), 0)) - (
        kstart + lax.broadcasted_iota(jnp.int32, (tq, wk), 1))
    bias_w = jnp.where((diff >= 0) & (diff < WINDOW), 0.0, NEG)
    s_w = lax.dot_general(qr, kv4_ref[0, pl.ds(kstart, wk), 2 * LANES:3 * LANES], NT_DIMS,
                          preferred_element_type=F32)
    s_w = s_w.reshape(n_h, tq, wk) + bias_w[None]
    e_w = jnp.exp2(s_w - jnp.max(s_w, axis=-1, keepdims=True)).reshape(rows, wk).astype(BF16)
    acc_w = times_values(e_w, kv4_ref[0, pl.ds(kstart, wk), 3 * LANES:4 * LANES])

    imp_t = lax.dot_general(ovl_ref[...], p_grp, NT_DIMS, precision=lax.Precision.HIGHEST,
                            preferred_element_type=F32)
    jn = lax.broadcasted_iota(jnp.int32, (LANES, n_g * tq), 0)
    tpos = t0 + lax.broadcasted_iota(jnp.int32, (LANES, tq), 1)
    blk_t = jnp.concatenate([tpos] * n_g, axis=1) // SEL_BLOCK
    valid = jn <= blk_t
    forced = (jn == 0) | (jn == blk_t) | (jn == blk_t - 1)
    others = _pick_topk_rows(jnp.where(valid & jnp.logical_not(forced), imp_t, -jnp.inf), min(SEL_TOPK, ns) - 3)
    keep = valid & (forced | (others > 0.5))
    drop = jnp.where(keep, 0.0, -BIG).T.astype(BF16)
    lhs = jnp.concatenate(
        [qr, jnp.concatenate([drop[g * tq:(g + 1) * tq] for g in range(n_g) for _ in range(n_r)], axis=0)], axis=1)

    def scores(kt):
        k0 = pl.multiple_of(kt * tk, tk)
        rhs = jnp.concatenate([kv4_ref[0, pl.ds(k0, tk), 0:LANES], hot_ref[pl.ds(k0, tk), :]], axis=1)
        return lax.dot_general(lhs, rhs, NT_DIMS, preferred_element_type=F32)

    def sel_tile(kt, carry, s):
        m_i, acc = carry
        m_new = jnp.maximum(m_i, jnp.max(s, axis=-1, keepdims=True))
        p = jnp.exp2(s - m_new).astype(BF16)
        pv = times_values(p, kv4_ref[0, pl.ds(pl.multiple_of(kt * tk, tk), tk), LANES:2 * LANES])
        return m_new, jnp.exp2(m_i - m_new) * acc + pv

    last = t0 // tk
    carry = (jnp.full((rows, 1), NEG, F32), jnp.zeros((rows, LANES), F32))
    carry = lax.fori_loop(0, last, lambda kt, c: sel_tile(kt, c, scores(kt)), carry)
    kpos = last * tk + lax.broadcasted_iota(jnp.int32, (tq, tk), 1)
    tq_s = t0 + lax.broadcasted_iota(jnp.int32, (tq, tk), 0)
    causal = jnp.where(kpos <= tq_s, 0.0, NEG)
    _, acc_s = sel_tile(last, carry, (scores(last).reshape(n_h, tq, tk) + causal[None]).reshape(rows, tk))

    inv_s = 1.0 / pltpu.roll(acc_s, HEAD_DIM, axis=1)
    inv_w = 1.0 / pltpu.roll(acc_w, HEAD_DIM, axis=1)
    for r in range(n_r):
        halves = []
        for g in range(n_g):
            h = g * n_r + r
            sl = slice(h * tq, (h + 1) * tq)
            halves.append(gates[:, 3 * h:3 * h + 1] * o_c[sl]
                          + (gates[:, 3 * h + 1:3 * h + 2] * inv_s[sl]) * acc_s[sl]
                          + (gates[:, 3 * h + 2:3 * h + 3] * inv_w[sl]) * acc_w[sl])
        o_ref[0, :, r * LANES:(r + 1) * LANES] = jnp.where(lane < HEAD_DIM, halves[0], halves[1]).astype(BF16)


def _nsa(qn, qr, kv4, cmp, gates, *, tk):
    bsz, s, _ = qn.shape
    tq = 128
    ns = s // SEL_BLOCK
    assert ns <= LANES
    assert min(SEL_TOPK, ns) >= 3
    ovl = jnp.pad(_overlap(cmp.shape[2], ns), ((0, 0), (0, LANES - ns))).T
    hot = _block_one_hot(s, SEL_BLOCK)
    row = lambda b, i: (b, i, 0)
    return pl.pallas_call(
        functools.partial(_nsa_kernel, seq=s, tk=tk),
        out_shape=jax.ShapeDtypeStruct((bsz, s, NSA_QD), BF16),
        grid=(bsz, s // tq),
        in_specs=[
            pl.BlockSpec((1, tq, NSA_QD), row),
            pl.BlockSpec((1, tq, NSA_QD), row),
            pl.BlockSpec((1, s, 4 * NSA_KVD), lambda b, i: (b, 0, 0)),
            pl.BlockSpec((1, 2) + cmp.shape[2:], lambda b, i: (b, 0, 0, 0)),
            pl.BlockSpec((1, tq, LANES), row),
            _resident(ovl.shape),
            _resident(hot.shape),
        ],
        out_specs=pl.BlockSpec((1, tq, NSA_QD), row),
        compiler_params=_params(("parallel", "parallel")),
        name="nsa_attn",
    )(qn, qr, kv4, cmp, gates, ovl, hot)


def _rope_tables(s):
    half = ROT_DIM // 2
    pos = jnp.arange(s, dtype=F32)
    freqs = ROPE_THETA ** (-jnp.arange(half, dtype=F32) * 2.0 / ROT_DIM)
    ang = pos[:, None] * freqs[None, :]
    cos, sin = jnp.cos(ang), jnp.sin(ang)
    ones = jnp.ones((s, HEAD_DIM - ROT_DIM), F32)
    zeros = jnp.zeros((s, HEAD_DIM - ROT_DIM), F32)
    zh = jnp.zeros((s, half), F32)
    cos_t = jnp.concatenate([cos, cos, ones], axis=1)
    sa_t = jnp.concatenate([-sin, zh, zeros], axis=1)
    sb_t = jnp.concatenate([zh, sin, zeros], axis=1)
    return tuple(jnp.tile(t, (1, LANES // HEAD_DIM)) for t in (cos_t, sa_t, sb_t))


def _block_one_hot(s, block):
    return (jnp.arange(s)[:, None] // block == jnp.arange(LANES)[None, :]).astype(BF16)


def _block_diag(w):
    n, a, b = w.shape
    eye = jnp.eye(n, dtype=w.dtype)
    return (eye[:, None, :, None] * w[:, :, None, :]).reshape(n * a, n * b)


def _nsa_head_perm():
    r = jnp.arange(NSA_HPG)[:, None, None]
    g = jnp.arange(NSA_KV_GROUPS)[None, :, None]
    dd = jnp.arange(HEAD_DIM)[None, None, :]
    return (HEAD_DIM * (NSA_HPG * g + r) + dd).reshape(-1)


def _overlap(nrow, ns):
    ci = jnp.arange(nrow)[:, None] * CMP_STRIDE
    sj = jnp.arange(ns)[None, :] * SEL_BLOCK
    ov = jnp.clip(jnp.minimum(ci + CMP_LEN, sj + SEL_BLOCK) - jnp.maximum(ci, sj), 0, None)
    return ov.astype(F32) / CMP_LEN


def kernel(x, c, mod_w, mod_b, norm_g, ffn_w1, ffn_w2, mix0_in_w, lru_conv_w, lru_conv_b, lru_wa, lru_ba,
           lru_wx, lru_bx, lru_lambda, mix0_out_w, mix1_in_w, cmp_pos, cmp_w1, cmp_w2, mix1_out_w, final_norm_g):
    bsz, s, d = x.shape
    depth = mod_w.shape[0]
    assert d == D_MODEL and s % 512 == 0 and s >= WINDOW + 128
    tm = 512

    mod = _modulation(c, mod_w, mod_b)
    tabs = _rope_tables(s)
    w1a = ffn_w1[..., :D_FF].astype(BF16)
    w1b = ffn_w1[..., D_FF:].astype(BF16)
    w2 = ffn_w2.astype(BF16)

    for l in range(depth):
        j = l // 2
        x = _ffn(x, mod[l], norm_g[l, 0], w1a[l, 0], w1b[l, 0], w2[l, 0], final_norm_g,
                 row0=0, final=False, tm=tm)
        if l % 2 == 0:
            xg, q, k, v, cent = _inproj0(x, mod[l], norm_g[l, 1], mix0_in_w[j].astype(BF16), tabs)
            y_lru = _lru(xg, lru_conv_w[j], lru_conv_b[j],
                         _block_diag(lru_wa[j]).astype(BF16), lru_ba[j].reshape(-1),
                         _block_diag(lru_wx[j]).astype(BF16), lru_bx[j].reshape(-1), lru_lambda[j], tm=256)
            y_att = _moba(q, k, v, cent.reshape(bsz, -1, MOBA_WIDTH))
            wo = mix0_out_w[j].astype(BF16)
            x = _outproj(x, mod[l], (y_lru, y_att), (wo[:LRU_WIDTH], wo[LRU_WIDTH:]), tm=tm)
        else:
            perm = _nsa_head_perm()
            w_in = mix1_in_w[j]
            w_in = jnp.concatenate([w_in[:, :NSA_QD][:, perm], w_in[:, NSA_QD:]], axis=1)
            w_in = jnp.pad(w_in, ((0, 0), (0, IN1_PAD - w_in.shape[1]))).astype(BF16)
            qn, qr, kvc, kv4, gates = _inproj1(x, mod[l], norm_g[l, 1], w_in, tabs, tm=tm)
            nrow = s // CMP_STRIDE
            cblk = kvc.reshape(bsz, nrow, CMP_STRIDE, 2, NSA_KV_GROUPS, HEAD_DIM)
            cblk = cblk.transpose(0, 3, 4, 1, 2, 5).reshape(bsz, 2, NSA_KV_GROUPS, nrow, CMP_STRIDE * HEAD_DIM)
            half = CMP_STRIDE * HEAD_DIM
            cmp = _compress(cblk, cmp_pos[j].reshape(2, 2, half),
                            cmp_w1[j].reshape(2, 2, half, -1).astype(BF16), cmp_w2[j].astype(BF16))
            y = _nsa(qn, qr, kv4, cmp, gates, tk=512)
            x = _outproj(x, mod[l], (y,), (mix1_out_w[j][perm].astype(BF16),), tm=tm)
        x = _ffn(x, mod[l], norm_g[l, 2], w1a[l, 1], w1b[l, 1], w2[l, 1], final_norm_g,
                 row0=6, final=(l == depth - 1), tm=tm)
    return x
```

```python
import functools
import math

import jax
import jax.numpy as jnp
from jax import lax
from jax.experimental import pallas as pl
from jax.experimental.pallas import tpu as pltpu

F32 = jnp.float32
BF16 = jnp.bfloat16

D_MODEL = 1024
D_FF = 2816
EPS = 1e-6
ROPE_THETA = 500000.0
HEAD_DIM = 64
ROT_DIM = HEAD_DIM // 4
NEG = -1e30
BIG = 1e30
LANES = 128

LRU_WIDTH = D_MODEL // 2
LRU_BLOCKS = 8
CONV_WIDTH = 4
LRU_C = 8.0
MOBA_WIDTH = D_MODEL // 2
MOBA_BLOCK = 256
MOBA_TOPK = 3
MOBA_TQ = 2 * MOBA_BLOCK

NSA_HEADS = D_MODEL // HEAD_DIM
NSA_KV_GROUPS = 2
NSA_HPG = NSA_HEADS // NSA_KV_GROUPS
CMP_LEN = 32
CMP_STRIDE = 16
SEL_BLOCK = 64
SEL_TOPK = 16
WINDOW = 512
NSA_QD = NSA_HEADS * HEAD_DIM
NSA_KVD = NSA_KV_GROUPS * HEAD_DIM
IN1_PAD = NSA_QD + 6 * NSA_KVD + LANES

VMEM_LIMIT = 52 * 1024 * 1024

NT_DIMS = (((1,), (1,)), ((), ()))
Q_PRESCALE = HEAD_DIM ** -0.5 * math.log2(math.e)


def _params(sem):
    return pltpu.CompilerParams(dimension_semantics=sem, vmem_limit_bytes=VMEM_LIMIT)


def _resident(shape):
    nd = len(shape)
    return pl.BlockSpec(shape, lambda *_: (0,) * nd, pipeline_mode=pl.Buffered(1))


def _silu(a):
    return a * jax.nn.sigmoid(a)


def _gelu_tanh(x):
    cdf = 0.5 * (1.0 + jnp.tanh(math.sqrt(2.0 / math.pi) * (x + 0.044715 * (x * x * x))))
    return x * cdf


def _rmsnorm(x, g):
    return x * lax.rsqrt(jnp.mean(x * x, axis=-1, keepdims=True) + EPS) * g


def _norm_mod(x, g, shift, scale):
    return _rmsnorm(x, g) * (1.0 + scale) + shift


def _rope(u, cos, sa, sb):
    outs = []
    for j in range(u.shape[1] // LANES):
        c = u[:, j * LANES:(j + 1) * LANES]
        up = pltpu.roll(c, LANES - ROT_DIM // 2, axis=1)
        dn = pltpu.roll(c, ROT_DIM // 2, axis=1)
        outs.append(c * cos + up * sa + dn * sb)
    return jnp.concatenate(outs, axis=1)


SOFTMAX_CHUNK = 64


def _stream_softmax(s_ref, p_ref, m_old, bias, tq, keep_f32=False):
    rows, n = s_ref.shape
    n_chunks, n_l = rows // SOFTMAX_CHUNK, n // LANES

    def chunk(c):
        sl = slice(c * SOFTMAX_CHUNK, (c + 1) * SOFTMAX_CHUNK)
        sc = s_ref[sl, :]
        if bias is not None:
            q0 = (c * SOFTMAX_CHUNK) % tq
            sc = sc + bias[q0:q0 + SOFTMAX_CHUNK]
        return sl, [sc[:, j * LANES:(j + 1) * LANES] for j in range(n_l)]

    partial_max = jnp.concatenate([functools.reduce(jnp.maximum, chunk(c)[1]) for c in range(n_chunks)], axis=0)
    m_new = jnp.maximum(m_old, jnp.broadcast_to(jnp.max(partial_max, axis=1, keepdims=True), (rows, LANES)))
    sums = []
    for c in range(n_chunks):
        sl, cols = chunk(c)
        m_c = m_new[sl]
        ps = [jnp.exp2(col - m_c) for col in cols]
        for j, pj in enumerate(ps):
            if keep_f32:
                s_ref[sl, j * LANES:(j + 1) * LANES] = pj
            else:
                p_ref[sl, j * LANES:(j + 1) * LANES] = pj.astype(BF16)
        sums.append(functools.reduce(jnp.add, ps))
    return m_new, jnp.concatenate(sums, axis=0)


def _pick_topk_rows(g, k):
    n_ids = g.shape[0]
    ids = lax.broadcasted_iota(jnp.int32, g.shape, 0).astype(F32)
    picked = jnp.zeros(g.shape, F32)
    for _ in range(k):
        mx = jnp.max(g, axis=0, keepdims=True)
        first = jnp.min(jnp.where(g == mx, ids, float(n_ids)), axis=0, keepdims=True)
        hit = ids == first
        picked = jnp.where(hit, 1.0, picked)
        g = jnp.where(hit, -jnp.inf, g)
    return picked


def _mod_kernel(c_ref, w_ref, b_ref, o_ref):
    cond = _silu(c_ref[...])
    o_ref[0] = jnp.dot(cond.astype(BF16), w_ref[0].astype(BF16), preferred_element_type=F32) + b_ref[0]


def _modulation(c, mod_w, mod_b):
    depth, d, n = mod_w.shape
    bsz = c.shape[0]
    bp = -(-bsz // 8) * 8
    cp = jnp.pad(c, ((0, bp - bsz), (0, 0)))
    tn = 1152
    out = pl.pallas_call(
        _mod_kernel,
        out_shape=jax.ShapeDtypeStruct((depth, bp, n), F32),
        grid=(depth, n // tn),
        in_specs=[
            pl.BlockSpec((bp, d), lambda l, j: (0, 0)),
            pl.BlockSpec((1, d, tn), lambda l, j: (l, 0, j)),
            pl.BlockSpec((1, 1, tn), lambda l, j: (l, 0, j)),
        ],
        out_specs=pl.BlockSpec((1, bp, tn), lambda l, j: (l, 0, j)),
        compiler_params=_params(("parallel", "parallel")),
        name="adaln_mod",
    )(cp, mod_w, mod_b.reshape(depth, 1, n))
    return out[:, :bsz].reshape(depth, bsz, 9, d)


def _ffn_kernel(x_ref, mod_ref, g_ref, w1a_ref, w1b_ref, w2_ref, gf_ref, o_ref, *, row0, final):
    x = x_ref[0]
    m = mod_ref[0]
    h = _norm_mod(x, g_ref[...], m[row0:row0 + 1], m[row0 + 1:row0 + 2]).astype(BF16)
    a = jnp.dot(h, w1a_ref[...], preferred_element_type=F32)
    b = jnp.dot(h, w1b_ref[...], preferred_element_type=F32)
    act = (_silu(a) * b).astype(BF16)
    y = jnp.dot(act, w2_ref[...], preferred_element_type=F32)
    out = x + (0.5 * m[row0 + 2:row0 + 3]) * y
    if final:
        out = _rmsnorm(out, gf_ref[...])
    o_ref[0] = out


def _ffn(x, mod_l, g, w1a, w1b, w2, gf, *, row0, final, tm):
    bsz, s, d = x.shape
    kern = functools.partial(_ffn_kernel, row0=row0, final=final)
    return pl.pallas_call(
        kern,
        out_shape=jax.ShapeDtypeStruct(x.shape, F32),
        grid=(bsz, s // tm),
        in_specs=[
            pl.BlockSpec((1, tm, d), lambda b, i: (b, i, 0)),
            pl.BlockSpec((1, 9, d), lambda b, i: (b, 0, 0)),
            _resident((1, d)),
            _resident(w1a.shape),
            _resident(w1b.shape),
            _resident(w2.shape),
            _resident((1, d)),
        ],
        out_specs=pl.BlockSpec((1, tm, d), lambda b, i: (b, i, 0)),
        compiler_params=_params(("parallel", "parallel")),
        name="ffn_final" if final else "ffn",
    )(x, mod_l, g.reshape(1, d), w1a, w1b, w2, gf.reshape(1, d))


def _inproj0_kernel(x_ref, mod_ref, g_ref, w_ref, cos_ref, sa_ref, sb_ref,
                    xg_ref, q_ref, k_ref, v_ref, cent_ref):
    m = mod_ref[0]
    h = _norm_mod(x_ref[0], g_ref[...], m[3:4], m[4:5]).astype(BF16)
    u = jnp.dot(h, w_ref[...], preferred_element_type=F32)
    o1 = 2 * LRU_WIDTH
    xg_ref[0] = u[:, :o1]
    cos, sa, sb = cos_ref[...], sa_ref[...], sb_ref[...]
    q = _rope(u[:, o1:o1 + MOBA_WIDTH], cos, sa, sb)
    k = _rope(u[:, o1 + MOBA_WIDTH:o1 + 2 * MOBA_WIDTH], cos, sa, sb)
    q_ref[0] = (q * Q_PRESCALE).astype(BF16)
    k_ref[0] = k.astype(BF16)
    v_ref[0] = u[:, o1 + 2 * MOBA_WIDTH:].astype(BF16)
    cent_ref[0, 0] = jnp.mean(k, axis=0, keepdims=True)


def _inproj0(x, mod_l, g, w, rope_tabs):
    bsz, s, d = x.shape
    tm = MOBA_BLOCK
    nb = s // tm
    n = w.shape[1]
    row = lambda b, i: (b, i, 0)
    tab = pl.BlockSpec((tm, LANES), lambda b, i: (i, 0))
    return pl.pallas_call(
        _inproj0_kernel,
        out_shape=(
            jax.ShapeDtypeStruct((bsz, s, 2 * LRU_WIDTH), F32),
            jax.ShapeDtypeStruct((bsz, s, MOBA_WIDTH), BF16),
            jax.ShapeDtypeStruct((bsz, s, MOBA_WIDTH), BF16),
            jax.ShapeDtypeStruct((bsz, s, MOBA_WIDTH), BF16),
            jax.ShapeDtypeStruct((bsz, nb, 1, MOBA_WIDTH), F32),
        ),
        grid=(bsz, nb),
        in_specs=[
            pl.BlockSpec((1, tm, d), row),
            pl.BlockSpec((1, 9, d), lambda b, i: (b, 0, 0)),
            _resident((1, d)),
            _resident((d, n)),
            tab, tab, tab,
        ],
        out_specs=(
            pl.BlockSpec((1, tm, 2 * LRU_WIDTH), row),
            pl.BlockSpec((1, tm, MOBA_WIDTH), row),
            pl.BlockSpec((1, tm, MOBA_WIDTH), row),
            pl.BlockSpec((1, tm, MOBA_WIDTH), row),
            pl.BlockSpec((1, 1, 1, MOBA_WIDTH), lambda b, i: (b, i, 0, 0)),
        ),
        compiler_params=_params(("parallel", "parallel")),
        name="inproj0",
    )(x, mod_l, g.reshape(1, d), w, *rope_tabs)


def _lru_kernel(xg_ref, cw_ref, cb_ref, wa_ref, ba_ref, wx_ref, bx_ref, lam_ref, y_ref, xbuf, hprev, *, tm):
    c = LRU_WIDTH
    halo = 8

    @pl.when(pl.program_id(1) == 0)
    def _():
        xbuf[0:halo, :] = jnp.zeros((halo, c), F32)
        hprev[...] = jnp.zeros(hprev.shape, F32)

    x = xg_ref[0, :, 0:c]
    gate_in = xg_ref[0, :, c:2 * c]
    xbuf[halo:halo + tm, :] = x
    cw = cw_ref[...]
    xc = cb_ref[...] + cw[CONV_WIDTH - 1:CONV_WIDTH] * x
    for kk in range(CONV_WIDTH - 1):
        back = CONV_WIDTH - 1 - kk
        xc = xc + cw[kk:kk + 1] * xbuf[halo - back:halo - back + tm, :]
    xbuf[0:halo, :] = x[tm - halo:tm, :]

    xcb = xc.astype(BF16)
    r = jax.nn.sigmoid(jnp.dot(xcb, wa_ref[...], preferred_element_type=F32) + ba_ref[...])
    ig = jax.nn.sigmoid(jnp.dot(xcb, wx_ref[...], preferred_element_type=F32) + bx_ref[...])
    nlam = -lam_ref[...]
    softplus = jnp.maximum(nlam, 0.0) + jnp.log1p(jnp.exp(-jnp.abs(nlam)))
    log_a = (-LRU_C * r) * softplus
    a = jnp.exp(log_a)
    mult = jnp.sqrt(-jnp.tanh(log_a) * (a * a + 1.0))
    bt = mult * (ig * xc)

    row = lax.broadcasted_iota(jnp.int32, (tm, c), 0)
    step = 1
    while step < tm:
        a_sh = pltpu.roll(a, step, axis=0)
        b_sh = pltpu.roll(bt, step, axis=0)
        valid = row >= step
        bt = jnp.where(valid, a * b_sh + bt, bt)
        a = jnp.where(valid, a * a_sh, a)
        step *= 2
    h = bt + a * hprev[0:1, :]
    hprev[0:1, :] = h[tm - 1:tm, :]
    y_ref[0] = (h * _gelu_tanh(gate_in)).astype(BF16)


def _lru(xg, cw, cb, wa_bd, ba, wx_bd, bx, lam, *, tm):
    bsz, s, _ = xg.shape
    c = LRU_WIDTH
    vec = _resident((1, c))
    return pl.pallas_call(
        functools.partial(_lru_kernel, tm=tm),
        out_shape=jax.ShapeDtypeStruct((bsz, s, c), BF16),
        grid=(bsz, s // tm),
        in_specs=[
            pl.BlockSpec((1, tm, 2 * c), lambda b, i: (b, i, 0)),
            _resident((CONV_WIDTH, c)), vec,
            _resident((c, c)), vec,
            _resident((c, c)), vec,
            vec,
        ],
        out_specs=pl.BlockSpec((1, tm, c), lambda b, i: (b, i, 0)),
        scratch_shapes=[pltpu.VMEM((tm + 8, c), F32), pltpu.VMEM((8, c), F32)],
        compiler_params=_params(("parallel", "arbitrary")),
        name="rg_lru",
    )(xg, cw, cb.reshape(1, c), wa_bd, ba.reshape(1, c), wx_bd, bx.reshape(1, c), lam.reshape(1, c))


def _moba_kernel(q_ref, k_ref, v_ref, cent_ref, hot_ref, o_ref, lhs_ref, s_ref, p_ref, m_ref, l_ref, acc_ref):
    tq = MOBA_TQ
    rows = 2 * tq
    i = pl.program_id(2)
    q = q_ref[0]
    lane = lax.broadcasted_iota(jnp.int32, (tq, LANES), 1)
    zero = jnp.zeros_like(q)
    q2 = jnp.concatenate([jnp.where(lane < HEAD_DIM, q, zero), jnp.where(lane >= HEAD_DIM, q, zero)], axis=0)

    gate_t = lax.dot_general(cent_ref[0], q2.astype(F32), NT_DIMS, precision=lax.Precision.HIGHEST,
                             preferred_element_type=F32)
    bid = lax.broadcasted_iota(jnp.int32, (LANES, rows), 0)
    qidx = jnp.concatenate([lax.broadcasted_iota(jnp.int32, (LANES, tq), 1)] * 2, axis=1)
    own = (tq // MOBA_BLOCK) * i + qidx // MOBA_BLOCK
    past = bid < own
    sel_t = _pick_topk_rows(jnp.where(past, gate_t, -jnp.inf), MOBA_TOPK)
    drop = jnp.where((past & (sel_t > 0.5)) | (bid == own), 0.0, -BIG).T.astype(BF16)
    lhs_ref[...] = jnp.concatenate([q2, drop], axis=1)
    m_ref[...] = jnp.full(m_ref.shape, NEG, F32)
    l_ref[...] = jnp.zeros(l_ref.shape, F32)
    acc_ref[...] = jnp.zeros(acc_ref.shape, F32)

    def tile(kt, bias):
        k0 = pl.multiple_of(kt * tq, tq)
        rhs = jnp.concatenate([k_ref[0, pl.ds(k0, tq), :], hot_ref[pl.ds(k0, tq), :]], axis=1)
        s_ref[...] = lax.dot_general(lhs_ref[...], rhs, NT_DIMS, preferred_element_type=F32)
        m_old = m_ref[...]
        m_new, p_sum = _stream_softmax(s_ref, p_ref, m_old, bias, tq)
        alpha = jnp.exp2(m_old - m_new)
        m_ref[...] = m_new
        l_ref[...] = alpha * l_ref[...] + p_sum
        acc_ref[...] = alpha * acc_ref[...] + jnp.dot(p_ref[...], v_ref[0, pl.ds(k0, tq), :],
                                                      preferred_element_type=F32)

    qpos = lax.broadcasted_iota(jnp.int32, (tq, tq), 0)
    kpos = lax.broadcasted_iota(jnp.int32, (tq, tq), 1)
    tile(i, jnp.where(kpos <= qpos, 0.0, NEG))

    @pl.loop(0, i)
    def _(kt):
        tile(kt, None)

    out = acc_ref[...] / jnp.sum(l_ref[...], axis=1, keepdims=True)
    o_ref[0] = jnp.where(lane < HEAD_DIM, out[:tq], out[tq:]).astype(BF16)


def _moba(q, k, v, cent):
    bsz, s, w = q.shape
    tq = MOBA_TQ
    nb = s // MOBA_BLOCK
    assert nb <= LANES and s % tq == 0
    cent = jnp.pad(cent, ((0, 0), (0, LANES - nb), (0, 0)))
    hot = _block_one_hot(s, MOBA_BLOCK)
    return pl.pallas_call(
        _moba_kernel,
        out_shape=jax.ShapeDtypeStruct((bsz, s, w), BF16),
        grid=(bsz, w // LANES, s // tq),
        in_specs=[
            pl.BlockSpec((1, tq, LANES), lambda b, hp, i: (b, i, hp)),
            pl.BlockSpec((1, s, LANES), lambda b, hp, i: (b, 0, hp)),
            pl.BlockSpec((1, s, LANES), lambda b, hp, i: (b, 0, hp)),
            pl.BlockSpec((1, LANES, LANES), lambda b, hp, i: (b, 0, hp)),
            _resident(hot.shape),
        ],
        out_specs=pl.BlockSpec((1, tq, LANES), lambda b, hp, i: (b, i, hp)),
        scratch_shapes=[pltpu.VMEM((2 * tq, 2 * LANES), BF16), pltpu.VMEM((2 * tq, tq), F32),
                        pltpu.VMEM((2 * tq, tq), BF16)] + [pltpu.VMEM((2 * tq, LANES), F32)] * 3,
        compiler_params=_params(("parallel", "parallel", "parallel")),
        name="moba_attn",
    )(q, k, v, cent, hot)


def _outproj0_kernel(x_ref, mod_ref, ya_ref, yb_ref, wa_ref, wb_ref, o_ref):
    y = jnp.dot(ya_ref[0], wa_ref[...], preferred_element_type=F32)
    y = y + jnp.dot(yb_ref[0], wb_ref[...], preferred_element_type=F32)
    o_ref[0] = x_ref[0] + mod_ref[0][5:6] * y


def _outproj1_kernel(x_ref, mod_ref, y_ref, w_ref, o_ref):
    y = jnp.dot(y_ref[0], w_ref[...], preferred_element_type=F32)
    o_ref[0] = x_ref[0] + mod_ref[0][5:6] * y


def _outproj(x, mod_l, ys, ws, *, tm):
    bsz, s, d = x.shape
    row = lambda b, i: (b, i, 0)
    kern = _outproj0_kernel if len(ys) == 2 else _outproj1_kernel
    return pl.pallas_call(
        kern,
        out_shape=jax.ShapeDtypeStruct(x.shape, F32),
        grid=(bsz, s // tm),
        in_specs=[pl.BlockSpec((1, tm, d), row), pl.BlockSpec((1, 9, d), lambda b, i: (b, 0, 0))]
        + [pl.BlockSpec((1, tm, y.shape[2]), row) for y in ys]
        + [_resident(w.shape) for w in ws],
        out_specs=pl.BlockSpec((1, tm, d), row),
        compiler_params=_params(("parallel", "parallel")),
        name="outproj",
    )(x, mod_l, *ys, *ws)


def _inproj1_kernel(x_ref, mod_ref, g_ref, w_ref, cos_ref, sa_ref, sb_ref,
                    qn_ref, qr_ref, kvc_ref, kv4_ref, gate_ref):
    m = mod_ref[0]
    h = _norm_mod(x_ref[0], g_ref[...], m[3:4], m[4:5]).astype(BF16)
    u = jnp.dot(h, w_ref[...], preferred_element_type=F32)
    cos, sa, sb = cos_ref[...], sa_ref[...], sb_ref[...]
    q = u[:, :NSA_QD] * Q_PRESCALE
    qn_ref[0] = q.astype(BF16)
    qr_ref[0] = _rope(q, cos, sa, sb).astype(BF16)
    o = NSA_QD
    kvc_ref[0] = u[:, o:o + 2 * NSA_KVD]
    o += 2 * NSA_KVD
    ks = _rope(u[:, o:o + NSA_KVD], cos, sa, sb)
    vs = u[:, o + NSA_KVD:o + 2 * NSA_KVD]
    kw = _rope(u[:, o + 2 * NSA_KVD:o + 3 * NSA_KVD], cos, sa, sb)
    vw = u[:, o + 3 * NSA_KVD:o + 4 * NSA_KVD]
    kv4_ref[0] = jnp.concatenate([ks, vs, kw, vw], axis=1).astype(BF16)
    gate_ref[0] = jax.nn.sigmoid(u[:, o + 4 * NSA_KVD:])


def _inproj1(x, mod_l, g, w, rope_tabs, *, tm):
    bsz, s, d = x.shape
    n = w.shape[1]
    row = lambda b, i: (b, i, 0)
    tab = pl.BlockSpec((tm, LANES), lambda b, i: (i, 0))
    widths = (NSA_QD, NSA_QD, 2 * NSA_KVD, 4 * NSA_KVD, LANES)
    dtypes = (BF16, BF16, F32, BF16, F32)
    return pl.pallas_call(
        _inproj1_kernel,
        out_shape=tuple(jax.ShapeDtypeStruct((bsz, s, wd), dt) for wd, dt in zip(widths, dtypes)),
        grid=(bsz, s // tm),
        in_specs=[
            pl.BlockSpec((1, tm, d), row),
            pl.BlockSpec((1, 9, d), lambda b, i: (b, 0, 0)),
            _resident((1, d)),
            _resident((d, n)),
            tab, tab, tab,
        ],
        out_specs=tuple(pl.BlockSpec((1, tm, wd), row) for wd in widths),
        compiler_params=_params(("parallel", "parallel")),
        name="inproj1",
    )(x, mod_l, g.reshape(1, d), w, *rope_tabs)


def _compress_kernel(c_ref, pos_ref, w1_ref, w2_ref, o_ref):
    nrow = c_ref.shape[3]
    for kv in range(2):
        outs = []
        for g in range(NSA_KV_GROUPS):
            blk = c_ref[0, kv, g]
            top = (blk + pos_ref[kv, 0:1]).astype(BF16)
            bot = (blk + pos_ref[kv, 1:2]).astype(BF16)
            pa = jnp.dot(top, w1_ref[kv, 0], preferred_element_type=F32)
            pb = jnp.dot(bot, w1_ref[kv, 1], preferred_element_type=F32)
            pre = pa + pltpu.roll(pb, nrow - 1, axis=0)
            hid = _gelu_tanh(pre).astype(BF16)
            outs.append(jnp.dot(hid, w2_ref[kv], preferred_element_type=F32))
        o_ref[0, kv] = jnp.concatenate(outs, axis=1).astype(BF16)


def _compress(cblk, pos, w1, w2):
    bsz, _, _, nrow, flat = cblk.shape
    return pl.pallas_call(
        _compress_kernel,
        out_shape=jax.ShapeDtypeStruct((bsz, 2, nrow, NSA_KVD), BF16),
        grid=(bsz,),
        in_specs=[
            pl.BlockSpec((1, 2, NSA_KV_GROUPS, nrow, flat), lambda b: (b, 0, 0, 0, 0)),
            _resident(pos.shape),
            _resident(w1.shape),
            _resident(w2.shape),
        ],
        out_specs=pl.BlockSpec((1, 2, nrow, NSA_KVD), lambda b: (b, 0, 0, 0)),
        compiler_params=_params(("parallel",)),
        name="nsa_compress",
    )(cblk, pos, w1, w2)


def _nsa_kernel(qn_ref, qr_ref, kv4_ref, cmp_ref, gate_ref, ovl_ref, hot_ref, gexp_ref, o_ref,
                lhs_ref, s_ref, p_ref, pg_ref, m_ref, acc_ref, *, seq, tk):
    tq = 128
    nrow = cmp_ref.shape[2]
    ns = seq // SEL_BLOCK
    n_g, n_r = NSA_KV_GROUPS, NSA_HPG
    n_h = n_g * n_r
    rows = n_h * tq
    t0 = pl.program_id(1) * tq
    lane = lax.broadcasted_iota(jnp.int32, (tq, LANES), 1)
    gates = gate_ref[0]

    def stack(q_ref):
        parts = []
        for g in range(n_g):
            in_grp = (lane >= g * HEAD_DIM) & (lane < (g + 1) * HEAD_DIM)
            for r in range(n_r):
                ch = q_ref[0, :, r * LANES:(r + 1) * LANES]
                parts.append(jnp.where(in_grp, ch, jnp.zeros_like(ch)))
        return jnp.concatenate(parts, axis=0)

    half = rows // n_g
    neg_stats = jnp.full((rows, LANES), NEG, F32)

    def times_values(p_view, v, g):
        vl = lax.broadcasted_iota(jnp.int32, v.shape, 1)
        vg = jnp.where((vl >= g * HEAD_DIM) & (vl < (g + 1) * HEAD_DIM), v, jnp.ones_like(v))
        return jnp.dot(p_view[g * half:(g + 1) * half, :], vg, preferred_element_type=F32)

    sc_ref, pc_ref = s_ref.at[:, 0:nrow], p_ref.at[:, 0:nrow]
    tq_c = t0 + lax.broadcasted_iota(jnp.int32, (tq, nrow), 0)
    cend = lax.broadcasted_iota(jnp.int32, (tq, nrow), 1) * CMP_STRIDE + (CMP_LEN - 1)
    sc_ref[...] = lax.dot_general(stack(qn_ref), cmp_ref[0, 0], NT_DIMS, preferred_element_type=F32)
    _, e_sum = _stream_softmax(sc_ref, pc_ref, neg_stats, jnp.where(cend <= tq_c, 0.0, NEG), tq, keep_f32=True)
    has_block = jnp.where(t0 + lax.broadcasted_iota(jnp.int32, (tq, 1), 0) >= CMP_LEN - 1, 1.0, 0.0)
    inv_c = jnp.concatenate([has_block] * n_h, axis=0) / jnp.sum(e_sum, axis=1, keepdims=True)
    inv_c = jnp.broadcast_to(inv_c, (rows, LANES))
    for c in range(rows // SOFTMAX_CHUNK):
        sl = slice(c * SOFTMAX_CHUNK, (c + 1) * SOFTMAX_CHUNK)
        g, q0 = (c * SOFTMAX_CHUNK) // half, (c * SOFTMAX_CHUNK) % tq
        gsl = slice(g * tq + q0, g * tq + q0 + SOFTMAX_CHUNK)
        for j in range(nrow // LANES):
            cols = slice(j * LANES, (j + 1) * LANES)
            pn = sc_ref[sl, cols] * inv_c[sl]
            pc_ref[sl, cols] = pn.astype(BF16)
            first_head = (c * SOFTMAX_CHUNK) % half < tq
            pg_ref[gsl, cols] = pn if first_head else pg_ref[gsl, cols] + pn
    o_c = jnp.dot(pc_ref[...], cmp_ref[0, 1], preferred_element_type=F32)

    imp_t = lax.dot_general(ovl_ref[...], pg_ref[...], NT_DIMS, precision=lax.Precision.HIGHEST,
                            preferred_element_type=F32)
    jn = lax.broadcasted_iota(jnp.int32, (LANES, n_g * tq), 0)
    tpos = t0 + lax.broadcasted_iota(jnp.int32, (LANES, tq), 1)
    blk_t = jnp.concatenate([tpos] * n_g, axis=1) // SEL_BLOCK
    valid = jn <= blk_t
    forced = (jn == 0) | (jn == blk_t) | (jn == blk_t - 1)
    others = _pick_topk_rows(jnp.where(valid & jnp.logical_not(forced), imp_t, -jnp.inf), min(SEL_TOPK, ns) - 3)
    keep = valid & (forced | (others > 0.5))
    drop = jnp.where(keep, 0.0, -BIG).T.astype(BF16)
    lhs_ref[...] = jnp.concatenate(
        [stack(qr_ref), jnp.concatenate([drop[g * tq:(g + 1) * tq] for g in range(n_g) for _ in range(n_r)], axis=0)],
        axis=1)

    wk = WINDOW + tq
    sw_ref, pw_ref = s_ref.at[:, 0:wk], p_ref.at[:, 0:wk]
    kstart = pl.multiple_of(jnp.clip(t0 - WINDOW, 0, seq - wk), tq)
    diff = (t0 + lax.broadcasted_iota(jnp.int32, (tq, wk), 0)) - (
        kstart + lax.broadcasted_iota(jnp.int32, (tq, wk), 1))
    sw_ref[...] = lax.dot_general(lhs_ref[:, 0:LANES], kv4_ref[0, pl.ds(kstart, wk), 2 * LANES:3 * LANES], NT_DIMS,
                                  preferred_element_type=F32)
    _stream_softmax(sw_ref, pw_ref, neg_stats, jnp.where((diff >= 0) & (diff < WINDOW), 0.0, NEG), tq)
    acc_w = jnp.concatenate([times_values(pw_ref, kv4_ref[0, pl.ds(kstart, wk), 3 * LANES:4 * LANES], g)
                             for g in range(n_g)], axis=0)

    ss_ref, ps_ref = s_ref.at[:, 0:tk], p_ref.at[:, 0:tk]
    m_ref[...] = neg_stats
    acc_ref[...] = jnp.zeros(acc_ref.shape, F32)

    def sel_tile(kt, bias):
        k0 = pl.multiple_of(kt * tk, tk)
        rhs = jnp.concatenate([kv4_ref[0, pl.ds(k0, tk), 0:LANES], hot_ref[pl.ds(k0, tk), :]], axis=1)
        ss_ref[...] = lax.dot_general(lhs_ref[...], rhs, NT_DIMS, preferred_element_type=F32)
        m_old = m_ref[...]
        m_new, _ = _stream_softmax(ss_ref, ps_ref, m_old, bias, tq)
        m_ref[...] = m_new
        alpha = jnp.exp2(m_old - m_new)
        for g in range(n_g):
            gs = slice(g * half, (g + 1) * half)
            acc_ref[gs, :] = alpha[gs] * acc_ref[gs, :] + times_values(ps_ref, kv4_ref[0, pl.ds(k0, tk), LANES:2 * LANES], g)

    last = t0 // tk

    @pl.loop(0, last)
    def _(kt):
        sel_tile(kt, None)

    @pl.when(last >= 0)
    def _():
        kpos = last * tk + lax.broadcasted_iota(jnp.int32, (tq, tk), 1)
        tq_s = t0 + lax.broadcasted_iota(jnp.int32, (tq, tk), 0)
        sel_tile(last, jnp.where(kpos <= tq_s, 0.0, NEG))

    acc_s = acc_ref[...]

    branches = (o_c, acc_s / pltpu.roll(acc_s, HEAD_DIM, axis=1), acc_w / pltpu.roll(acc_w, HEAD_DIM, axis=1))
    gates_b = gates.astype(BF16)
    spread = [jnp.dot(gates_b, gexp_ref[c], preferred_element_type=F32) for c in range(3)]
    for r in range(n_r):
        y = None
        for c in range(3):
            heads = [branches[c][(g * n_r + r) * tq:(g * n_r + r + 1) * tq] for g in range(n_g)]
            term = spread[c][:, r * LANES:(r + 1) * LANES] * jnp.where(lane < HEAD_DIM, heads[0], heads[1])
            y = term if y is None else y + term
        o_ref[0, :, r * LANES:(r + 1) * LANES] = y.astype(BF16)


def _nsa(qn, qr, kv4, cmp, gates, *, tk):
    bsz, s, _ = qn.shape
    tq = 128
    ns = s // SEL_BLOCK
    assert ns <= LANES
    assert min(SEL_TOPK, ns) >= 3
    ovl = jnp.pad(_overlap(cmp.shape[2], ns), ((0, 0), (0, LANES - ns))).T
    hot = _block_one_hot(s, SEL_BLOCK)
    head_of_p = _nsa_head_perm() // HEAD_DIM
    gexp = (jnp.arange(LANES)[None, :, None] == 3 * head_of_p[None, None, :] + jnp.arange(3)[:, None, None]).astype(BF16)
    rows = NSA_HEADS * tq
    width = max(cmp.shape[2], WINDOW + tq, tk)
    row = lambda b, i: (b, i, 0)
    return pl.pallas_call(
        functools.partial(_nsa_kernel, seq=s, tk=tk),
        out_shape=jax.ShapeDtypeStruct((bsz, s, NSA_QD), BF16),
        grid=(bsz, s // tq),
        in_specs=[
            pl.BlockSpec((1, tq, NSA_QD), row),
            pl.BlockSpec((1, tq, NSA_QD), row),
            pl.BlockSpec((1, s, 4 * NSA_KVD), lambda b, i: (b, 0, 0)),
            pl.BlockSpec((1, 2) + cmp.shape[2:], lambda b, i: (b, 0, 0, 0)),
            pl.BlockSpec((1, tq, LANES), row),
            _resident(ovl.shape),
            _resident(hot.shape),
            _resident(gexp.shape),
        ],
        out_specs=pl.BlockSpec((1, tq, NSA_QD), row),
        scratch_shapes=[pltpu.VMEM((rows, 2 * LANES), BF16), pltpu.VMEM((rows, width), F32),
                        pltpu.VMEM((rows, width), BF16), pltpu.VMEM((NSA_KV_GROUPS * tq, cmp.shape[2]), F32),
                        pltpu.VMEM((rows, LANES), F32), pltpu.VMEM((rows, LANES), F32)],
        compiler_params=_params(("parallel", "parallel")),
        name="nsa_attn",
    )(qn, qr, kv4, cmp, gates, ovl, hot, gexp)


def _rope_tables(s):
    half = ROT_DIM // 2
    pos = jnp.arange(s, dtype=F32)
    freqs = ROPE_THETA ** (-jnp.arange(half, dtype=F32) * 2.0 / ROT_DIM)
    ang = pos[:, None] * freqs[None, :]
    cos, sin = jnp.cos(ang), jnp.sin(ang)
    ones = jnp.ones((s, HEAD_DIM - ROT_DIM), F32)
    zeros = jnp.zeros((s, HEAD_DIM - ROT_DIM), F32)
    zh = jnp.zeros((s, half), F32)
    cos_t = jnp.concatenate([cos, cos, ones], axis=1)
    sa_t = jnp.concatenate([-sin, zh, zeros], axis=1)
    sb_t = jnp.concatenate([zh, sin, zeros], axis=1)
    return tuple(jnp.tile(t, (1, LANES // HEAD_DIM)) for t in (cos_t, sa_t, sb_t))


def _block_one_hot(s, block):
    return (jnp.arange(s)[:, None] // block == jnp.arange(LANES)[None, :]).astype(BF16)


def _block_diag(w):
    n, a, b = w.shape
    eye = jnp.eye(n, dtype=w.dtype)
    return (eye[:, None, :, None] * w[:, :, None, :]).reshape(n * a, n * b)


def _nsa_head_perm():
    r = jnp.arange(NSA_HPG)[:, None, None]
    g = jnp.arange(NSA_KV_GROUPS)[None, :, None]
    dd = jnp.arange(HEAD_DIM)[None, None, :]
    return (HEAD_DIM * (NSA_HPG * g + r) + dd).reshape(-1)


def _overlap(nrow, ns):
    ci = jnp.arange(nrow)[:, None] * CMP_STRIDE
    sj = jnp.arange(ns)[None, :] * SEL_BLOCK
    ov = jnp.clip(jnp.minimum(ci + CMP_LEN, sj + SEL_BLOCK) - jnp.maximum(ci, sj), 0, None)
    return ov.astype(F32) / CMP_LEN


def kernel(x, c, mod_w, mod_b, norm_g, ffn_w1, ffn_w2, mix0_in_w, lru_conv_w, lru_conv_b, lru_wa, lru_ba,
           lru_wx, lru_bx, lru_lambda, mix0_out_w, mix1_in_w, cmp_pos, cmp_w1, cmp_w2, mix1_out_w, final_norm_g):
    bsz, s, d = x.shape
    depth = mod_w.shape[0]
    assert d == D_MODEL and s % 512 == 0 and s >= WINDOW + 128
    tm = 512

    mod = _modulation(c, mod_w, mod_b)
    tabs = _rope_tables(s)
    w1a = ffn_w1[..., :D_FF].astype(BF16)
    w1b = ffn_w1[..., D_FF:].astype(BF16)
    w2 = ffn_w2.astype(BF16)

    for l in range(depth):
        j = l // 2
        x = _ffn(x, mod[l], norm_g[l, 0], w1a[l, 0], w1b[l, 0], w2[l, 0], final_norm_g,
                 row0=0, final=False, tm=tm)
        if l % 2 == 0:
            xg, q, k, v, cent = _inproj0(x, mod[l], norm_g[l, 1], mix0_in_w[j].astype(BF16), tabs)
            y_lru = _lru(xg, lru_conv_w[j], lru_conv_b[j],
                         _block_diag(lru_wa[j]).astype(BF16), lru_ba[j].reshape(-1),
                         _block_diag(lru_wx[j]).astype(BF16), lru_bx[j].reshape(-1), lru_lambda[j], tm=256)
            y_att = _moba(q, k, v, cent.reshape(bsz, -1, MOBA_WIDTH))
            wo = mix0_out_w[j].astype(BF16)
            x = _outproj(x, mod[l], (y_lru, y_att), (wo[:LRU_WIDTH], wo[LRU_WIDTH:]), tm=tm)
        else:
            perm = _nsa_head_perm()
            w_in = mix1_in_w[j]
            w_in = jnp.concatenate([w_in[:, :NSA_QD][:, perm], w_in[:, NSA_QD:]], axis=1)
            w_in = jnp.pad(w_in, ((0, 0), (0, IN1_PAD - w_in.shape[1]))).astype(BF16)
            qn, qr, kvc, kv4, gates = _inproj1(x, mod[l], norm_g[l, 1], w_in, tabs, tm=tm)
            nrow = s // CMP_STRIDE
            cblk = kvc.reshape(bsz, nrow, CMP_STRIDE, 2, NSA_KV_GROUPS, HEAD_DIM)
            cblk = cblk.transpose(0, 3, 4, 1, 2, 5).reshape(bsz, 2, NSA_KV_GROUPS, nrow, CMP_STRIDE * HEAD_DIM)
            half = CMP_STRIDE * HEAD_DIM
            cmp = _compress(cblk, cmp_pos[j].reshape(2, 2, half),
                            cmp_w1[j].reshape(2, 2, half, -1).astype(BF16), cmp_w2[j].astype(BF16))
            y = _nsa(qn, qr, kv4, cmp, gates, tk=512)
            x = _outproj(x, mod[l], (y,), (mix1_out_w[j][perm].astype(BF16),), tm=tm)
        x = _ffn(x, mod[l], norm_g[l, 2], w1a[l, 1], w1b[l, 1], w2[l, 1], final_norm_g,
                 row0=6, final=(l == depth - 1), tm=tm)
    return x
```

```python
import functools
import math

import jax
import jax.numpy as jnp
from jax import lax
from jax.experimental import pallas as pl
from jax.experimental.pallas import tpu as pltpu

F32 = jnp.float32
BF16 = jnp.bfloat16

D_MODEL = 1024
D_FF = 2816
EPS = 1e-6
ROPE_THETA = 500000.0
HEAD_DIM = 64
ROT_DIM = HEAD_DIM // 4
NEG = -1e30
BIG = 1e30
LANES = 128

LRU_WIDTH = D_MODEL // 2
LRU_BLOCKS = 8
CONV_WIDTH = 4
LRU_C = 8.0
MOBA_WIDTH = D_MODEL // 2
MOBA_BLOCK = 256
MOBA_TOPK = 3
MOBA_TQ = 2 * MOBA_BLOCK
MOBA_CHUNKS = 2

NSA_HEADS = D_MODEL // HEAD_DIM
NSA_KV_GROUPS = 2
NSA_HPG = NSA_HEADS // NSA_KV_GROUPS
CMP_LEN = 32
CMP_STRIDE = 16
SEL_BLOCK = 64
SEL_TOPK = 16
WINDOW = 512
NSA_QD = NSA_HEADS * HEAD_DIM
NSA_KVD = NSA_KV_GROUPS * HEAD_DIM
IN1_PAD = NSA_QD + 6 * NSA_KVD + LANES

VMEM_LIMIT = 52 * 1024 * 1024

NT_DIMS = (((1,), (1,)), ((), ()))
Q_PRESCALE = HEAD_DIM ** -0.5 * math.log2(math.e)


def _params(sem):
    return pltpu.CompilerParams(dimension_semantics=sem, vmem_limit_bytes=VMEM_LIMIT)


def _resident(shape):
    nd = len(shape)
    return pl.BlockSpec(shape, lambda *_: (0,) * nd, pipeline_mode=pl.Buffered(1))


def _silu(a):
    return a * jax.nn.sigmoid(a)


def _gelu_tanh(x):
    cdf = 0.5 * (1.0 + jnp.tanh(math.sqrt(2.0 / math.pi) * (x + 0.044715 * (x * x * x))))
    return x * cdf


def _rmsnorm(x, g):
    return x * lax.rsqrt(jnp.mean(x * x, axis=-1, keepdims=True) + EPS) * g


def _norm_mod(x, g, shift, scale):
    return _rmsnorm(x, g) * (1.0 + scale) + shift


def _rope(u, cos, sa, sb):
    outs = []
    for j in range(u.shape[1] // LANES):
        c = u[:, j * LANES:(j + 1) * LANES]
        up = pltpu.roll(c, LANES - ROT_DIM // 2, axis=1)
        dn = pltpu.roll(c, ROT_DIM // 2, axis=1)
        outs.append(c * cos + up * sa + dn * sb)
    return jnp.concatenate(outs, axis=1)


def _bf16_pieces(x):
    hi = x.astype(BF16)
    r1 = x - hi.astype(F32)
    mid = r1.astype(BF16)
    lo = (r1 - mid.astype(F32)).astype(BF16)
    return jnp.concatenate([hi, mid, lo], axis=0)


SOFTMAX_CHUNK = 64


def _stream_softmax(s_ref, p_ref, m_old, bias, tq, keep_f32=False):
    rows, n = s_ref.shape
    n_chunks, n_l = rows // SOFTMAX_CHUNK, n // LANES

    def chunk(c):
        sl = slice(c * SOFTMAX_CHUNK, (c + 1) * SOFTMAX_CHUNK)
        sc = s_ref[sl, :]
        if bias is not None:
            q0 = (c * SOFTMAX_CHUNK) % tq
            sc = sc + bias[q0:q0 + SOFTMAX_CHUNK]
        return sl, [sc[:, j * LANES:(j + 1) * LANES] for j in range(n_l)]

    partial_max = jnp.concatenate([functools.reduce(jnp.maximum, chunk(c)[1]) for c in range(n_chunks)], axis=0)
    m_new = jnp.maximum(m_old, jnp.broadcast_to(jnp.max(partial_max, axis=1, keepdims=True), (rows, LANES)))
    sums = []
    for c in range(n_chunks):
        sl, cols = chunk(c)
        m_c = m_new[sl]
        ps = [jnp.exp2(col - m_c) for col in cols]
        for j, pj in enumerate(ps):
            if keep_f32:
                s_ref[sl, j * LANES:(j + 1) * LANES] = pj
            else:
                p_ref[sl, j * LANES:(j + 1) * LANES] = pj.astype(BF16)
        sums.append(functools.reduce(jnp.add, ps))
    return m_new, jnp.concatenate(sums, axis=0)


def _pick_topk_rows(g, k):
    n_ids = g.shape[0]
    ids = lax.broadcasted_iota(jnp.int32, g.shape, 0).astype(F32)
    picked = jnp.zeros(g.shape, F32)
    for _ in range(k):
        mx = jnp.max(g, axis=0, keepdims=True)
        first = jnp.min(jnp.where(g == mx, ids, float(n_ids)), axis=0, keepdims=True)
        hit = ids == first
        picked = jnp.where(hit, 1.0, picked)
        g = jnp.where(hit, -jnp.inf, g)
    return picked


def _mod_kernel(c_ref, w_ref, b_ref, o_ref):
    cond = _silu(c_ref[...])
    o_ref[0] = jnp.dot(cond.astype(BF16), w_ref[0].astype(BF16), preferred_element_type=F32) + b_ref[0]


def _modulation(c, mod_w, mod_b):
    depth, d, n = mod_w.shape
    bsz = c.shape[0]
    bp = -(-bsz // 8) * 8
    cp = jnp.pad(c, ((0, bp - bsz), (0, 0)))
    tn = 1152
    out = pl.pallas_call(
        _mod_kernel,
        out_shape=jax.ShapeDtypeStruct((depth, bp, n), F32),
        grid=(depth, n // tn),
        in_specs=[
            pl.BlockSpec((bp, d), lambda l, j: (0, 0)),
            pl.BlockSpec((1, d, tn), lambda l, j: (l, 0, j)),
            pl.BlockSpec((1, 1, tn), lambda l, j: (l, 0, j)),
        ],
        out_specs=pl.BlockSpec((1, bp, tn), lambda l, j: (l, 0, j)),
        compiler_params=_params(("parallel", "parallel")),
        name="adaln_mod",
    )(cp, mod_w, mod_b.reshape(depth, 1, n))
    return out[:, :bsz].reshape(depth, bsz, 9, d)


def _ffn_kernel(x_ref, mod_ref, g_ref, w1a_ref, w1b_ref, w2_ref, gf_ref, *rest, row0, final, n_mix):
    o_ref = rest[-1]
    x = x_ref[0]
    m = mod_ref[0]
    if n_mix:
        mixed = sum(jnp.dot(y_ref[0], w_ref[...], preferred_element_type=F32)
                    for y_ref, w_ref in zip(rest[:n_mix], rest[n_mix:2 * n_mix]))
        x = x + m[5:6] * mixed
    h = _norm_mod(x, g_ref[...], m[row0:row0 + 1], m[row0 + 1:row0 + 2]).astype(BF16)
    a = jnp.dot(h, w1a_ref[...], preferred_element_type=F32)
    b = jnp.dot(h, w1b_ref[...], preferred_element_type=F32)
    act = (_silu(a) * b).astype(BF16)
    y = jnp.dot(act, w2_ref[...], preferred_element_type=F32)
    out = x + (0.5 * m[row0 + 2:row0 + 3]) * y
    if final:
        out = _rmsnorm(out, gf_ref[...])
    o_ref[0] = out


def _ffn(x, mod_l, g, w1a, w1b, w2, gf, *, row0, final, tm, mix=()):
    bsz, s, d = x.shape
    kern = functools.partial(_ffn_kernel, row0=row0, final=final, n_mix=len(mix))
    row = lambda b, i: (b, i, 0)
    return pl.pallas_call(
        kern,
        out_shape=jax.ShapeDtypeStruct(x.shape, F32),
        grid=(bsz, s // tm),
        in_specs=[
            pl.BlockSpec((1, tm, d), row),
            pl.BlockSpec((1, 9, d), lambda b, i: (b, 0, 0)),
            _resident((1, d)),
            _resident(w1a.shape),
            _resident(w1b.shape),
            _resident(w2.shape),
            _resident((1, d)),
        ] + [pl.BlockSpec((1, tm, y.shape[2]), row) for y, _ in mix] + [_resident(w.shape) for _, w in mix],
        out_specs=pl.BlockSpec((1, tm, d), row),
        compiler_params=_params(("parallel", "parallel")),
        name="ffn_final" if final else ("mix_ffn" if mix else "ffn"),
    )(x, mod_l, g.reshape(1, d), w1a, w1b, w2, gf.reshape(1, d), *[y for y, _ in mix], *[w for _, w in mix])


def _inproj0_kernel(x_ref, mod_ref, g_ref, w_ref, cos_ref, sa_ref, sb_ref,
                    xg_ref, q_ref, k_ref, v_ref, cent_ref):
    m = mod_ref[0]
    h = _norm_mod(x_ref[0], g_ref[...], m[3:4], m[4:5]).astype(BF16)
    u = jnp.dot(h, w_ref[...], preferred_element_type=F32)
    o1 = 2 * LRU_WIDTH
    xg_ref[0] = u[:, :o1]
    cos, sa, sb = cos_ref[...], sa_ref[...], sb_ref[...]
    q = _rope(u[:, o1:o1 + MOBA_WIDTH], cos, sa, sb)
    k = _rope(u[:, o1 + MOBA_WIDTH:o1 + 2 * MOBA_WIDTH], cos, sa, sb)
    q_ref[0] = (q * Q_PRESCALE).astype(BF16)
    k_ref[0] = k.astype(BF16)
    v_ref[0] = u[:, o1 + 2 * MOBA_WIDTH:].astype(BF16)
    cent_ref[0, 0] = jnp.mean(k, axis=0, keepdims=True)


def _inproj0(x, mod_l, g, w, rope_tabs):
    bsz, s, d = x.shape
    tm = MOBA_BLOCK
    nb = s // tm
    n = w.shape[1]
    row = lambda b, i: (b, i, 0)
    tab = pl.BlockSpec((tm, LANES), lambda b, i: (i, 0))
    return pl.pallas_call(
        _inproj0_kernel,
        out_shape=(
            jax.ShapeDtypeStruct((bsz, s, 2 * LRU_WIDTH), F32),
            jax.ShapeDtypeStruct((bsz, s, MOBA_WIDTH), BF16),
            jax.ShapeDtypeStruct((bsz, s, MOBA_WIDTH), BF16),
            jax.ShapeDtypeStruct((bsz, s, MOBA_WIDTH), BF16),
            jax.ShapeDtypeStruct((bsz, nb, 1, MOBA_WIDTH), F32),
        ),
        grid=(bsz, nb),
        in_specs=[
            pl.BlockSpec((1, tm, d), row),
            pl.BlockSpec((1, 9, d), lambda b, i: (b, 0, 0)),
            _resident((1, d)),
            _resident((d, n)),
            tab, tab, tab,
        ],
        out_specs=(
            pl.BlockSpec((1, tm, 2 * LRU_WIDTH), row),
            pl.BlockSpec((1, tm, MOBA_WIDTH), row),
            pl.BlockSpec((1, tm, MOBA_WIDTH), row),
            pl.BlockSpec((1, tm, MOBA_WIDTH), row),
            pl.BlockSpec((1, 1, 1, MOBA_WIDTH), lambda b, i: (b, i, 0, 0)),
        ),
        compiler_params=_params(("parallel", "parallel")),
        name="inproj0",
    )(x, mod_l, g.reshape(1, d), w, *rope_tabs)


def _lru_kernel(xg_ref, cw_ref, cb_ref, wa_ref, ba_ref, wx_ref, bx_ref, lam_ref, y_ref, xbuf, hprev, *, tm):
    c = LRU_WIDTH
    halo = 8

    @pl.when(pl.program_id(1) == 0)
    def _():
        xbuf[0:halo, :] = jnp.zeros((halo, c), F32)
        hprev[...] = jnp.zeros(hprev.shape, F32)

    x = xg_ref[0, :, 0:c]
    gate_in = xg_ref[0, :, c:2 * c]
    xbuf[halo:halo + tm, :] = x
    cw = cw_ref[...]
    xc = cb_ref[...] + cw[CONV_WIDTH - 1:CONV_WIDTH] * x
    for kk in range(CONV_WIDTH - 1):
        back = CONV_WIDTH - 1 - kk
        xc = xc + cw[kk:kk + 1] * xbuf[halo - back:halo - back + tm, :]
    xbuf[0:halo, :] = x[tm - halo:tm, :]

    xcb = xc.astype(BF16)
    r = jax.nn.sigmoid(jnp.dot(xcb, wa_ref[...], preferred_element_type=F32) + ba_ref[...])
    ig = jax.nn.sigmoid(jnp.dot(xcb, wx_ref[...], preferred_element_type=F32) + bx_ref[...])
    nlam = -lam_ref[...]
    softplus = jnp.maximum(nlam, 0.0) + jnp.log1p(jnp.exp(-jnp.abs(nlam)))
    log_a = (-LRU_C * r) * softplus
    a = jnp.exp(log_a)
    mult = jnp.sqrt(-jnp.tanh(log_a) * (a * a + 1.0))
    bt = mult * (ig * xc)

    row = lax.broadcasted_iota(jnp.int32, (tm, c), 0)
    step = 1
    while step < tm:
        a_sh = pltpu.roll(a, step, axis=0)
        b_sh = pltpu.roll(bt, step, axis=0)
        valid = row >= step
        bt = jnp.where(valid, a * b_sh + bt, bt)
        a = jnp.where(valid, a * a_sh, a)
        step *= 2
    h = bt + a * hprev[0:1, :]
    hprev[0:1, :] = h[tm - 1:tm, :]
    y_ref[0] = (h * _gelu_tanh(gate_in)).astype(BF16)


def _lru(xg, cw, cb, wa_bd, ba, wx_bd, bx, lam, *, tm):
    bsz, s, _ = xg.shape
    c = LRU_WIDTH
    vec = _resident((1, c))
    return pl.pallas_call(
        functools.partial(_lru_kernel, tm=tm),
        out_shape=jax.ShapeDtypeStruct((bsz, s, c), BF16),
        grid=(bsz, s // tm),
        in_specs=[
            pl.BlockSpec((1, tm, 2 * c), lambda b, i: (b, i, 0)),
            _resident((CONV_WIDTH, c)), vec,
            _resident((c, c)), vec,
            _resident((c, c)), vec,
            vec,
        ],
        out_specs=pl.BlockSpec((1, tm, c), lambda b, i: (b, i, 0)),
        scratch_shapes=[pltpu.VMEM((tm + 8, c), F32), pltpu.VMEM((8, c), F32)],
        compiler_params=_params(("parallel", "arbitrary")),
        name="rg_lru",
    )(xg, cw, cb.reshape(1, c), wa_bd, ba.reshape(1, c), wx_bd, bx.reshape(1, c), lam.reshape(1, c))


def _moba_kernel(q_ref, k_ref, v_ref, cent_ref, hot_ref, o_ref, lhs_ref, s_ref, p_ref, m_ref, l_ref, acc_ref):
    tq = MOBA_TQ
    rows = 2 * tq
    i = pl.program_id(2)
    lane = lax.broadcasted_iota(jnp.int32, (tq, LANES), 1)
    bid = lax.broadcasted_iota(jnp.int32, (LANES, rows), 0)
    qidx = jnp.concatenate([lax.broadcasted_iota(jnp.int32, (LANES, tq), 1)] * 2, axis=1)
    own = (tq // MOBA_BLOCK) * i + qidx // MOBA_BLOCK
    past = bid < own
    chunks = [slice(c * LANES, (c + 1) * LANES) for c in range(MOBA_CHUNKS)]

    for c, cl in enumerate(chunks):
        q = q_ref[0, :, cl]
        zero = jnp.zeros_like(q)
        q2 = jnp.concatenate([jnp.where(lane < HEAD_DIM, q, zero), jnp.where(lane >= HEAD_DIM, q, zero)], axis=0)
        gate3 = lax.dot_general(_bf16_pieces(cent_ref[0, :, cl]), q2, NT_DIMS, preferred_element_type=F32)
        gate_t = gate3[:LANES] + gate3[LANES:2 * LANES] + gate3[2 * LANES:]
        sel_t = _pick_topk_rows(jnp.where(past, gate_t, -jnp.inf), MOBA_TOPK)
        drop = jnp.where((past & (sel_t > 0.5)) | (bid == own), 0.0, -BIG).T.astype(BF16)
        lhs_ref[c] = jnp.concatenate([q2, drop], axis=1)
        m_ref[c] = jnp.full((rows, LANES), NEG, F32)
        l_ref[c] = jnp.zeros((rows, LANES), F32)
        acc_ref[c] = jnp.zeros((rows, LANES), F32)

    def tile(kt, bias):
        k0 = pl.multiple_of(kt * tq, tq)
        for c, cl in enumerate(chunks):
            rhs = jnp.concatenate([k_ref[0, pl.ds(k0, tq), cl], hot_ref[pl.ds(k0, tq), :]], axis=1)
            s_ref[c] = lax.dot_general(lhs_ref[c], rhs, NT_DIMS, preferred_element_type=F32)
        for c, cl in enumerate(chunks):
            m_old = m_ref[c]
            m_new, p_sum = _stream_softmax(s_ref.at[c], p_ref.at[c], m_old, bias, tq)
            alpha = jnp.exp2(m_old - m_new)
            m_ref[c] = m_new
            l_ref[c] = alpha * l_ref[c] + p_sum
            acc_ref[c] = alpha * acc_ref[c] + jnp.dot(p_ref[c], v_ref[0, pl.ds(k0, tq), cl],
                                                      preferred_element_type=F32)

    @pl.when(i >= 0)
    def _():
        qpos = lax.broadcasted_iota(jnp.int32, (tq, tq), 0)
        kpos = lax.broadcasted_iota(jnp.int32, (tq, tq), 1)
        tile(i, jnp.where(kpos <= qpos, 0.0, NEG))

    @pl.loop(0, i)
    def _(kt):
        tile(kt, None)

    for c, cl in enumerate(chunks):
        out = acc_ref[c] / jnp.sum(l_ref[c], axis=1, keepdims=True)
        o_ref[0, :, cl] = jnp.where(lane < HEAD_DIM, out[:tq], out[tq:]).astype(BF16)


def _moba(q, k, v, cent):
    bsz, s, w = q.shape
    tq = MOBA_TQ
    nb = s // MOBA_BLOCK
    assert nb <= LANES and s % tq == 0
    cent = jnp.pad(cent, ((0, 0), (0, LANES - nb), (0, 0)))
    hot = _block_one_hot(s, MOBA_BLOCK)
    nc, cw = MOBA_CHUNKS, MOBA_CHUNKS * LANES
    return pl.pallas_call(
        _moba_kernel,
        out_shape=jax.ShapeDtypeStruct((bsz, s, w), BF16),
        grid=(bsz, w // cw, s // tq),
        in_specs=[
            pl.BlockSpec((1, tq, cw), lambda b, hp, i: (b, i, hp)),
            pl.BlockSpec((1, s, cw), lambda b, hp, i: (b, 0, hp)),
            pl.BlockSpec((1, s, cw), lambda b, hp, i: (b, 0, hp)),
            pl.BlockSpec((1, LANES, cw), lambda b, hp, i: (b, 0, hp)),
            _resident(hot.shape),
        ],
        out_specs=pl.BlockSpec((1, tq, cw), lambda b, hp, i: (b, i, hp)),
        scratch_shapes=[pltpu.VMEM((nc, 2 * tq, 2 * LANES), BF16), pltpu.VMEM((nc, 2 * tq, tq), F32),
                        pltpu.VMEM((nc, 2 * tq, tq), BF16)] + [pltpu.VMEM((nc, 2 * tq, LANES), F32)] * 3,
        compiler_params=_params(("parallel", "parallel", "parallel")),
        name="moba_attn",
    )(q, k, v, cent, hot)


def _inproj1_kernel(x_ref, mod_ref, g_ref, w_ref, cos_ref, sa_ref, sb_ref,
                    qn_ref, qr_ref, kvc_ref, kv4_ref, gate_ref):
    m = mod_ref[0]
    h = _norm_mod(x_ref[0], g_ref[...], m[3:4], m[4:5]).astype(BF16)
    u = jnp.dot(h, w_ref[...], preferred_element_type=F32)
    cos, sa, sb = cos_ref[...], sa_ref[...], sb_ref[...]
    q = u[:, :NSA_QD] * Q_PRESCALE
    qn_ref[0] = q.astype(BF16)
    qr_ref[0] = _rope(q, cos, sa, sb).astype(BF16)
    o = NSA_QD
    kvc_ref[0] = u[:, o:o + 2 * NSA_KVD]
    o += 2 * NSA_KVD
    ks = _rope(u[:, o:o + NSA_KVD], cos, sa, sb)
    vs = u[:, o + NSA_KVD:o + 2 * NSA_KVD]
    kw = _rope(u[:, o + 2 * NSA_KVD:o + 3 * NSA_KVD], cos, sa, sb)
    vw = u[:, o + 3 * NSA_KVD:o + 4 * NSA_KVD]
    kv4_ref[0] = jnp.concatenate([ks, vs, kw, vw], axis=1).astype(BF16)
    gate_ref[0] = jax.nn.sigmoid(u[:, o + 4 * NSA_KVD:])


def _inproj1(x, mod_l, g, w, rope_tabs, *, tm):
    bsz, s, d = x.shape
    n = w.shape[1]
    row = lambda b, i: (b, i, 0)
    tab = pl.BlockSpec((tm, LANES), lambda b, i: (i, 0))
    widths = (NSA_QD, NSA_QD, 2 * NSA_KVD, 4 * NSA_KVD, LANES)
    dtypes = (BF16, BF16, F32, BF16, F32)
    return pl.pallas_call(
        _inproj1_kernel,
        out_shape=tuple(jax.ShapeDtypeStruct((bsz, s, wd), dt) for wd, dt in zip(widths, dtypes)),
        grid=(bsz, s // tm),
        in_specs=[
            pl.BlockSpec((1, tm, d), row),
            pl.BlockSpec((1, 9, d), lambda b, i: (b, 0, 0)),
            _resident((1, d)),
            _resident((d, n)),
            tab, tab, tab,
        ],
        out_specs=tuple(pl.BlockSpec((1, tm, wd), row) for wd in widths),
        compiler_params=_params(("parallel", "parallel")),
        name="inproj1",
    )(x, mod_l, g.reshape(1, d), w, *rope_tabs)


def _compress_kernel(c_ref, pos_ref, w1_ref, w2_ref, o_ref):
    nrow = c_ref.shape[3]
    for kv in range(2):
        outs = []
        for g in range(NSA_KV_GROUPS):
            blk = c_ref[0, kv, g]
            top = (blk + pos_ref[kv, 0:1]).astype(BF16)
            bot = (blk + pos_ref[kv, 1:2]).astype(BF16)
            pa = jnp.dot(top, w1_ref[kv, 0], preferred_element_type=F32)
            pb = jnp.dot(bot, w1_ref[kv, 1], preferred_element_type=F32)
            pre = pa + pltpu.roll(pb, nrow - 1, axis=0)
            hid = _gelu_tanh(pre).astype(BF16)
            outs.append(jnp.dot(hid, w2_ref[kv], preferred_element_type=F32))
        o_ref[0, kv] = jnp.concatenate(outs, axis=1).astype(BF16)


def _compress(cblk, pos, w1, w2):
    bsz, _, _, nrow, flat = cblk.shape
    return pl.pallas_call(
        _compress_kernel,
        out_shape=jax.ShapeDtypeStruct((bsz, 2, nrow, NSA_KVD), BF16),
        grid=(bsz,),
        in_specs=[
            pl.BlockSpec((1, 2, NSA_KV_GROUPS, nrow, flat), lambda b: (b, 0, 0, 0, 0)),
            _resident(pos.shape),
            _resident(w1.shape),
            _resident(w2.shape),
        ],
        out_specs=pl.BlockSpec((1, 2, nrow, NSA_KVD), lambda b: (b, 0, 0, 0)),
        compiler_params=_params(("parallel",)),
        name="nsa_compress",
    )(cblk, pos, w1, w2)


def _nsa_kernel(qn_ref, qr_ref, kv4_ref, cmp_ref, gate_ref, ovl_ref, hot_ref, gexp_ref, o_ref,
                lhs_ref, s_ref, p_ref, pg_ref, m_ref, acc_ref, oc_ref, *, seq, tk):
    tq = 128
    nrow = cmp_ref.shape[2]
    ns = seq // SEL_BLOCK
    n_g, n_r = NSA_KV_GROUPS, NSA_HPG
    n_h = n_g * n_r
    rows = n_h * tq
    t0 = pl.program_id(1) * tq
    lane = lax.broadcasted_iota(jnp.int32, (tq, LANES), 1)
    gates = gate_ref[0]

    def stack(q_ref):
        parts = []
        for g in range(n_g):
            in_grp = (lane >= g * HEAD_DIM) & (lane < (g + 1) * HEAD_DIM)
            for r in range(n_r):
                ch = q_ref[0, :, r * LANES:(r + 1) * LANES]
                parts.append(jnp.where(in_grp, ch, jnp.zeros_like(ch)))
        return jnp.concatenate(parts, axis=0)

    half = rows // n_g
    neg_stats = jnp.full((rows, LANES), NEG, F32)

    def times_values(p_view, v, g):
        vl = lax.broadcasted_iota(jnp.int32, v.shape, 1)
        vg = jnp.where((vl >= g * HEAD_DIM) & (vl < (g + 1) * HEAD_DIM), v, jnp.ones_like(v))
        return jnp.dot(p_view[g * half:(g + 1) * half, :], vg, preferred_element_type=F32)

    def compressed(width):
        sc_ref, pc_ref = s_ref.at[:, 0:width], p_ref.at[:, 0:width]
        tq_c = t0 + lax.broadcasted_iota(jnp.int32, (tq, width), 0)
        cend = lax.broadcasted_iota(jnp.int32, (tq, width), 1) * CMP_STRIDE + (CMP_LEN - 1)
        sc_ref[...] = lax.dot_general(stack(qn_ref), cmp_ref[0, 0, 0:width, :], NT_DIMS, preferred_element_type=F32)
        _, e_sum = _stream_softmax(sc_ref, pc_ref, neg_stats, jnp.where(cend <= tq_c, 0.0, NEG), tq, keep_f32=True)
        has_block = jnp.where(t0 + lax.broadcasted_iota(jnp.int32, (tq, 1), 0) >= CMP_LEN - 1, 1.0, 0.0)
        inv_c = jnp.concatenate([has_block] * n_h, axis=0) / jnp.sum(e_sum, axis=1, keepdims=True)
        inv_c = jnp.broadcast_to(inv_c, (rows, LANES))
        for c in range(rows // SOFTMAX_CHUNK):
            sl = slice(c * SOFTMAX_CHUNK, (c + 1) * SOFTMAX_CHUNK)
            g, q0 = (c * SOFTMAX_CHUNK) // half, (c * SOFTMAX_CHUNK) % tq
            gsl = slice(g * tq + q0, g * tq + q0 + SOFTMAX_CHUNK)
            for j in range(width // LANES):
                cols = slice(j * LANES, (j + 1) * LANES)
                pn = sc_ref[sl, cols] * inv_c[sl]
                pc_ref[sl, cols] = pn.astype(BF16)
                first_head = (c * SOFTMAX_CHUNK) % half < tq
                pg_ref[gsl, cols] = pn if first_head else pg_ref[gsl, cols] + pn
        if width < nrow:
            pg_ref[:, width:nrow] = jnp.zeros((n_g * tq, nrow - width), F32)
        oc_ref[...] = jnp.dot(pc_ref[...], cmp_ref[0, 1, 0:width, :], preferred_element_type=F32)

    compressed(nrow)
    o_c = oc_ref[...]

    imp3 = jnp.dot(_bf16_pieces(pg_ref[...]), ovl_ref[...], preferred_element_type=F32)
    n_q = n_g * tq
    imp_t = (imp3[:n_q] + imp3[n_q:2 * n_q] + imp3[2 * n_q:]).T
    jn = lax.broadcasted_iota(jnp.int32, (LANES, n_g * tq), 0)
    tpos = t0 + lax.broadcasted_iota(jnp.int32, (LANES, tq), 1)
    blk_t = jnp.concatenate([tpos] * n_g, axis=1) // SEL_BLOCK
    valid = jn <= blk_t
    forced = (jn == 0) | (jn == blk_t) | (jn == blk_t - 1)
    others = _pick_topk_rows(jnp.where(valid & jnp.logical_not(forced), imp_t, -jnp.inf), min(SEL_TOPK, ns) - 3)
    keep = valid & (forced | (others > 0.5))
    drop = jnp.where(keep, 0.0, -BIG).T.astype(BF16)
    lhs_ref[...] = jnp.concatenate(
        [stack(qr_ref), jnp.concatenate([drop[g * tq:(g + 1) * tq] for g in range(n_g) for _ in range(n_r)], axis=0)],
        axis=1)

    wk = WINDOW + tq
    sw_ref, pw_ref = s_ref.at[:, 0:wk], p_ref.at[:, 0:wk]
    kstart = pl.multiple_of(jnp.clip(t0 - WINDOW, 0, seq - wk), tq)
    diff = (t0 + lax.broadcasted_iota(jnp.int32, (tq, wk), 0)) - (
        kstart + lax.broadcasted_iota(jnp.int32, (tq, wk), 1))
    sw_ref[...] = lax.dot_general(lhs_ref[:, 0:LANES], kv4_ref[0, pl.ds(kstart, wk), 2 * LANES:3 * LANES], NT_DIMS,
                                  preferred_element_type=F32)
    _stream_softmax(sw_ref, pw_ref, neg_stats, jnp.where((diff >= 0) & (diff < WINDOW), 0.0, NEG), tq)
    acc_w = jnp.concatenate([times_values(pw_ref, kv4_ref[0, pl.ds(kstart, wk), 3 * LANES:4 * LANES], g)
                             for g in range(n_g)], axis=0)

    ss_ref, ps_ref = s_ref.at[:, 0:tk], p_ref.at[:, 0:tk]
    m_ref[...] = neg_stats
    acc_ref[...] = jnp.zeros(acc_ref.shape, F32)

    def sel_tile(kt, bias):
        k0 = pl.multiple_of(kt * tk, tk)
        rhs = jnp.concatenate([kv4_ref[0, pl.ds(k0, tk), 0:LANES], hot_ref[pl.ds(k0, tk), :]], axis=1)
        ss_ref[...] = lax.dot_general(lhs_ref[...], rhs, NT_DIMS, preferred_element_type=F32)
        m_old = m_ref[...]
        m_new, _ = _stream_softmax(ss_ref, ps_ref, m_old, bias, tq)
        m_ref[...] = m_new
        alpha = jnp.exp2(m_old - m_new)
        for g in range(n_g):
            gs = slice(g * half, (g + 1) * half)
            acc_ref[gs, :] = alpha[gs] * acc_ref[gs, :] + times_values(ps_ref, kv4_ref[0, pl.ds(k0, tk), LANES:2 * LANES], g)

    last = t0 // tk

    @pl.loop(0, last)
    def _(kt):
        sel_tile(kt, None)

    @pl.when(last >= 0)
    def _():
        kpos = last * tk + lax.broadcasted_iota(jnp.int32, (tq, tk), 1)
        tq_s = t0 + lax.broadcasted_iota(jnp.int32, (tq, tk), 0)
        sel_tile(last, jnp.where(kpos <= tq_s, 0.0, NEG))

    acc_s = acc_ref[...]

    branches = (o_c, acc_s / pltpu.roll(acc_s, HEAD_DIM, axis=1), acc_w / pltpu.roll(acc_w, HEAD_DIM, axis=1))
    gates_b = gates.astype(BF16)
    spread = [jnp.dot(gates_b, gexp_ref[c], preferred_element_type=F32) for c in range(3)]
    for r in range(n_r):
        y = None
        for c in range(3):
            heads = [branches[c][(g * n_r + r) * tq:(g * n_r + r + 1) * tq] for g in range(n_g)]
            term = spread[c][:, r * LANES:(r + 1) * LANES] * jnp.where(lane < HEAD_DIM, heads[0], heads[1])
            y = term if y is None else y + term
        o_ref[0, :, r * LANES:(r + 1) * LANES] = y.astype(BF16)


def _nsa(qn, qr, kv4, cmp, gates, *, tk):
    bsz, s, _ = qn.shape
    tq = 128
    ns = s // SEL_BLOCK
    assert ns <= LANES
    assert min(SEL_TOPK, ns) >= 3
    ovl = jnp.pad(_overlap(cmp.shape[2], ns), ((0, 0), (0, LANES - ns))).astype(BF16)
    hot = _block_one_hot(s, SEL_BLOCK)
    head_of_p = _nsa_head_perm() // HEAD_DIM
    gexp = (jnp.arange(LANES)[None, :, None] == 3 * head_of_p[None, None, :] + jnp.arange(3)[:, None, None]).astype(BF16)
    rows = NSA_HEADS * tq
    width = max(cmp.shape[2], WINDOW + tq, tk)
    row = lambda b, i: (b, i, 0)
    return pl.pallas_call(
        functools.partial(_nsa_kernel, seq=s, tk=tk),
        out_shape=jax.ShapeDtypeStruct((bsz, s, NSA_QD), BF16),
        grid=(bsz, s // tq),
        in_specs=[
            pl.BlockSpec((1, tq, NSA_QD), row),
            pl.BlockSpec((1, tq, NSA_QD), row),
            pl.BlockSpec((1, s, 4 * NSA_KVD), lambda b, i: (b, 0, 0)),
            pl.BlockSpec((1, 2) + cmp.shape[2:], lambda b, i: (b, 0, 0, 0)),
            pl.BlockSpec((1, tq, LANES), row),
            _resident(ovl.shape),
            _resident(hot.shape),
            _resident(gexp.shape),
        ],
        out_specs=pl.BlockSpec((1, tq, NSA_QD), row),
        scratch_shapes=[pltpu.VMEM((rows, 2 * LANES), BF16), pltpu.VMEM((rows, width), F32),
                        pltpu.VMEM((rows, width), BF16), pltpu.VMEM((NSA_KV_GROUPS * tq, cmp.shape[2]), F32),
                        pltpu.VMEM((rows, LANES), F32), pltpu.VMEM((rows, LANES), F32),
                        pltpu.VMEM((rows, LANES), F32)],
        compiler_params=_params(("parallel", "parallel")),
        name="nsa_attn",
    )(qn, qr, kv4, cmp, gates, ovl, hot, gexp)


def _rope_tables(s):
    half = ROT_DIM // 2
    pos = jnp.arange(s, dtype=F32)
    freqs = ROPE_THETA ** (-jnp.arange(half, dtype=F32) * 2.0 / ROT_DIM)
    ang = pos[:, None] * freqs[None, :]
    cos, sin = jnp.cos(ang), jnp.sin(ang)
    ones = jnp.ones((s, HEAD_DIM - ROT_DIM), F32)
    zeros = jnp.zeros((s, HEAD_DIM - ROT_DIM), F32)
    zh = jnp.zeros((s, half), F32)
    cos_t = jnp.concatenate([cos, cos, ones], axis=1)
    sa_t = jnp.concatenate([-sin, zh, zeros], axis=1)
    sb_t = jnp.concatenate([zh, sin, zeros], axis=1)
    return tuple(jnp.tile(t, (1, LANES // HEAD_DIM)) for t in (cos_t, sa_t, sb_t))


def _block_one_hot(s, block):
    return (jnp.arange(s)[:, None] // block == jnp.arange(LANES)[None, :]).astype(BF16)


def _block_diag(w):
    n, a, b = w.shape
    eye = jnp.eye(n, dtype=w.dtype)
    return (eye[:, None, :, None] * w[:, :, None, :]).reshape(n * a, n * b)


def _nsa_head_perm():
    r = jnp.arange(NSA_HPG)[:, None, None]
    g = jnp.arange(NSA_KV_GROUPS)[None, :, None]
    dd = jnp.arange(HEAD_DIM)[None, None, :]
    return (HEAD_DIM * (NSA_HPG * g + r) + dd).reshape(-1)


def _overlap(nrow, ns):
    ci = jnp.arange(nrow)[:, None] * CMP_STRIDE
    sj = jnp.arange(ns)[None, :] * SEL_BLOCK
    ov = jnp.clip(jnp.minimum(ci + CMP_LEN, sj + SEL_BLOCK) - jnp.maximum(ci, sj), 0, None)
    return ov.astype(F32) / CMP_LEN


def kernel(x, c, mod_w, mod_b, norm_g, ffn_w1, ffn_w2, mix0_in_w, lru_conv_w, lru_conv_b, lru_wa, lru_ba,
           lru_wx, lru_bx, lru_lambda, mix0_out_w, mix1_in_w, cmp_pos, cmp_w1, cmp_w2, mix1_out_w, final_norm_g):
    bsz, s, d = x.shape
    depth = mod_w.shape[0]
    assert d == D_MODEL and s % 512 == 0 and s >= WINDOW + 128
    tm = 512

    mod = _modulation(c, mod_w, mod_b)
    tabs = _rope_tables(s)
    w1a = ffn_w1[..., :D_FF].astype(BF16)
    w1b = ffn_w1[..., D_FF:].astype(BF16)
    w2 = ffn_w2.astype(BF16)

    for l in range(depth):
        j = l // 2
        x = _ffn(x, mod[l], norm_g[l, 0], w1a[l, 0], w1b[l, 0], w2[l, 0], final_norm_g,
                 row0=0, final=False, tm=tm)
        if l % 2 == 0:
            xg, q, k, v, cent = _inproj0(x, mod[l], norm_g[l, 1], mix0_in_w[j].astype(BF16), tabs)
            y_lru = _lru(xg, lru_conv_w[j], lru_conv_b[j],
                         _block_diag(lru_wa[j]).astype(BF16), lru_ba[j].reshape(-1),
                         _block_diag(lru_wx[j]).astype(BF16), lru_bx[j].reshape(-1), lru_lambda[j], tm=256)
            y_att = _moba(q, k, v, cent.reshape(bsz, -1, MOBA_WIDTH))
            wo = mix0_out_w[j].astype(BF16)
            mix = ((y_lru, wo[:LRU_WIDTH]), (y_att, wo[LRU_WIDTH:]))
        else:
            perm = _nsa_head_perm()
            w_in = mix1_in_w[j]
            w_in = jnp.concatenate([w_in[:, :NSA_QD][:, perm], w_in[:, NSA_QD:]], axis=1)
            w_in = jnp.pad(w_in, ((0, 0), (0, IN1_PAD - w_in.shape[1]))).astype(BF16)
            qn, qr, kvc, kv4, gates = _inproj1(x, mod[l], norm_g[l, 1], w_in, tabs, tm=tm)
            nrow = s // CMP_STRIDE
            cblk = kvc.reshape(bsz, nrow, CMP_STRIDE, 2, NSA_KV_GROUPS, HEAD_DIM)
            cblk = cblk.transpose(0, 3, 4, 1, 2, 5).reshape(bsz, 2, NSA_KV_GROUPS, nrow, CMP_STRIDE * HEAD_DIM)
            half = CMP_STRIDE * HEAD_DIM
            cmp = _compress(cblk, cmp_pos[j].reshape(2, 2, half),
                            cmp_w1[j].reshape(2, 2, half, -1).astype(BF16), cmp_w2[j].astype(BF16))
            y = _nsa(qn, qr, kv4, cmp, gates, tk=512)
            mix = ((y, mix1_out_w[j][perm].astype(BF16)),)
        x = _ffn(x, mod[l], norm_g[l, 2], w1a[l, 1], w1b[l, 1], w2[l, 1], final_norm_g,
                 row0=6, final=(l == depth - 1), tm=tm, mix=mix)
    return x
```

```python
import functools
import math

import jax
import jax.numpy as jnp
from jax import lax
from jax.experimental import pallas as pl
from jax.experimental.pallas import tpu as pltpu

F32 = jnp.float32
BF16 = jnp.bfloat16

D_MODEL = 1024
D_FF = 2816
EPS = 1e-6
ROPE_THETA = 500000.0
HEAD_DIM = 64
ROT_DIM = HEAD_DIM // 4
NEG = -1e30
BIG = 1e30
LANES = 128

LRU_WIDTH = D_MODEL // 2
LRU_BLOCKS = 8
CONV_WIDTH = 4
LRU_C = 8.0
MOBA_WIDTH = D_MODEL // 2
MOBA_BLOCK = 256
MOBA_TOPK = 3
MOBA_TQ = 2 * MOBA_BLOCK
MOBA_CHUNKS = 2

NSA_HEADS = D_MODEL // HEAD_DIM
NSA_KV_GROUPS = 2
NSA_HPG = NSA_HEADS // NSA_KV_GROUPS
CMP_LEN = 32
CMP_STRIDE = 16
SEL_BLOCK = 64
SEL_TOPK = 16
WINDOW = 512
NSA_QD = NSA_HEADS * HEAD_DIM
NSA_KVD = NSA_KV_GROUPS * HEAD_DIM
IN1_PAD = NSA_QD + 6 * NSA_KVD + LANES

VMEM_LIMIT = 52 * 1024 * 1024

NT_DIMS = (((1,), (1,)), ((), ()))
Q_PRESCALE = HEAD_DIM ** -0.5 * math.log2(math.e)


def _params(sem):
    return pltpu.CompilerParams(dimension_semantics=sem, vmem_limit_bytes=VMEM_LIMIT)


def _resident(shape):
    nd = len(shape)
    return pl.BlockSpec(shape, lambda *_: (0,) * nd, pipeline_mode=pl.Buffered(1))


def _silu(a):
    return a * jax.nn.sigmoid(a)


def _gelu_tanh(x):
    cdf = 0.5 * (1.0 + jnp.tanh(math.sqrt(2.0 / math.pi) * (x + 0.044715 * (x * x * x))))
    return x * cdf


def _rmsnorm(x, g):
    return x * lax.rsqrt(jnp.mean(x * x, axis=-1, keepdims=True) + EPS) * g


def _norm_mod(x, g, shift, scale):
    return _rmsnorm(x, g) * (1.0 + scale) + shift


def _rope(u, cos, sa, sb):
    outs = []
    for j in range(u.shape[1] // LANES):
        c = u[:, j * LANES:(j + 1) * LANES]
        up = pltpu.roll(c, LANES - ROT_DIM // 2, axis=1)
        dn = pltpu.roll(c, ROT_DIM // 2, axis=1)
        outs.append(c * cos + up * sa + dn * sb)
    return jnp.concatenate(outs, axis=1)


def _bf16_pieces(x):
    hi = x.astype(BF16)
    r1 = x - hi.astype(F32)
    mid = r1.astype(BF16)
    lo = (r1 - mid.astype(F32)).astype(BF16)
    return jnp.concatenate([hi, mid, lo], axis=0)


SOFTMAX_CHUNK = 64


def _stream_softmax(s_ref, p_ref, m_old, bias, tq):
    rows, n = s_ref.shape
    n_chunks, n_l = rows // SOFTMAX_CHUNK, n // LANES

    def chunk(c):
        sl = slice(c * SOFTMAX_CHUNK, (c + 1) * SOFTMAX_CHUNK)
        sc = s_ref[sl, :]
        if bias is not None:
            q0 = (c * SOFTMAX_CHUNK) % tq
            sc = sc + bias[q0:q0 + SOFTMAX_CHUNK]
        return sl, [sc[:, j * LANES:(j + 1) * LANES] for j in range(n_l)]

    partial_max = jnp.concatenate([functools.reduce(jnp.maximum, chunk(c)[1]) for c in range(n_chunks)], axis=0)
    m_new = jnp.maximum(m_old, jnp.broadcast_to(jnp.max(partial_max, axis=1, keepdims=True), (rows, LANES)))
    sums = []
    for c in range(n_chunks):
        sl, cols = chunk(c)
        m_c = m_new[sl]
        ps = [jnp.exp2(col - m_c) for col in cols]
        for j, pj in enumerate(ps):
            p_ref[sl, j * LANES:(j + 1) * LANES] = pj.astype(BF16)
        sums.append(functools.reduce(jnp.add, ps))
    return m_new, jnp.concatenate(sums, axis=0)


def _pick_topk_rows(g, k):
    n_ids = g.shape[0]
    ids = lax.broadcasted_iota(jnp.int32, g.shape, 0).astype(F32)
    picked = jnp.zeros(g.shape, F32)
    for _ in range(k):
        mx = jnp.max(g, axis=0, keepdims=True)
        first = jnp.min(jnp.where(g == mx, ids, float(n_ids)), axis=0, keepdims=True)
        hit = ids == first
        picked = jnp.where(hit, 1.0, picked)
        g = jnp.where(hit, -jnp.inf, g)
    return picked


def _mod_kernel(c_ref, w_ref, b_ref, o_ref):
    cond = _silu(c_ref[...])
    o_ref[0] = jnp.dot(cond.astype(BF16), w_ref[0].astype(BF16), preferred_element_type=F32) + b_ref[0]


def _modulation(c, mod_w, mod_b):
    depth, d, n = mod_w.shape
    bsz = c.shape[0]
    bp = -(-bsz // 8) * 8
    cp = jnp.pad(c, ((0, bp - bsz), (0, 0)))
    tn = 1152
    out = pl.pallas_call(
        _mod_kernel,
        out_shape=jax.ShapeDtypeStruct((depth, bp, n), F32),
        grid=(depth, n // tn),
        in_specs=[
            pl.BlockSpec((bp, d), lambda l, j: (0, 0)),
            pl.BlockSpec((1, d, tn), lambda l, j: (l, 0, j)),
            pl.BlockSpec((1, 1, tn), lambda l, j: (l, 0, j)),
        ],
        out_specs=pl.BlockSpec((1, bp, tn), lambda l, j: (l, 0, j)),
        compiler_params=_params(("parallel", "parallel")),
        name="adaln_mod",
    )(cp, mod_w, mod_b.reshape(depth, 1, n))
    return out[:, :bsz].reshape(depth, bsz, 9, d)


def _ffn_kernel(x_ref, mod_ref, g_ref, w1a_ref, w1b_ref, w2_ref, gf_ref, *rest, row0, final, n_mix):
    o_ref = rest[-1]
    x = x_ref[0]
    m = mod_ref[0]
    if n_mix:
        mixed = sum(jnp.dot(y_ref[0], w_ref[...], preferred_element_type=F32)
                    for y_ref, w_ref in zip(rest[:n_mix], rest[n_mix:2 * n_mix]))
        x = x + m[5:6] * mixed
    h = _norm_mod(x, g_ref[...], m[row0:row0 + 1], m[row0 + 1:row0 + 2]).astype(BF16)
    a = jnp.dot(h, w1a_ref[...], preferred_element_type=F32)
    b = jnp.dot(h, w1b_ref[...], preferred_element_type=F32)
    act = (_silu(a) * b).astype(BF16)
    y = jnp.dot(act, w2_ref[...], preferred_element_type=F32)
    out = x + (0.5 * m[row0 + 2:row0 + 3]) * y
    if final:
        out = _rmsnorm(out, gf_ref[...])
    o_ref[0] = out


def _ffn(x, mod_l, g, w1a, w1b, w2, gf, *, row0, final, tm, mix=()):
    bsz, s, d = x.shape
    kern = functools.partial(_ffn_kernel, row0=row0, final=final, n_mix=len(mix))
    row = lambda b, i: (b, i, 0)
    return pl.pallas_call(
        kern,
        out_shape=jax.ShapeDtypeStruct(x.shape, F32),
        grid=(bsz, s // tm),
        in_specs=[
            pl.BlockSpec((1, tm, d), row),
            pl.BlockSpec((1, 9, d), lambda b, i: (b, 0, 0)),
            _resident((1, d)),
            _resident(w1a.shape),
            _resident(w1b.shape),
            _resident(w2.shape),
            _resident((1, d)),
        ] + [pl.BlockSpec((1, tm, y.shape[2]), row) for y, _ in mix] + [_resident(w.shape) for _, w in mix],
        out_specs=pl.BlockSpec((1, tm, d), row),
        compiler_params=_params(("parallel", "parallel")),
        name="ffn_final" if final else ("mix_ffn" if mix else "ffn"),
    )(x, mod_l, g.reshape(1, d), w1a, w1b, w2, gf.reshape(1, d), *[y for y, _ in mix], *[w for _, w in mix])


def _inproj0_kernel(x_ref, mod_ref, g_ref, w_ref, cos_ref, sa_ref, sb_ref,
                    xg_ref, q_ref, k_ref, v_ref, cent_ref):
    m = mod_ref[0]
    h = _norm_mod(x_ref[0], g_ref[...], m[3:4], m[4:5]).astype(BF16)
    u = jnp.dot(h, w_ref[...], preferred_element_type=F32)
    o1 = 2 * LRU_WIDTH
    xg_ref[0] = u[:, :o1]
    cos, sa, sb = cos_ref[...], sa_ref[...], sb_ref[...]
    q = _rope(u[:, o1:o1 + MOBA_WIDTH], cos, sa, sb)
    k = _rope(u[:, o1 + MOBA_WIDTH:o1 + 2 * MOBA_WIDTH], cos, sa, sb)
    q_ref[0] = (q * Q_PRESCALE).astype(BF16)
    k_ref[0] = k.astype(BF16)
    v_ref[0] = u[:, o1 + 2 * MOBA_WIDTH:].astype(BF16)
    cent_ref[0, 0] = jnp.mean(k, axis=0, keepdims=True)


def _inproj0(x, mod_l, g, w, rope_tabs):
    bsz, s, d = x.shape
    tm = MOBA_BLOCK
    nb = s // tm
    n = w.shape[1]
    row = lambda b, i: (b, i, 0)
    tab = pl.BlockSpec((tm, LANES), lambda b, i: (i, 0))
    return pl.pallas_call(
        _inproj0_kernel,
        out_shape=(
            jax.ShapeDtypeStruct((bsz, s, 2 * LRU_WIDTH), F32),
            jax.ShapeDtypeStruct((bsz, s, MOBA_WIDTH), BF16),
            jax.ShapeDtypeStruct((bsz, s, MOBA_WIDTH), BF16),
            jax.ShapeDtypeStruct((bsz, s, MOBA_WIDTH), BF16),
            jax.ShapeDtypeStruct((bsz, nb, 1, MOBA_WIDTH), F32),
        ),
        grid=(bsz, nb),
        in_specs=[
            pl.BlockSpec((1, tm, d), row),
            pl.BlockSpec((1, 9, d), lambda b, i: (b, 0, 0)),
            _resident((1, d)),
            _resident((d, n)),
            tab, tab, tab,
        ],
        out_specs=(
            pl.BlockSpec((1, tm, 2 * LRU_WIDTH), row),
            pl.BlockSpec((1, tm, MOBA_WIDTH), row),
            pl.BlockSpec((1, tm, MOBA_WIDTH), row),
            pl.BlockSpec((1, tm, MOBA_WIDTH), row),
            pl.BlockSpec((1, 1, 1, MOBA_WIDTH), lambda b, i: (b, i, 0, 0)),
        ),
        compiler_params=_params(("parallel", "parallel")),
        name="inproj0",
    )(x, mod_l, g.reshape(1, d), w, *rope_tabs)


def _lru_kernel(xg_ref, cw_ref, cb_ref, wa_ref, ba_ref, wx_ref, bx_ref, lam_ref, y_ref, xbuf, hprev, *, tm):
    c = LRU_WIDTH
    halo = 8

    @pl.when(pl.program_id(1) == 0)
    def _():
        xbuf[0:halo, :] = jnp.zeros((halo, c), F32)
        hprev[...] = jnp.zeros(hprev.shape, F32)

    x = xg_ref[0, :, 0:c]
    gate_in = xg_ref[0, :, c:2 * c]
    xbuf[halo:halo + tm, :] = x
    cw = cw_ref[...]
    xc = cb_ref[...] + cw[CONV_WIDTH - 1:CONV_WIDTH] * x
    for kk in range(CONV_WIDTH - 1):
        back = CONV_WIDTH - 1 - kk
        xc = xc + cw[kk:kk + 1] * xbuf[halo - back:halo - back + tm, :]
    xbuf[0:halo, :] = x[tm - halo:tm, :]

    xcb = xc.astype(BF16)
    r = jax.nn.sigmoid(jnp.dot(xcb, wa_ref[...], preferred_element_type=F32) + ba_ref[...])
    ig = jax.nn.sigmoid(jnp.dot(xcb, wx_ref[...], preferred_element_type=F32) + bx_ref[...])
    nlam = -lam_ref[...]
    softplus = jnp.maximum(nlam, 0.0) + jnp.log1p(jnp.exp(-jnp.abs(nlam)))
    log_a = (-LRU_C * r) * softplus
    a = jnp.exp(log_a)
    mult = jnp.sqrt(-jnp.tanh(log_a) * (a * a + 1.0))
    bt = mult * (ig * xc)

    row = lax.broadcasted_iota(jnp.int32, (tm, c), 0)
    step = 1
    while step < tm:
        a_sh = pltpu.roll(a, step, axis=0)
        b_sh = pltpu.roll(bt, step, axis=0)
        valid = row >= step
        bt = jnp.where(valid, a * b_sh + bt, bt)
        a = jnp.where(valid, a * a_sh, a)
        step *= 2
    h = bt + a * hprev[0:1, :]
    hprev[0:1, :] = h[tm - 1:tm, :]
    y_ref[0] = (h * _gelu_tanh(gate_in)).astype(BF16)


def _lru(xg, cw, cb, wa_bd, ba, wx_bd, bx, lam, *, tm):
    bsz, s, _ = xg.shape
    c = LRU_WIDTH
    vec = _resident((1, c))
    return pl.pallas_call(
        functools.partial(_lru_kernel, tm=tm),
        out_shape=jax.ShapeDtypeStruct((bsz, s, c), BF16),
        grid=(bsz, s // tm),
        in_specs=[
            pl.BlockSpec((1, tm, 2 * c), lambda b, i: (b, i, 0)),
            _resident((CONV_WIDTH, c)), vec,
            _resident((c, c)), vec,
            _resident((c, c)), vec,
            vec,
        ],
        out_specs=pl.BlockSpec((1, tm, c), lambda b, i: (b, i, 0)),
        scratch_shapes=[pltpu.VMEM((tm + 8, c), F32), pltpu.VMEM((8, c), F32)],
        compiler_params=_params(("parallel", "arbitrary")),
        name="rg_lru",
    )(xg, cw, cb.reshape(1, c), wa_bd, ba.reshape(1, c), wx_bd, bx.reshape(1, c), lam.reshape(1, c))


def _moba_kernel(q_ref, k_ref, v_ref, cent_ref, hot_ref, o_ref, lhs_ref, s_ref, p_ref, m_ref, l_ref, acc_ref):
    tq = MOBA_TQ
    rows = 2 * tq
    nb_pad = -(-(k_ref.shape[1] // MOBA_BLOCK) // 8) * 8
    i = pl.program_id(2)
    lane = lax.broadcasted_iota(jnp.int32, (tq, LANES), 1)
    bid = lax.broadcasted_iota(jnp.int32, (LANES, rows), 0)
    qidx = jnp.concatenate([lax.broadcasted_iota(jnp.int32, (LANES, tq), 1)] * 2, axis=1)
    own = (tq // MOBA_BLOCK) * i + qidx // MOBA_BLOCK
    past = bid < own
    qidx_r = jnp.concatenate([lax.broadcasted_iota(jnp.int32, (nb_pad, tq), 1)] * 2, axis=1)
    past_real = lax.broadcasted_iota(jnp.int32, (nb_pad, rows), 0) < (tq // MOBA_BLOCK) * i + qidx_r // MOBA_BLOCK
    chunks = [slice(c * LANES, (c + 1) * LANES) for c in range(MOBA_CHUNKS)]

    for c, cl in enumerate(chunks):
        q = q_ref[0, :, cl]
        zero = jnp.zeros_like(q)
        q2 = jnp.concatenate([jnp.where(lane < HEAD_DIM, q, zero), jnp.where(lane >= HEAD_DIM, q, zero)], axis=0)
        gate3 = lax.dot_general(_bf16_pieces(cent_ref[0, :, cl]), q2, NT_DIMS, preferred_element_type=F32)
        gate_t = gate3[:LANES] + gate3[LANES:2 * LANES] + gate3[2 * LANES:]
        sel_t = _pick_topk_rows(jnp.where(past_real, gate_t[:nb_pad], -jnp.inf), MOBA_TOPK)
        sel_t = jnp.concatenate([sel_t, jnp.zeros((LANES - nb_pad, rows), F32)], axis=0)
        drop = jnp.where((past & (sel_t > 0.5)) | (bid == own), 0.0, -BIG).T.astype(BF16)
        lhs_ref[c] = jnp.concatenate([q2, drop], axis=1)
        m_ref[c] = jnp.full((rows, LANES), NEG, F32)
        l_ref[c] = jnp.zeros((rows, LANES), F32)
        acc_ref[c] = jnp.zeros((rows, LANES), F32)

    def tile(kt, bias):
        k0 = pl.multiple_of(kt * tq, tq)
        for c, cl in enumerate(chunks):
            rhs = jnp.concatenate([k_ref[0, pl.ds(k0, tq), cl], hot_ref[pl.ds(k0, tq), :]], axis=1)
            s_ref[c] = lax.dot_general(lhs_ref[c], rhs, NT_DIMS, preferred_element_type=F32)
        for c, cl in enumerate(chunks):
            m_old = m_ref[c]
            m_new, p_sum = _stream_softmax(s_ref.at[c], p_ref.at[c], m_old, bias, tq)
            alpha = jnp.exp2(m_old - m_new)
            m_ref[c] = m_new
            l_ref[c] = alpha * l_ref[c] + p_sum
            acc_ref[c] = alpha * acc_ref[c] + jnp.dot(p_ref[c], v_ref[0, pl.ds(k0, tq), cl],
                                                      preferred_element_type=F32)

    @pl.when(i >= 0)
    def _():
        qpos = lax.broadcasted_iota(jnp.int32, (tq, tq), 0)
        kpos = lax.broadcasted_iota(jnp.int32, (tq, tq), 1)
        tile(i, jnp.where(kpos <= qpos, 0.0, NEG))

    @pl.loop(0, i)
    def _(kt):
        tile(kt, None)

    for c, cl in enumerate(chunks):
        out = acc_ref[c] / jnp.sum(l_ref[c], axis=1, keepdims=True)
        o_ref[0, :, cl] = jnp.where(lane < HEAD_DIM, out[:tq], out[tq:]).astype(BF16)


def _moba(q, k, v, cent):
    bsz, s, w = q.shape
    tq = MOBA_TQ
    nb = s // MOBA_BLOCK
    assert nb <= LANES and s % tq == 0
    cent = jnp.pad(cent, ((0, 0), (0, LANES - nb), (0, 0)))
    hot = _block_one_hot(s, MOBA_BLOCK)
    nc, cw = MOBA_CHUNKS, MOBA_CHUNKS * LANES
    return pl.pallas_call(
        _moba_kernel,
        out_shape=jax.ShapeDtypeStruct((bsz, s, w), BF16),
        grid=(bsz, w // cw, s // tq),
        in_specs=[
            pl.BlockSpec((1, tq, cw), lambda b, hp, i: (b, i, hp)),
            pl.BlockSpec((1, s, cw), lambda b, hp, i: (b, 0, hp)),
            pl.BlockSpec((1, s, cw), lambda b, hp, i: (b, 0, hp)),
            pl.BlockSpec((1, LANES, cw), lambda b, hp, i: (b, 0, hp)),
            _resident(hot.shape),
        ],
        out_specs=pl.BlockSpec((1, tq, cw), lambda b, hp, i: (b, i, hp)),
        scratch_shapes=[pltpu.VMEM((nc, 2 * tq, 2 * LANES), BF16), pltpu.VMEM((nc, 2 * tq, tq), F32),
                        pltpu.VMEM((nc, 2 * tq, tq), BF16)] + [pltpu.VMEM((nc, 2 * tq, LANES), F32)] * 3,
        compiler_params=_params(("parallel", "parallel", "parallel")),
        name="moba_attn",
    )(q, k, v, cent, hot)


def _inproj1_kernel(x_ref, mod_ref, g_ref, w_ref, cos_ref, sa_ref, sb_ref,
                    qn_ref, qr_ref, kvc_ref, kv4_ref, gate_ref):
    m = mod_ref[0]
    h = _norm_mod(x_ref[0], g_ref[...], m[3:4], m[4:5]).astype(BF16)
    u = jnp.dot(h, w_ref[...], preferred_element_type=F32)
    cos, sa, sb = cos_ref[...], sa_ref[...], sb_ref[...]
    q = u[:, :NSA_QD] * Q_PRESCALE
    qn_ref[0] = q.astype(BF16)
    qr_ref[0] = _rope(q, cos, sa, sb).astype(BF16)
    o = NSA_QD
    kvc_ref[0, 0] = u[:, o:o + NSA_KVD]
    kvc_ref[0, 1] = u[:, o + NSA_KVD:o + 2 * NSA_KVD]
    o += 2 * NSA_KVD
    ks = _rope(u[:, o:o + NSA_KVD], cos, sa, sb)
    vs = u[:, o + NSA_KVD:o + 2 * NSA_KVD]
    kw = _rope(u[:, o + 2 * NSA_KVD:o + 3 * NSA_KVD], cos, sa, sb)
    vw = u[:, o + 3 * NSA_KVD:o + 4 * NSA_KVD]
    kv4_ref[0] = jnp.concatenate([ks, vs, kw, vw], axis=1).astype(BF16)
    gate_ref[0] = jax.nn.sigmoid(u[:, o + 4 * NSA_KVD:])


def _inproj1(x, mod_l, g, w, rope_tabs, *, tm):
    bsz, s, d = x.shape
    n = w.shape[1]
    row = lambda b, i: (b, i, 0)
    tab = pl.BlockSpec((tm, LANES), lambda b, i: (i, 0))
    widths = (NSA_QD, NSA_QD, None, 4 * NSA_KVD, LANES)
    dtypes = (BF16, BF16, F32, BF16, F32)
    shape = lambda wd: (bsz, s, wd) if wd else (bsz, 2, s, NSA_KVD)
    spec = lambda wd: pl.BlockSpec((1, tm, wd), row) if wd else pl.BlockSpec((1, 2, tm, NSA_KVD),
                                                                             lambda b, i: (b, 0, i, 0))
    return pl.pallas_call(
        _inproj1_kernel,
        out_shape=tuple(jax.ShapeDtypeStruct(shape(wd), dt) for wd, dt in zip(widths, dtypes)),
        grid=(bsz, s // tm),
        in_specs=[
            pl.BlockSpec((1, tm, d), row),
            pl.BlockSpec((1, 9, d), lambda b, i: (b, 0, 0)),
            _resident((1, d)),
            _resident((d, n)),
            tab, tab, tab,
        ],
        out_specs=tuple(spec(wd) for wd in widths),
        compiler_params=_params(("parallel", "parallel")),
        name="inproj1",
    )(x, mod_l, g.reshape(1, d), w, *rope_tabs)


def _compress_kernel(kvc_ref, pos_ref, w1_ref, w2_ref, o_ref):
    nrow = o_ref.shape[2]
    for kv in range(2):
        top = jnp.zeros((nrow, w1_ref.shape[3]), F32)
        bot = jnp.zeros((nrow, w1_ref.shape[3]), F32)
        for r in range(CMP_STRIDE):
            x = kvc_ref[0, kv, pl.ds(r, nrow, stride=CMP_STRIDE), :]
            top += jnp.dot((x + pos_ref[kv, r:r + 1]).astype(BF16), w1_ref[kv, r], preferred_element_type=F32)
            bot += jnp.dot((x + pos_ref[kv, CMP_STRIDE + r:CMP_STRIDE + r + 1]).astype(BF16),
                           w1_ref[kv, CMP_STRIDE + r], preferred_element_type=F32)
        pre = top + pltpu.roll(bot, nrow - 1, axis=0)
        o_ref[0, kv] = jnp.dot(_gelu_tanh(pre).astype(BF16), w2_ref[kv], preferred_element_type=F32).astype(BF16)


def _compress(kvc, pos, w1, w2):
    bsz, _, s, _ = kvc.shape
    nrow = s // CMP_STRIDE
    eye = jnp.eye(NSA_KV_GROUPS, dtype=F32)
    pos2 = jnp.tile(pos, (1, 1, NSA_KV_GROUPS))
    hidden = w1.shape[-1]
    w1r = w1.reshape(2, CMP_LEN, HEAD_DIM, hidden)
    w1bd = jnp.einsum('gh,kric->krgihc', eye, w1r).reshape(2, CMP_LEN, NSA_KVD, NSA_KV_GROUPS * hidden)
    w2bd = jnp.einsum('gh,kcd->kgchd', eye, w2).reshape(2, NSA_KV_GROUPS * hidden, NSA_KVD)
    return pl.pallas_call(
        _compress_kernel,
        out_shape=jax.ShapeDtypeStruct((bsz, 2, nrow, NSA_KVD), BF16),
        grid=(bsz,),
        in_specs=[
            pl.BlockSpec((1, 2, s, NSA_KVD), lambda b: (b, 0, 0, 0)),
            _resident(pos2.shape),
            _resident(w1bd.shape),
            _resident(w2bd.shape),
        ],
        out_specs=pl.BlockSpec((1, 2, nrow, NSA_KVD), lambda b: (b, 0, 0, 0)),
        compiler_params=_params(("parallel",)),
        name="nsa_compress",
    )(kvc, pos2, w1bd.astype(BF16), w2bd.astype(BF16))


def _nsa_kernel(qn_ref, qr_ref, kv4_ref, cmp_ref, gate_ref, ovl_ref, hot_ref, gexp_ref, o_ref,
                lhs_ref, s_ref, p_ref, m_ref, acc_ref, *, seq, tk):
    tq = 128
    nrow = cmp_ref.shape[2]
    ns = seq // SEL_BLOCK
    n_g, n_r = NSA_KV_GROUPS, NSA_HPG
    n_h = n_g * n_r
    rows = n_h * tq
    t0 = pl.program_id(1) * tq
    lane = lax.broadcasted_iota(jnp.int32, (tq, LANES), 1)
    gates = gate_ref[0]

    def stack(q_ref):
        parts = []
        for g in range(n_g):
            in_grp = (lane >= g * HEAD_DIM) & (lane < (g + 1) * HEAD_DIM)
            for r in range(n_r):
                ch = q_ref[0, :, r * LANES:(r + 1) * LANES]
                parts.append(jnp.where(in_grp, ch, jnp.zeros_like(ch)))
        return jnp.concatenate(parts, axis=0)

    half = rows // n_g
    neg_stats = jnp.full((rows, LANES), NEG, F32)

    def times_values(p_view, v, g, extra=None):
        vl = lax.broadcasted_iota(jnp.int32, v.shape, 1)
        vg = jnp.where((vl >= g * HEAD_DIM) & (vl < (g + 1) * HEAD_DIM), v, jnp.ones_like(v))
        if extra is not None:
            vg = jnp.concatenate([vg, extra], axis=1)
        return jnp.dot(p_view[g * half:(g + 1) * half, :], vg, preferred_element_type=F32)

    sc_ref, pc_ref = s_ref.at[:, 0:nrow], p_ref.at[:, 0:nrow]
    tq_c = t0 + lax.broadcasted_iota(jnp.int32, (tq, nrow), 0)
    cend = lax.broadcasted_iota(jnp.int32, (tq, nrow), 1) * CMP_STRIDE + (CMP_LEN - 1)
    sc_ref[...] = lax.dot_general(stack(qn_ref), cmp_ref[0, 0], NT_DIMS, preferred_element_type=F32)
    _stream_softmax(sc_ref, pc_ref, neg_stats, jnp.where(cend <= tq_c, 0.0, NEG), tq)
    both = jnp.concatenate([times_values(pc_ref, cmp_ref[0, 1], g, extra=ovl_ref[...]) for g in range(n_g)], axis=0)
    acc_c, imp_h = both[:, 0:LANES], both[:, LANES:]
    lane_r = lax.broadcasted_iota(jnp.int32, (rows, LANES), 1)
    grp_r = lax.broadcasted_iota(jnp.int32, (rows, LANES), 0) // half
    den_c = jnp.where(lane_r // HEAD_DIM == grp_r, pltpu.roll(acc_c, HEAD_DIM, axis=1), acc_c)
    has_block = jnp.where(t0 + lax.broadcasted_iota(jnp.int32, (tq, LANES), 0) >= CMP_LEN - 1, 1.0, 0.0)
    inv_c = jnp.concatenate([has_block] * n_h, axis=0) / den_c
    o_c = acc_c * inv_c

    imp_h = imp_h * inv_c
    imp_t = jnp.concatenate([functools.reduce(jnp.add, [imp_h[(g * n_r + r) * tq:(g * n_r + r + 1) * tq]
                                                        for r in range(n_r)]) for g in range(n_g)], axis=0).T
    jn = lax.broadcasted_iota(jnp.int32, (LANES, n_g * tq), 0)
    tpos = t0 + lax.broadcasted_iota(jnp.int32, (LANES, tq), 1)
    blk_t = jnp.concatenate([tpos] * n_g, axis=1) // SEL_BLOCK
    valid = jn <= blk_t
    forced = (jn == 0) | (jn == blk_t) | (jn == blk_t - 1)
    others = _pick_topk_rows(jnp.where(valid & jnp.logical_not(forced), imp_t, -jnp.inf), min(SEL_TOPK, ns) - 3)
    keep = valid & (forced | (others > 0.5))
    drop = jnp.where(keep, 0.0, -BIG).T.astype(BF16)
    lhs_ref[...] = jnp.concatenate(
        [stack(qr_ref), jnp.concatenate([drop[g * tq:(g + 1) * tq] for g in range(n_g) for _ in range(n_r)], axis=0)],
        axis=1)

    wk = WINDOW + tq
    sw_ref, pw_ref = s_ref.at[:, 0:wk], p_ref.at[:, 0:wk]
    kstart = pl.multiple_of(jnp.clip(t0 - WINDOW, 0, seq - wk), tq)
    diff = (t0 + lax.broadcasted_iota(jnp.int32, (tq, wk), 0)) - (
        kstart + lax.broadcasted_iota(jnp.int32, (tq, wk), 1))
    sw_ref[...] = lax.dot_general(lhs_ref[:, 0:LANES], kv4_ref[0, pl.ds(kstart, wk), 2 * LANES:3 * LANES], NT_DIMS,
                                  preferred_element_type=F32)
    _stream_softmax(sw_ref, pw_ref, neg_stats, jnp.where((diff >= 0) & (diff < WINDOW), 0.0, NEG), tq)
    acc_w = jnp.concatenate([times_values(pw_ref, kv4_ref[0, pl.ds(kstart, wk), 3 * LANES:4 * LANES], g)
                             for g in range(n_g)], axis=0)

    m_ref[...] = neg_stats
    acc_ref[...] = jnp.zeros(acc_ref.shape, F32)

    def sel_tile(k0, width, bias):
        rhs = jnp.concatenate([kv4_ref[0, pl.ds(k0, width), 0:LANES], hot_ref[pl.ds(k0, width), :]], axis=1)
        views = []
        for g in range(n_g):
            gs = slice(g * half, (g + 1) * half)
            sg_ref = s_ref.at[gs, 0:width]
            sg_ref[...] = lax.dot_general(lhs_ref[gs, :], rhs, NT_DIMS, preferred_element_type=F32)
            views.append((gs, sg_ref, p_ref.at[gs, 0:width]))
        for g, (gs, sg_ref, pg_ref) in enumerate(views):
            m_old = m_ref[gs, :]
            m_new, _ = _stream_softmax(sg_ref, pg_ref, m_old, bias, tq)
            m_ref[gs, :] = m_new
            vl = lax.broadcasted_iota(jnp.int32, (width, LANES), 1)
            v = kv4_ref[0, pl.ds(k0, width), LANES:2 * LANES]
            vg = jnp.where((vl >= g * HEAD_DIM) & (vl < (g + 1) * HEAD_DIM), v, jnp.ones_like(v))
            acc_ref[gs, :] = jnp.exp2(m_old - m_new) * acc_ref[gs, :] + jnp.dot(pg_ref[...], vg,
                                                                                preferred_element_type=F32)

    last = t0 // tk

    @pl.loop(0, last)
    def _(kt):
        sel_tile(pl.multiple_of(kt * tk, tk), tk, None)

    k_last = pl.multiple_of(last * tk, tk)

    def last_tile(width):
        kpos = k_last + lax.broadcasted_iota(jnp.int32, (tq, width), 1)
        tq_s = t0 + lax.broadcasted_iota(jnp.int32, (tq, width), 0)
        sel_tile(k_last, width, jnp.where(kpos <= tq_s, 0.0, NEG))

    in_first_half = t0 - k_last + tq <= tk // 2
    pl.when(in_first_half)(lambda: last_tile(tk // 2))
    pl.when(jnp.logical_not(in_first_half))(lambda: last_tile(tk))

    acc_s = acc_ref[...]

    branches = (o_c, acc_s / pltpu.roll(acc_s, HEAD_DIM, axis=1), acc_w / pltpu.roll(acc_w, HEAD_DIM, axis=1))
    gates_b = gates.astype(BF16)
    spread = [jnp.dot(gates_b, gexp_ref[c], preferred_element_type=F32) for c in range(3)]
    for r in range(n_r):
        y = None
        for c in range(3):
            heads = [branches[c][(g * n_r + r) * tq:(g * n_r + r + 1) * tq] for g in range(n_g)]
            term = spread[c][:, r * LANES:(r + 1) * LANES] * jnp.where(lane < HEAD_DIM, heads[0], heads[1])
            y = term if y is None else y + term
        o_ref[0, :, r * LANES:(r + 1) * LANES] = y.astype(BF16)


def _nsa(qn, qr, kv4, cmp, gates, *, tk):
    bsz, s, _ = qn.shape
    tq = 128
    ns = s // SEL_BLOCK
    assert ns <= LANES
    assert min(SEL_TOPK, ns) >= 3
    ovl = jnp.pad(_overlap(cmp.shape[2], ns), ((0, 0), (0, LANES - ns))).astype(BF16)
    hot = _block_one_hot(s, SEL_BLOCK)
    head_of_p = _nsa_head_perm() // HEAD_DIM
    gexp = (jnp.arange(LANES)[None, :, None] == 3 * head_of_p[None, None, :] + jnp.arange(3)[:, None, None]).astype(BF16)
    rows = NSA_HEADS * tq
    width = max(cmp.shape[2], WINDOW + tq, tk)
    row = lambda b, i: (b, i, 0)
    return pl.pallas_call(
        functools.partial(_nsa_kernel, seq=s, tk=tk),
        out_shape=jax.ShapeDtypeStruct((bsz, s, NSA_QD), BF16),
        grid=(bsz, s // tq),
        in_specs=[
            pl.BlockSpec((1, tq, NSA_QD), row),
            pl.BlockSpec((1, tq, NSA_QD), row),
            pl.BlockSpec((1, s, 4 * NSA_KVD), lambda b, i: (b, 0, 0)),
            pl.BlockSpec((1, 2) + cmp.shape[2:], lambda b, i: (b, 0, 0, 0)),
            pl.BlockSpec((1, tq, LANES), row),
            _resident(ovl.shape),
            _resident(hot.shape),
            _resident(gexp.shape),
        ],
        out_specs=pl.BlockSpec((1, tq, NSA_QD), row),
        scratch_shapes=[pltpu.VMEM((rows, 2 * LANES), BF16), pltpu.VMEM((rows, width), F32),
                        pltpu.VMEM((rows, width), BF16),
                        pltpu.VMEM((rows, LANES), F32), pltpu.VMEM((rows, LANES), F32)],
        compiler_params=_params(("parallel", "parallel")),
        name="nsa_attn",
    )(qn, qr, kv4, cmp, gates, ovl, hot, gexp)


def _rope_tables(s):
    half = ROT_DIM // 2
    pos = jnp.arange(s, dtype=F32)
    freqs = ROPE_THETA ** (-jnp.arange(half, dtype=F32) * 2.0 / ROT_DIM)
    ang = pos[:, None] * freqs[None, :]
    cos, sin = jnp.cos(ang), jnp.sin(ang)
    ones = jnp.ones((s, HEAD_DIM - ROT_DIM), F32)
    zeros = jnp.zeros((s, HEAD_DIM - ROT_DIM), F32)
    zh = jnp.zeros((s, half), F32)
    cos_t = jnp.concatenate([cos, cos, ones], axis=1)
    sa_t = jnp.concatenate([-sin, zh, zeros], axis=1)
    sb_t = jnp.concatenate([zh, sin, zeros], axis=1)
    return tuple(jnp.tile(t, (1, LANES // HEAD_DIM)) for t in (cos_t, sa_t, sb_t))


def _block_one_hot(s, block):
    return (jnp.arange(s)[:, None] // block == jnp.arange(LANES)[None, :]).astype(BF16)


def _block_diag(w):
    n, a, b = w.shape
    eye = jnp.eye(n, dtype=w.dtype)
    return (eye[:, None, :, None] * w[:, :, None, :]).reshape(n * a, n * b)


def _nsa_head_perm():
    r = jnp.arange(NSA_HPG)[:, None, None]
    g = jnp.arange(NSA_KV_GROUPS)[None, :, None]
    dd = jnp.arange(HEAD_DIM)[None, None, :]
    return (HEAD_DIM * (NSA_HPG * g + r) + dd).reshape(-1)


def _overlap(nrow, ns):
    ci = jnp.arange(nrow)[:, None] * CMP_STRIDE
    sj = jnp.arange(ns)[None, :] * SEL_BLOCK
    ov = jnp.clip(jnp.minimum(ci + CMP_LEN, sj + SEL_BLOCK) - jnp.maximum(ci, sj), 0, None)
    return ov.astype(F32) / CMP_LEN


def kernel(x, c, mod_w, mod_b, norm_g, ffn_w1, ffn_w2, mix0_in_w, lru_conv_w, lru_conv_b, lru_wa, lru_ba,
           lru_wx, lru_bx, lru_lambda, mix0_out_w, mix1_in_w, cmp_pos, cmp_w1, cmp_w2, mix1_out_w, final_norm_g):
    bsz, s, d = x.shape
    depth = mod_w.shape[0]
    assert d == D_MODEL and s % 512 == 0 and s >= WINDOW + 128
    tm = 512

    mod = _modulation(c, mod_w, mod_b)
    tabs = _rope_tables(s)
    w1a = ffn_w1[..., :D_FF].astype(BF16)
    w1b = ffn_w1[..., D_FF:].astype(BF16)
    w2 = ffn_w2.astype(BF16)

    for l in range(depth):
        j = l // 2
        x = _ffn(x, mod[l], norm_g[l, 0], w1a[l, 0], w1b[l, 0], w2[l, 0], final_norm_g,
                 row0=0, final=False, tm=tm)
        if l % 2 == 0:
            xg, q, k, v, cent = _inproj0(x, mod[l], norm_g[l, 1], mix0_in_w[j].astype(BF16), tabs)
            y_lru = _lru(xg, lru_conv_w[j], lru_conv_b[j],
                         _block_diag(lru_wa[j]).astype(BF16), lru_ba[j].reshape(-1),
                         _block_diag(lru_wx[j]).astype(BF16), lru_bx[j].reshape(-1), lru_lambda[j], tm=256)
            y_att = _moba(q, k, v, cent.reshape(bsz, -1, MOBA_WIDTH))
            wo = mix0_out_w[j].astype(BF16)
            mix = ((y_lru, wo[:LRU_WIDTH]), (y_att, wo[LRU_WIDTH:]))
        else:
            perm = _nsa_head_perm()
            w_in = mix1_in_w[j]
            w_in = jnp.concatenate([w_in[:, :NSA_QD][:, perm], w_in[:, NSA_QD:]], axis=1)
            w_in = jnp.pad(w_in, ((0, 0), (0, IN1_PAD - w_in.shape[1]))).astype(BF16)
            qn, qr, kvc, kv4, gates = _inproj1(x, mod[l], norm_g[l, 1], w_in, tabs, tm=tm)
            cmp = _compress(kvc, cmp_pos[j], cmp_w1[j], cmp_w2[j])
            y = _nsa(qn, qr, kv4, cmp, gates, tk=512)
            mix = ((y, mix1_out_w[j][perm].astype(BF16)),)
        x = _ffn(x, mod[l], norm_g[l, 2], w1a[l, 1], w1b[l, 1], w2[l, 1], final_norm_g,
                 row0=6, final=(l == depth - 1), tm=tm, mix=mix)
    return x
```

```python
import functools
import math

import jax
import jax.numpy as jnp
from jax import lax
from jax.experimental import pallas as pl
from jax.experimental.pallas import tpu as pltpu

F32 = jnp.float32
BF16 = jnp.bfloat16

D_MODEL = 1024
D_FF = 2816
EPS = 1e-6
ROPE_THETA = 500000.0
HEAD_DIM = 64
ROT_DIM = HEAD_DIM // 4
NEG = -1e30
BIG = 1e30
LANES = 128

LRU_WIDTH = D_MODEL // 2
LRU_BLOCKS = 8
CONV_WIDTH = 4
LRU_C = 8.0
MOBA_WIDTH = D_MODEL // 2
MOBA_BLOCK = 256
MOBA_TOPK = 3
MOBA_TQ = 2 * MOBA_BLOCK
MOBA_CHUNKS = 4

NSA_HEADS = D_MODEL // HEAD_DIM
NSA_KV_GROUPS = 2
NSA_HPG = NSA_HEADS // NSA_KV_GROUPS
CMP_LEN = 32
CMP_STRIDE = 16
SEL_BLOCK = 64
SEL_TOPK = 16
WINDOW = 512
NSA_QD = NSA_HEADS * HEAD_DIM
NSA_KVD = NSA_KV_GROUPS * HEAD_DIM
IN1_PAD = NSA_QD + 6 * NSA_KVD + LANES
NSA_TQ = 128

VMEM_LIMIT = 52 * 1024 * 1024

NT_DIMS = (((1,), (1,)), ((), ()))
Q_PRESCALE = HEAD_DIM ** -0.5 * math.log2(math.e)


def _params(sem):
    return pltpu.CompilerParams(dimension_semantics=sem, vmem_limit_bytes=VMEM_LIMIT)


def _resident(shape):
    nd = len(shape)
    return pl.BlockSpec(shape, lambda *_: (0,) * nd, pipeline_mode=pl.Buffered(1))


def _silu(a):
    return a * jax.nn.sigmoid(a)


def _gelu_tanh(x):
    cdf = 0.5 * (1.0 + jnp.tanh(math.sqrt(2.0 / math.pi) * (x + 0.044715 * (x * x * x))))
    return x * cdf


def _rmsnorm(x, g):
    return x * lax.rsqrt(jnp.mean(x * x, axis=-1, keepdims=True) + EPS) * g


def _norm_mod(x, g, shift, scale):
    return _rmsnorm(x, g) * (1.0 + scale) + shift


def _rope(u, cos, sa, sb):
    outs = []
    for j in range(u.shape[1] // LANES):
        c = u[:, j * LANES:(j + 1) * LANES]
        up = pltpu.roll(c, LANES - ROT_DIM // 2, axis=1)
        dn = pltpu.roll(c, ROT_DIM // 2, axis=1)
        outs.append(c * cos + up * sa + dn * sb)
    return jnp.concatenate(outs, axis=1)


def _bf16_pieces(x):
    hi = x.astype(BF16)
    r1 = x - hi.astype(F32)
    mid = r1.astype(BF16)
    lo = (r1 - mid.astype(F32)).astype(BF16)
    return jnp.concatenate([hi, mid, lo], axis=0)


SOFTMAX_CHUNK = 64


def _stream_softmax(s_ref, p_ref, m_old, bias, tq):
    rows, n = s_ref.shape
    n_chunks, n_l = rows // SOFTMAX_CHUNK, n // LANES

    def chunk(c):
        sl = slice(c * SOFTMAX_CHUNK, (c + 1) * SOFTMAX_CHUNK)
        sc = s_ref[sl, :]
        if bias is not None:
            q0 = (c * SOFTMAX_CHUNK) % tq
            sc = sc + bias[q0:q0 + SOFTMAX_CHUNK]
        return sl, [sc[:, j * LANES:(j + 1) * LANES] for j in range(n_l)]

    partial_max = jnp.concatenate([functools.reduce(jnp.maximum, chunk(c)[1]) for c in range(n_chunks)], axis=0)
    m_new = jnp.maximum(m_old, jnp.broadcast_to(jnp.max(partial_max, axis=1, keepdims=True), (rows, LANES)))
    sums = []
    for c in range(n_chunks):
        sl, cols = chunk(c)
        m_c = m_new[sl]
        ps = [jnp.exp2(col - m_c) for col in cols]
        for j, pj in enumerate(ps):
            p_ref[sl, j * LANES:(j + 1) * LANES] = pj.astype(BF16)
        sums.append(functools.reduce(jnp.add, ps))
    return m_new, jnp.concatenate(sums, axis=0)


def _pick_topk_rows(g, k):
    n_ids = g.shape[0]
    ids = lax.broadcasted_iota(jnp.int32, g.shape, 0).astype(F32)
    picked = jnp.zeros(g.shape, F32)
    for _ in range(k):
        mx = jnp.max(g, axis=0, keepdims=True)
        first = jnp.min(jnp.where(g == mx, ids, float(n_ids)), axis=0, keepdims=True)
        hit = ids == first
        picked = jnp.where(hit, 1.0, picked)
        g = jnp.where(hit, -jnp.inf, g)
    return picked


def _mod_kernel(c_ref, w_ref, b_ref, o_ref):
    cond = _silu(c_ref[...])
    o_ref[0] = jnp.dot(cond.astype(BF16), w_ref[0].astype(BF16), preferred_element_type=F32) + b_ref[0]


def _modulation(c, mod_w, mod_b):
    depth, d, n = mod_w.shape
    bsz = c.shape[0]
    bp = -(-bsz // 8) * 8
    cp = jnp.pad(c, ((0, bp - bsz), (0, 0)))
    tn = 1152
    out = pl.pallas_call(
        _mod_kernel,
        out_shape=jax.ShapeDtypeStruct((depth, bp, n), F32),
        grid=(depth, n // tn),
        in_specs=[
            pl.BlockSpec((bp, d), lambda l, j: (0, 0)),
            pl.BlockSpec((1, d, tn), lambda l, j: (l, 0, j)),
            pl.BlockSpec((1, 1, tn), lambda l, j: (l, 0, j)),
        ],
        out_specs=pl.BlockSpec((1, bp, tn), lambda l, j: (l, 0, j)),
        compiler_params=_params(("parallel", "parallel")),
        name="adaln_mod",
    )(cp, mod_w, mod_b.reshape(depth, 1, n))
    return out[:, :bsz].reshape(depth, bsz, 9, d)


def _ffn_kernel(x_ref, mod_ref, g_ref, w1a_ref, w1b_ref, w2_ref, gf_ref, *rest, row0, final, n_mix):
    o_ref = rest[-1]
    x = x_ref[0]
    m = mod_ref[0]
    if n_mix:
        mixed = sum(jnp.dot(y_ref[0], w_ref[...], preferred_element_type=F32)
                    for y_ref, w_ref in zip(rest[:n_mix], rest[n_mix:2 * n_mix]))
        x = x + m[5:6] * mixed
    h = _norm_mod(x, g_ref[...], m[row0:row0 + 1], m[row0 + 1:row0 + 2]).astype(BF16)
    a = jnp.dot(h, w1a_ref[...], preferred_element_type=F32)
    b = jnp.dot(h, w1b_ref[...], preferred_element_type=F32)
    act = (_silu(a) * b).astype(BF16)
    y = jnp.dot(act, w2_ref[...], preferred_element_type=F32)
    out = x + (0.5 * m[row0 + 2:row0 + 3]) * y
    if final:
        out = _rmsnorm(out, gf_ref[...])
    o_ref[0] = out


def _ffn(x, mod_l, g, w1a, w1b, w2, gf, *, row0, final, tm, mix=()):
    bsz, s, d = x.shape
    kern = functools.partial(_ffn_kernel, row0=row0, final=final, n_mix=len(mix))
    row = lambda b, i: (b, i, 0)
    return pl.pallas_call(
        kern,
        out_shape=jax.ShapeDtypeStruct(x.shape, F32),
        grid=(bsz, s // tm),
        in_specs=[
            pl.BlockSpec((1, tm, d), row),
            pl.BlockSpec((1, 9, d), lambda b, i: (b, 0, 0)),
            _resident((1, d)),
            _resident(w1a.shape),
            _resident(w1b.shape),
            _resident(w2.shape),
            _resident((1, d)),
        ] + [pl.BlockSpec((1, tm, y.shape[2]), row) for y, _ in mix] + [_resident(w.shape) for _, w in mix],
        out_specs=pl.BlockSpec((1, tm, d), row),
        compiler_params=_params(("parallel", "parallel")),
        name="ffn_final" if final else ("mix_ffn" if mix else "ffn"),
    )(x, mod_l, g.reshape(1, d), w1a, w1b, w2, gf.reshape(1, d), *[y for y, _ in mix], *[w for _, w in mix])


def _inproj0_kernel(x_ref, mod_ref, g_ref, w_ref, cos_ref, sa_ref, sb_ref,
                    xg_ref, q_ref, k_ref, v_ref, cent_ref):
    m = mod_ref[0]
    h = _norm_mod(x_ref[0], g_ref[...], m[3:4], m[4:5]).astype(BF16)
    u = jnp.dot(h, w_ref[...], preferred_element_type=F32)
    o1 = 2 * LRU_WIDTH
    xg_ref[0] = u[:, :o1]
    cos, sa, sb = cos_ref[...], sa_ref[...], sb_ref[...]
    q = _rope(u[:, o1:o1 + MOBA_WIDTH], cos, sa, sb)
    k = _rope(u[:, o1 + MOBA_WIDTH:o1 + 2 * MOBA_WIDTH], cos, sa, sb)
    q_ref[0] = (q * Q_PRESCALE).astype(BF16)
    k_ref[0] = k.astype(BF16)
    v_ref[0] = u[:, o1 + 2 * MOBA_WIDTH:].astype(BF16)
    cent_ref[0, 0] = jnp.mean(k, axis=0, keepdims=True)


def _inproj0(x, mod_l, g, w, rope_tabs):
    bsz, s, d = x.shape
    tm = MOBA_BLOCK
    nb = s // tm
    n = w.shape[1]
    row = lambda b, i: (b, i, 0)
    tab = pl.BlockSpec((tm, LANES), lambda b, i: (i, 0))
    return pl.pallas_call(
        _inproj0_kernel,
        out_shape=(
            jax.ShapeDtypeStruct((bsz, s, 2 * LRU_WIDTH), F32),
            jax.ShapeDtypeStruct((bsz, s, MOBA_WIDTH), BF16),
            jax.ShapeDtypeStruct((bsz, s, MOBA_WIDTH), BF16),
            jax.ShapeDtypeStruct((bsz, s, MOBA_WIDTH), BF16),
            jax.ShapeDtypeStruct((bsz, nb, 1, MOBA_WIDTH), F32),
        ),
        grid=(bsz, nb),
        in_specs=[
            pl.BlockSpec((1, tm, d), row),
            pl.BlockSpec((1, 9, d), lambda b, i: (b, 0, 0)),
            _resident((1, d)),
            _resident((d, n)),
            tab, tab, tab,
        ],
        out_specs=(
            pl.BlockSpec((1, tm, 2 * LRU_WIDTH), row),
            pl.BlockSpec((1, tm, MOBA_WIDTH), row),
            pl.BlockSpec((1, tm, MOBA_WIDTH), row),
            pl.BlockSpec((1, tm, MOBA_WIDTH), row),
            pl.BlockSpec((1, 1, 1, MOBA_WIDTH), lambda b, i: (b, i, 0, 0)),
        ),
        compiler_params=_params(("parallel", "parallel")),
        name="inproj0",
    )(x, mod_l, g.reshape(1, d), w, *rope_tabs)


def _lru_kernel(xg_ref, cw_ref, cb_ref, wa_ref, ba_ref, wx_ref, bx_ref, lam_ref, y_ref, xbuf, hprev, *, tm):
    c = LRU_WIDTH
    halo = 8

    @pl.when(pl.program_id(1) == 0)
    def _():
        xbuf[0:halo, :] = jnp.zeros((halo, c), F32)
        hprev[...] = jnp.zeros(hprev.shape, F32)

    x = xg_ref[0, :, 0:c]
    gate_in = xg_ref[0, :, c:2 * c]
    xbuf[halo:halo + tm, :] = x
    cw = cw_ref[...]
    xc = cb_ref[...] + cw[CONV_WIDTH - 1:CONV_WIDTH] * x
    for kk in range(CONV_WIDTH - 1):
        back = CONV_WIDTH - 1 - kk
        xc = xc + cw[kk:kk + 1] * xbuf[halo - back:halo - back + tm, :]
    xbuf[0:halo, :] = x[tm - halo:tm, :]

    xcb = xc.astype(BF16)
    r = jax.nn.sigmoid(jnp.dot(xcb, wa_ref[...], preferred_element_type=F32) + ba_ref[...])
    ig = jax.nn.sigmoid(jnp.dot(xcb, wx_ref[...], preferred_element_type=F32) + bx_ref[...])
    nlam = -lam_ref[...]
    softplus = jnp.maximum(nlam, 0.0) + jnp.log1p(jnp.exp(-jnp.abs(nlam)))
    log_a = (-LRU_C * r) * softplus
    a = jnp.exp(log_a)
    mult = jnp.sqrt(-jnp.tanh(log_a) * (a * a + 1.0))
    bt = mult * (ig * xc)

    row = lax.broadcasted_iota(jnp.int32, (tm, c), 0)
    step = 1
    while step < tm:
        a_sh = pltpu.roll(a, step, axis=0)
        b_sh = pltpu.roll(bt, step, axis=0)
        valid = row >= step
        bt = jnp.where(valid, a * b_sh + bt, bt)
        a = jnp.where(valid, a * a_sh, a)
        step *= 2
    h = bt + a * hprev[0:1, :]
    hprev[0:1, :] = h[tm - 1:tm, :]
    y_ref[0] = (h * _gelu_tanh(gate_in)).astype(BF16)


def _lru(xg, cw, cb, wa_bd, ba, wx_bd, bx, lam, *, tm):
    bsz, s, _ = xg.shape
    c = LRU_WIDTH
    vec = _resident((1, c))
    return pl.pallas_call(
        functools.partial(_lru_kernel, tm=tm),
        out_shape=jax.ShapeDtypeStruct((bsz, s, c), BF16),
        grid=(bsz, s // tm),
        in_specs=[
            pl.BlockSpec((1, tm, 2 * c), lambda b, i: (b, i, 0)),
            _resident((CONV_WIDTH, c)), vec,
            _resident((c, c)), vec,
            _resident((c, c)), vec,
            vec,
        ],
        out_specs=pl.BlockSpec((1, tm, c), lambda b, i: (b, i, 0)),
        scratch_shapes=[pltpu.VMEM((tm + 8, c), F32), pltpu.VMEM((8, c), F32)],
        compiler_params=_params(("parallel", "arbitrary")),
        name="rg_lru",
    )(xg, cw, cb.reshape(1, c), wa_bd, ba.reshape(1, c), wx_bd, bx.reshape(1, c), lam.reshape(1, c))


def _moba_kernel(q_ref, k_ref, v_ref, cent_ref, hot_ref, o_ref, lhs_ref, s_ref, p_ref, m_ref, l_ref, acc_ref):
    tq = MOBA_TQ
    rows = 2 * tq
    nb_pad = -(-(k_ref.shape[1] // MOBA_BLOCK) // 8) * 8
    i = pl.program_id(2)
    lane = lax.broadcasted_iota(jnp.int32, (tq, LANES), 1)
    bid = lax.broadcasted_iota(jnp.int32, (LANES, rows), 0)
    qidx = jnp.concatenate([lax.broadcasted_iota(jnp.int32, (LANES, tq), 1)] * 2, axis=1)
    own = (tq // MOBA_BLOCK) * i + qidx // MOBA_BLOCK
    past = bid < own
    qidx_r = jnp.concatenate([lax.broadcasted_iota(jnp.int32, (nb_pad, tq), 1)] * 2, axis=1)
    past_real = lax.broadcasted_iota(jnp.int32, (nb_pad, rows), 0) < (tq // MOBA_BLOCK) * i + qidx_r // MOBA_BLOCK
    chunks = [slice(c * LANES, (c + 1) * LANES) for c in range(MOBA_CHUNKS)]

    for c, cl in enumerate(chunks):
        q = q_ref[0, :, cl]
        zero = jnp.zeros_like(q)
        q2 = jnp.concatenate([jnp.where(lane < HEAD_DIM, q, zero), jnp.where(lane >= HEAD_DIM, q, zero)], axis=0)
        gate3 = lax.dot_general(_bf16_pieces(cent_ref[0, :, cl]), q2, NT_DIMS, preferred_element_type=F32)
        gate_t = gate3[:LANES] + gate3[LANES:2 * LANES] + gate3[2 * LANES:]
        sel_t = _pick_topk_rows(jnp.where(past_real, gate_t[:nb_pad], -jnp.inf), MOBA_TOPK)
        sel_t = jnp.concatenate([sel_t, jnp.zeros((LANES - nb_pad, rows), F32)], axis=0)
        drop = jnp.where((past & (sel_t > 0.5)) | (bid == own), 0.0, -BIG).T.astype(BF16)
        lhs_ref[c] = jnp.concatenate([q2, drop], axis=1)
        m_ref[c] = jnp.full((rows, LANES), NEG, F32)
        l_ref[c] = jnp.zeros((rows, LANES), F32)
        acc_ref[c] = jnp.zeros((rows, LANES), F32)

    def tile(kt, bias):
        k0 = pl.multiple_of(kt * tq, tq)
        for c, cl in enumerate(chunks):
            rhs = jnp.concatenate([k_ref[0, pl.ds(k0, tq), cl], hot_ref[pl.ds(k0, tq), :]], axis=1)
            s_ref[c] = lax.dot_general(lhs_ref[c], rhs, NT_DIMS, preferred_element_type=F32)
        for c, cl in enumerate(chunks):
            m_old = m_ref[c]
            m_new, p_sum = _stream_softmax(s_ref.at[c], p_ref.at[c], m_old, bias, tq)
            alpha = jnp.exp2(m_old - m_new)
            m_ref[c] = m_new
            l_ref[c] = alpha * l_ref[c] + p_sum
            acc_ref[c] = alpha * acc_ref[c] + jnp.dot(p_ref[c], v_ref[0, pl.ds(k0, tq), cl],
                                                      preferred_element_type=F32)

    @pl.when(i >= 0)
    def _():
        qpos = lax.broadcasted_iota(jnp.int32, (tq, tq), 0)
        kpos = lax.broadcasted_iota(jnp.int32, (tq, tq), 1)
        tile(i, jnp.where(kpos <= qpos, 0.0, NEG))

    @pl.loop(0, i)
    def _(kt):
        tile(kt, None)

    for c, cl in enumerate(chunks):
        out = acc_ref[c] / jnp.sum(l_ref[c], axis=1, keepdims=True)
        o_ref[0, :, cl] = jnp.where(lane < HEAD_DIM, out[:tq], out[tq:]).astype(BF16)


def _moba(q, k, v, cent):
    bsz, s, w = q.shape
    tq = MOBA_TQ
    nb = s // MOBA_BLOCK
    assert nb <= LANES and s % tq == 0
    cent = jnp.pad(cent, ((0, 0), (0, LANES - nb), (0, 0)))
    hot = _block_one_hot(s, MOBA_BLOCK)
    nc, cw = MOBA_CHUNKS, MOBA_CHUNKS * LANES
    return pl.pallas_call(
        _moba_kernel,
        out_shape=jax.ShapeDtypeStruct((bsz, s, w), BF16),
        grid=(bsz, w // cw, s // tq),
        in_specs=[
            pl.BlockSpec((1, tq, cw), lambda b, hp, i: (b, i, hp)),
            pl.BlockSpec((1, s, cw), lambda b, hp, i: (b, 0, hp), pipeline_mode=pl.Buffered(1)),
            pl.BlockSpec((1, s, cw), lambda b, hp, i: (b, 0, hp), pipeline_mode=pl.Buffered(1)),
            pl.BlockSpec((1, LANES, cw), lambda b, hp, i: (b, 0, hp)),
            _resident(hot.shape),
        ],
        out_specs=pl.BlockSpec((1, tq, cw), lambda b, hp, i: (b, i, hp)),
        scratch_shapes=[pltpu.VMEM((nc, 2 * tq, 2 * LANES), BF16), pltpu.VMEM((nc, 2 * tq, tq), F32),
                        pltpu.VMEM((nc, 2 * tq, tq), BF16)] + [pltpu.VMEM((nc, 2 * tq, LANES), F32)] * 3,
        compiler_params=_params(("parallel", "parallel", "parallel")),
        name="moba_attn",
    )(q, k, v, cent, hot)


def _inproj1_kernel(x_ref, mod_ref, g_ref, w_ref, cos_ref, sa_ref, sb_ref,
                    qn_ref, qr_ref, kvc_ref, kv4_ref, gate_ref):
    m = mod_ref[0]
    h = _norm_mod(x_ref[0], g_ref[...], m[3:4], m[4:5]).astype(BF16)
    u = jnp.dot(h, w_ref[...], preferred_element_type=F32)
    cos, sa, sb = cos_ref[...], sa_ref[...], sb_ref[...]
    q = u[:, :NSA_QD] * Q_PRESCALE
    qn_ref[0] = q.astype(BF16)
    qr_ref[0] = _rope(q, cos, sa, sb).astype(BF16)
    o = NSA_QD
    kvc_ref[0, 0] = u[:, o:o + NSA_KVD]
    kvc_ref[0, 1] = u[:, o + NSA_KVD:o + 2 * NSA_KVD]
    o += 2 * NSA_KVD
    ks = _rope(u[:, o:o + NSA_KVD], cos, sa, sb)
    vs = u[:, o + NSA_KVD:o + 2 * NSA_KVD]
    kw = _rope(u[:, o + 2 * NSA_KVD:o + 3 * NSA_KVD], cos, sa, sb)
    vw = u[:, o + 3 * NSA_KVD:o + 4 * NSA_KVD]
    kv4_ref[0] = jnp.concatenate([ks, vs, kw, vw], axis=1).astype(BF16)
    gate_ref[0] = jax.nn.sigmoid(u[:, o + 4 * NSA_KVD:])


def _inproj1(x, mod_l, g, w, rope_tabs, *, tm):
    bsz, s, d = x.shape
    n = w.shape[1]
    row = lambda b, i: (b, i, 0)
    tab = pl.BlockSpec((tm, LANES), lambda b, i: (i, 0))
    widths = (NSA_QD, NSA_QD, None, 4 * NSA_KVD, LANES)
    dtypes = (BF16, BF16, F32, BF16, F32)
    shape = lambda wd: (bsz, s, wd) if wd else (bsz, 2, s, NSA_KVD)
    spec = lambda wd: pl.BlockSpec((1, tm, wd), row) if wd else pl.BlockSpec((1, 2, tm, NSA_KVD),
                                                                             lambda b, i: (b, 0, i, 0))
    return pl.pallas_call(
        _inproj1_kernel,
        out_shape=tuple(jax.ShapeDtypeStruct(shape(wd), dt) for wd, dt in zip(widths, dtypes)),
        grid=(bsz, s // tm),
        in_specs=[
            pl.BlockSpec((1, tm, d), row),
            pl.BlockSpec((1, 9, d), lambda b, i: (b, 0, 0)),
            _resident((1, d)),
            _resident((d, n)),
            tab, tab, tab,
        ],
        out_specs=tuple(spec(wd) for wd in widths),
        compiler_params=_params(("parallel", "parallel")),
        name="inproj1",
    )(x, mod_l, g.reshape(1, d), w, *rope_tabs)


def _compress_kernel(kvc_ref, pos_ref, w1_ref, w2_ref, o_ref):
    nrow = o_ref.shape[2]
    for kv in range(2):
        top = jnp.zeros((nrow, w1_ref.shape[3]), F32)
        bot = jnp.zeros((nrow, w1_ref.shape[3]), F32)
        for r in range(CMP_STRIDE):
            x = kvc_ref[0, kv, pl.ds(r, nrow, stride=CMP_STRIDE), :]
            top += jnp.dot((x + pos_ref[kv, r:r + 1]).astype(BF16), w1_ref[kv, r], preferred_element_type=F32)
            bot += jnp.dot((x + pos_ref[kv, CMP_STRIDE + r:CMP_STRIDE + r + 1]).astype(BF16),
                           w1_ref[kv, CMP_STRIDE + r], preferred_element_type=F32)
        pre = top + pltpu.roll(bot, nrow - 1, axis=0)
        o_ref[0, kv] = jnp.dot(_gelu_tanh(pre).astype(BF16), w2_ref[kv], preferred_element_type=F32).astype(BF16)


def _compress(kvc, pos, w1, w2):
    bsz, _, s, _ = kvc.shape
    nrow = s // CMP_STRIDE
    eye = jnp.eye(NSA_KV_GROUPS, dtype=F32)
    pos2 = jnp.tile(pos, (1, 1, NSA_KV_GROUPS))
    hidden = w1.shape[-1]
    w1r = w1.reshape(2, CMP_LEN, HEAD_DIM, hidden)
    w1bd = jnp.einsum('gh,kric->krgihc', eye, w1r).reshape(2, CMP_LEN, NSA_KVD, NSA_KV_GROUPS * hidden)
    w2bd = jnp.einsum('gh,kcd->kgchd', eye, w2).reshape(2, NSA_KV_GROUPS * hidden, NSA_KVD)
    return pl.pallas_call(
        _compress_kernel,
        out_shape=jax.ShapeDtypeStruct((bsz, 2, nrow, NSA_KVD), BF16),
        grid=(bsz,),
        in_specs=[
            pl.BlockSpec((1, 2, s, NSA_KVD), lambda b: (b, 0, 0, 0)),
            _resident(pos2.shape),
            _resident(w1bd.shape),
            _resident(w2bd.shape),
        ],
        out_specs=pl.BlockSpec((1, 2, nrow, NSA_KVD), lambda b: (b, 0, 0, 0)),
        compiler_params=_params(("parallel",)),
        name="nsa_compress",
    )(kvc, pos2, w1bd.astype(BF16), w2bd.astype(BF16))


def _nsa_kernel(qn_ref, qr_ref, kv4_ref, cmp_ref, gate_ref, ovl_ref, hot_ref, gexp_ref, o_ref,
                lhs_ref, s_ref, p_ref, m_ref, acc_ref, *, seq, tk):
    tq = NSA_TQ
    nrow = cmp_ref.shape[2]
    ns = seq // SEL_BLOCK
    n_g, n_r = NSA_KV_GROUPS, NSA_HPG
    n_h = n_g * n_r
    rows = n_h * tq
    t0 = pl.program_id(1) * tq
    lane = lax.broadcasted_iota(jnp.int32, (tq, LANES), 1)
    gates = gate_ref[0]

    def stack(q_ref):
        parts = []
        for g in range(n_g):
            in_grp = (lane >= g * HEAD_DIM) & (lane < (g + 1) * HEAD_DIM)
            for r in range(n_r):
                ch = q_ref[0, :, r * LANES:(r + 1) * LANES]
                parts.append(jnp.where(in_grp, ch, jnp.zeros_like(ch)))
        return jnp.concatenate(parts, axis=0)

    half = rows // n_g
    neg_stats = jnp.full((rows, LANES), NEG, F32)

    def times_values(p_view, v, g, extra=None):
        vl = lax.broadcasted_iota(jnp.int32, v.shape, 1)
        vg = jnp.where((vl >= g * HEAD_DIM) & (vl < (g + 1) * HEAD_DIM), v, jnp.ones_like(v))
        if extra is not None:
            vg = jnp.concatenate([vg, extra], axis=1)
        return jnp.dot(p_view[g * half:(g + 1) * half, :], vg, preferred_element_type=F32)

    sc_ref, pc_ref = s_ref.at[:, 0:nrow], p_ref.at[:, 0:nrow]
    tq_c = t0 + lax.broadcasted_iota(jnp.int32, (tq, nrow), 0)
    cend = lax.broadcasted_iota(jnp.int32, (tq, nrow), 1) * CMP_STRIDE + (CMP_LEN - 1)
    sc_ref[...] = lax.dot_general(stack(qn_ref), cmp_ref[0, 0], NT_DIMS, preferred_element_type=F32)
    _stream_softmax(sc_ref, pc_ref, neg_stats, jnp.where(cend <= tq_c, 0.0, NEG), tq)
    both = jnp.concatenate([times_values(pc_ref, cmp_ref[0, 1], g, extra=ovl_ref[...]) for g in range(n_g)], axis=0)
    acc_c, imp_h = both[:, 0:LANES], both[:, LANES:]
    lane_r = lax.broadcasted_iota(jnp.int32, (rows, LANES), 1)
    grp_r = lax.broadcasted_iota(jnp.int32, (rows, LANES), 0) // half
    den_c = jnp.where(lane_r // HEAD_DIM == grp_r, pltpu.roll(acc_c, HEAD_DIM, axis=1), acc_c)
    has_block = jnp.where(t0 + lax.broadcasted_iota(jnp.int32, (tq, LANES), 0) >= CMP_LEN - 1, 1.0, 0.0)
    inv_c = jnp.concatenate([has_block] * n_h, axis=0) / den_c
    o_c = acc_c * inv_c

    imp_h = imp_h * inv_c
    imp_t = jnp.concatenate([functools.reduce(jnp.add, [imp_h[(g * n_r + r) * tq:(g * n_r + r + 1) * tq]
                                                        for r in range(n_r)]) for g in range(n_g)], axis=0).T
    jn = lax.broadcasted_iota(jnp.int32, (LANES, n_g * tq), 0)
    tpos = t0 + lax.broadcasted_iota(jnp.int32, (LANES, tq), 1)
    blk_t = jnp.concatenate([tpos] * n_g, axis=1) // SEL_BLOCK
    valid = jn <= blk_t
    forced = (jn == 0) | (jn == blk_t) | (jn == blk_t - 1)
    others = _pick_topk_rows(jnp.where(valid & jnp.logical_not(forced), imp_t, -jnp.inf), min(SEL_TOPK, ns) - 3)
    keep = valid & (forced | (others > 0.5))
    drop = jnp.where(keep, 0.0, -BIG).T.astype(BF16)
    lhs_ref[...] = jnp.concatenate(
        [stack(qr_ref), jnp.concatenate([drop[g * tq:(g + 1) * tq] for g in range(n_g) for _ in range(n_r)], axis=0)],
        axis=1)

    wk = WINDOW + tq
    sw_ref, pw_ref = s_ref.at[:, 0:wk], p_ref.at[:, 0:wk]
    kstart = pl.multiple_of(jnp.clip(t0 - WINDOW, 0, seq - wk), tq)
    diff = (t0 + lax.broadcasted_iota(jnp.int32, (tq, wk), 0)) - (
        kstart + lax.broadcasted_iota(jnp.int32, (tq, wk), 1))
    sw_ref[...] = lax.dot_general(lhs_ref[:, 0:LANES], kv4_ref[0, pl.ds(kstart, wk), 2 * LANES:3 * LANES], NT_DIMS,
                                  preferred_element_type=F32)
    _stream_softmax(sw_ref, pw_ref, neg_stats, jnp.where((diff >= 0) & (diff < WINDOW), 0.0, NEG), tq)
    acc_w = jnp.concatenate([times_values(pw_ref, kv4_ref[0, pl.ds(kstart, wk), 3 * LANES:4 * LANES], g)
                             for g in range(n_g)], axis=0)

    m_ref[...] = neg_stats
    acc_ref[...] = jnp.zeros(acc_ref.shape, F32)

    def sel_tile(k0, width, bias):
        rhs = jnp.concatenate([kv4_ref[0, pl.ds(k0, width), 0:LANES], hot_ref[pl.ds(k0, width), :]], axis=1)
        views = []
        for g in range(n_g):
            gs = slice(g * half, (g + 1) * half)
            sg_ref = s_ref.at[gs, 0:width]
            sg_ref[...] = lax.dot_general(lhs_ref[gs, :], rhs, NT_DIMS, preferred_element_type=F32)
            views.append((gs, sg_ref, p_ref.at[gs, 0:width]))
        for g, (gs, sg_ref, pg_ref) in enumerate(views):
            m_old = m_ref[gs, :]
            m_new, _ = _stream_softmax(sg_ref, pg_ref, m_old, bias, tq)
            m_ref[gs, :] = m_new
            vl = lax.broadcasted_iota(jnp.int32, (width, LANES), 1)
            v = kv4_ref[0, pl.ds(k0, width), LANES:2 * LANES]
            vg = jnp.where((vl >= g * HEAD_DIM) & (vl < (g + 1) * HEAD_DIM), v, jnp.ones_like(v))
            acc_ref[gs, :] = jnp.exp2(m_old - m_new) * acc_ref[gs, :] + jnp.dot(pg_ref[...], vg,
                                                                                preferred_element_type=F32)

    last = t0 // tk

    @pl.loop(0, last)
    def _(kt):
        sel_tile(pl.multiple_of(kt * tk, tk), tk, None)

    k_last = pl.multiple_of(last * tk, tk)

    def last_tile(width):
        kpos = k_last + lax.broadcasted_iota(jnp.int32, (tq, width), 1)
        tq_s = t0 + lax.broadcasted_iota(jnp.int32, (tq, width), 0)
        sel_tile(k_last, width, jnp.where(kpos <= tq_s, 0.0, NEG))

    in_first_half = t0 - k_last + tq <= tk // 2
    pl.when(in_first_half)(lambda: last_tile(tk // 2))
    pl.when(jnp.logical_not(in_first_half))(lambda: last_tile(tk))

    acc_s = acc_ref[...]

    branches = (o_c, acc_s / pltpu.roll(acc_s, HEAD_DIM, axis=1), acc_w / pltpu.roll(acc_w, HEAD_DIM, axis=1))
    gates_b = gates.astype(BF16)
    spread = [jnp.dot(gates_b, gexp_ref[c], preferred_element_type=F32) for c in range(3)]
    for r in range(n_r):
        y = None
        for c in range(3):
            heads = [branches[c][(g * n_r + r) * tq:(g * n_r + r + 1) * tq] for g in range(n_g)]
            term = spread[c][:, r * LANES:(r + 1) * LANES] * jnp.where(lane < HEAD_DIM, heads[0], heads[1])
            y = term if y is None else y + term
        o_ref[0, :, r * LANES:(r + 1) * LANES] = y.astype(BF16)


def _nsa(qn, qr, kv4, cmp, gates, *, tk):
    bsz, s, _ = qn.shape
    tq = NSA_TQ
    ns = s // SEL_BLOCK
    assert ns <= LANES
    assert min(SEL_TOPK, ns) >= 3
    ovl = jnp.pad(_overlap(cmp.shape[2], ns), ((0, 0), (0, LANES - ns))).astype(BF16)
    hot = _block_one_hot(s, SEL_BLOCK)
    head_of_p = _nsa_head_perm() // HEAD_DIM
    gexp = (jnp.arange(LANES)[None, :, None] == 3 * head_of_p[None, None, :] + jnp.arange(3)[:, None, None]).astype(BF16)
    rows = NSA_HEADS * tq
    width = max(cmp.shape[2], WINDOW + tq, tk)
    row = lambda b, i: (b, i, 0)
    return pl.pallas_call(
        functools.partial(_nsa_kernel, seq=s, tk=tk),
        out_shape=jax.ShapeDtypeStruct((bsz, s, NSA_QD), BF16),
        grid=(bsz, s // tq),
        in_specs=[
            pl.BlockSpec((1, tq, NSA_QD), row),
            pl.BlockSpec((1, tq, NSA_QD), row),
            pl.BlockSpec((1, s, 4 * NSA_KVD), lambda b, i: (b, 0, 0), pipeline_mode=pl.Buffered(1)),
            pl.BlockSpec((1, 2) + cmp.shape[2:], lambda b, i: (b, 0, 0, 0), pipeline_mode=pl.Buffered(1)),
            pl.BlockSpec((1, tq, LANES), row),
            _resident(ovl.shape),
            _resident(hot.shape),
            _resident(gexp.shape),
        ],
        out_specs=pl.BlockSpec((1, tq, NSA_QD), row),
        scratch_shapes=[pltpu.VMEM((rows, 2 * LANES), BF16), pltpu.VMEM((rows, width), F32),
                        pltpu.VMEM((rows, width), BF16),
                        pltpu.VMEM((rows, LANES), F32), pltpu.VMEM((rows, LANES), F32)],
        compiler_params=_params(("parallel", "parallel")),
        name="nsa_attn",
    )(qn, qr, kv4, cmp, gates, ovl, hot, gexp)


def _rope_tables(s):
    half = ROT_DIM // 2
    pos = jnp.arange(s, dtype=F32)
    freqs = ROPE_THETA ** (-jnp.arange(half, dtype=F32) * 2.0 / ROT_DIM)
    ang = pos[:, None] * freqs[None, :]
    cos, sin = jnp.cos(ang), jnp.sin(ang)
    ones = jnp.ones((s, HEAD_DIM - ROT_DIM), F32)
    zeros = jnp.zeros((s, HEAD_DIM - ROT_DIM), F32)
    zh = jnp.zeros((s, half), F32)
    cos_t = jnp.concatenate([cos, cos, ones], axis=1)
    sa_t = jnp.concatenate([-sin, zh, zeros], axis=1)
    sb_t = jnp.concatenate([zh, sin, zeros], axis=1)
    return tuple(jnp.tile(t, (1, LANES // HEAD_DIM)) for t in (cos_t, sa_t, sb_t))


def _block_one_hot(s, block):
    return (jnp.arange(s)[:, None] // block == jnp.arange(LANES)[None, :]).astype(BF16)


def _block_diag(w):
    n, a, b = w.shape
    eye = jnp.eye(n, dtype=w.dtype)
    return (eye[:, None, :, None] * w[:, :, None, :]).reshape(n * a, n * b)


def _nsa_head_perm():
    r = jnp.arange(NSA_HPG)[:, None, None]
    g = jnp.arange(NSA_KV_GROUPS)[None, :, None]
    dd = jnp.arange(HEAD_DIM)[None, None, :]
    return (HEAD_DIM * (NSA_HPG * g + r) + dd).reshape(-1)


def _overlap(nrow, ns):
    ci = jnp.arange(nrow)[:, None] * CMP_STRIDE
    sj = jnp.arange(ns)[None, :] * SEL_BLOCK
    ov = jnp.clip(jnp.minimum(ci + CMP_LEN, sj + SEL_BLOCK) - jnp.maximum(ci, sj), 0, None)
    return ov.astype(F32) / CMP_LEN


def kernel(x, c, mod_w, mod_b, norm_g, ffn_w1, ffn_w2, mix0_in_w, lru_conv_w, lru_conv_b, lru_wa, lru_ba,
           lru_wx, lru_bx, lru_lambda, mix0_out_w, mix1_in_w, cmp_pos, cmp_w1, cmp_w2, mix1_out_w, final_norm_g):
    bsz, s, d = x.shape
    depth = mod_w.shape[0]
    assert d == D_MODEL and s % 512 == 0 and s >= WINDOW + NSA_TQ
    tm = 512

    mod = _modulation(c, mod_w, mod_b)
    tabs = _rope_tables(s)
    w1a = ffn_w1[..., :D_FF].astype(BF16)
    w1b = ffn_w1[..., D_FF:].astype(BF16)
    w2 = ffn_w2.astype(BF16)

    for l in range(depth):
        j = l // 2
        x = _ffn(x, mod[l], norm_g[l, 0], w1a[l, 0], w1b[l, 0], w2[l, 0], final_norm_g,
                 row0=0, final=False, tm=tm)
        if l % 2 == 0:
            xg, q, k, v, cent = _inproj0(x, mod[l], norm_g[l, 1], mix0_in_w[j].astype(BF16), tabs)
            y_lru = _lru(xg, lru_conv_w[j], lru_conv_b[j],
                         _block_diag(lru_wa[j]).astype(BF16), lru_ba[j].reshape(-1),
                         _block_diag(lru_wx[j]).astype(BF16), lru_bx[j].reshape(-1), lru_lambda[j], tm=256)
            y_att = _moba(q, k, v, cent.reshape(bsz, -1, MOBA_WIDTH))
            wo = mix0_out_w[j].astype(BF16)
            mix = ((y_lru, wo[:LRU_WIDTH]), (y_att, wo[LRU_WIDTH:]))
        else:
            perm = _nsa_head_perm()
            w_in = mix1_in_w[j]
            w_in = jnp.concatenate([w_in[:, :NSA_QD][:, perm], w_in[:, NSA_QD:]], axis=1)
            w_in = jnp.pad(w_in, ((0, 0), (0, IN1_PAD - w_in.shape[1]))).astype(BF16)
            qn, qr, kvc, kv4, gates = _inproj1(x, mod[l], norm_g[l, 1], w_in, tabs, tm=tm)
            cmp = _compress(kvc, cmp_pos[j], cmp_w1[j], cmp_w2[j])
            y = _nsa(qn, qr, kv4, cmp, gates, tk=512)
            mix = ((y, mix1_out_w[j][perm].astype(BF16)),)
        x = _ffn(x, mod[l], norm_g[l, 2], w1a[l, 1], w1b[l, 1], w2[l, 1], final_norm_g,
                 row0=6, final=(l == depth - 1), tm=tm, mix=mix)
    return x
```

```python
import functools
import math

import jax
import jax.numpy as jnp
from jax import lax
from jax.experimental import pallas as pl
from jax.experimental.pallas import tpu as pltpu

F32 = jnp.float32
BF16 = jnp.bfloat16

D_MODEL = 1024
EPS = 1e-6
ROPE_THETA = 500000.0
HEAD_DIM = 64
ROT_DIM = HEAD_DIM // 4
NEG = -1e30
BIG = 1e30
LANES = 128

LRU_WIDTH = D_MODEL // 2
LRU_BLOCKS = 8
CONV_WIDTH = 4
LRU_C = 8.0
MOBA_WIDTH = D_MODEL // 2
MOBA_BLOCK = 256
MOBA_TOPK = 3
MOBA_TQ = 2 * MOBA_BLOCK
MOBA_CHUNKS = 4

NSA_HEADS = D_MODEL // HEAD_DIM
NSA_KV_GROUPS = 2
NSA_HPG = NSA_HEADS // NSA_KV_GROUPS
CMP_LEN = 32
CMP_STRIDE = 16
SEL_BLOCK = 64
SEL_TOPK = 16
WINDOW = 512
NSA_QD = NSA_HEADS * HEAD_DIM
NSA_KVD = NSA_KV_GROUPS * HEAD_DIM
IN1_PAD = NSA_QD + 6 * NSA_KVD + LANES
NSA_TQ = 128

VMEM_LIMIT = 52 * 1024 * 1024

NT_DIMS = (((1,), (1,)), ((), ()))
Q_PRESCALE = HEAD_DIM ** -0.5 * math.log2(math.e)


def _params(sem):
    return pltpu.CompilerParams(dimension_semantics=sem, vmem_limit_bytes=VMEM_LIMIT)


def _resident(shape):
    nd = len(shape)
    return pl.BlockSpec(shape, lambda *_: (0,) * nd, pipeline_mode=pl.Buffered(1))


def _silu(a):
    return a * jax.nn.sigmoid(a)


def _gelu_tanh(x):
    cdf = 0.5 * (1.0 + jnp.tanh(math.sqrt(2.0 / math.pi) * (x + 0.044715 * (x * x * x))))
    return x * cdf


def _rmsnorm(x, g):
    return x * lax.rsqrt(jnp.mean(x * x, axis=-1, keepdims=True) + EPS) * g


def _norm_mod(x, g, shift, scale):
    return _rmsnorm(x, g) * (1.0 + scale) + shift


def _rope(u, cos, sa, sb):
    outs = []
    for j in range(u.shape[1] // LANES):
        c = u[:, j * LANES:(j + 1) * LANES]
        up = pltpu.roll(c, LANES - ROT_DIM // 2, axis=1)
        dn = pltpu.roll(c, ROT_DIM // 2, axis=1)
        outs.append(c * cos + up * sa + dn * sb)
    return jnp.concatenate(outs, axis=1)


def _bf16_pieces(x):
    hi = x.astype(BF16)
    r1 = x - hi.astype(F32)
    mid = r1.astype(BF16)
    lo = (r1 - mid.astype(F32)).astype(BF16)
    return jnp.concatenate([hi, mid, lo], axis=0)


SOFTMAX_CHUNK = 64


def _stream_softmax(s_ref, p_ref, m_old, bias, tq):
    rows, n = s_ref.shape
    n_chunks, n_l = rows // SOFTMAX_CHUNK, n // LANES

    def chunk(c):
        sl = slice(c * SOFTMAX_CHUNK, (c + 1) * SOFTMAX_CHUNK)
        sc = s_ref[sl, :]
        if bias is not None:
            q0 = (c * SOFTMAX_CHUNK) % tq
            sc = sc + bias[q0:q0 + SOFTMAX_CHUNK]
        return sl, [sc[:, j * LANES:(j + 1) * LANES] for j in range(n_l)]

    partial_max = jnp.concatenate([functools.reduce(jnp.maximum, chunk(c)[1]) for c in range(n_chunks)], axis=0)
    m_new = jnp.maximum(m_old, jnp.broadcast_to(jnp.max(partial_max, axis=1, keepdims=True), (rows, LANES)))
    sums = []
    for c in range(n_chunks):
        sl, cols = chunk(c)
        m_c = m_new[sl]
        ps = [jnp.exp2(col - m_c) for col in cols]
        for j, pj in enumerate(ps):
            p_ref[sl, j * LANES:(j + 1) * LANES] = pj.astype(BF16)
        sums.append(functools.reduce(jnp.add, ps))
    return m_new, jnp.concatenate(sums, axis=0)


def _pick_topk_rows(g, k):
    n_ids = g.shape[0]
    ids = lax.broadcasted_iota(jnp.int32, g.shape, 0).astype(F32)
    picked = jnp.zeros(g.shape, F32)
    for _ in range(k):
        mx = jnp.max(g, axis=0, keepdims=True)
        first = jnp.min(jnp.where(g == mx, ids, float(n_ids)), axis=0, keepdims=True)
        hit = ids == first
        picked = jnp.where(hit, 1.0, picked)
        g = jnp.where(hit, -jnp.inf, g)
    return picked


def _mod_kernel(c_ref, w_ref, b_ref, o_ref):
    cond = _silu(c_ref[...])
    o_ref[0] = jnp.dot(cond.astype(BF16), w_ref[0].astype(BF16), preferred_element_type=F32) + b_ref[0]


def _modulation(c, mod_w, mod_b):
    depth, d, n = mod_w.shape
    bsz = c.shape[0]
    bp = -(-bsz // 8) * 8
    cp = jnp.pad(c, ((0, bp - bsz), (0, 0)))
    tn = 1152
    out = pl.pallas_call(
        _mod_kernel,
        out_shape=jax.ShapeDtypeStruct((depth, bp, n), F32),
        grid=(depth, n // tn),
        in_specs=[
            pl.BlockSpec((bp, d), lambda l, j: (0, 0)),
            pl.BlockSpec((1, d, tn), lambda l, j: (l, 0, j)),
            pl.BlockSpec((1, 1, tn), lambda l, j: (l, 0, j)),
        ],
        out_specs=pl.BlockSpec((1, bp, tn), lambda l, j: (l, 0, j)),
        compiler_params=_params(("parallel", "parallel")),
        name="adaln_mod",
    )(cp, mod_w, mod_b.reshape(depth, 1, n))
    return out[:, :bsz].reshape(depth, bsz, 9, d)


def _ffn_kernel(x_ref, mod_ref, g_ref, w1_ref, w2_ref, gf_ref, *rest, row0, final, n_mix):
    o_ref = rest[-1]
    x = x_ref[0]
    m = mod_ref[0]
    if n_mix:
        mixed = sum(jnp.dot(y_ref[0], w_ref[...], preferred_element_type=F32)
                    for y_ref, w_ref in zip(rest[:n_mix], rest[n_mix:2 * n_mix]))
        x = x + m[5:6] * mixed
    h = _norm_mod(x, g_ref[...], m[row0:row0 + 1], m[row0 + 1:row0 + 2]).astype(BF16)
    d_ff = w2_ref.shape[2]
    a = jnp.dot(h, w1_ref[0, 0, :, 0:d_ff], preferred_element_type=F32)
    b = jnp.dot(h, w1_ref[0, 0, :, d_ff:2 * d_ff], preferred_element_type=F32)
    act = (_silu(a) * b).astype(BF16)
    y = jnp.dot(act, w2_ref[0, 0], preferred_element_type=F32)
    out = x + (0.5 * m[row0 + 2:row0 + 3]) * y
    if final:
        out = _rmsnorm(out, gf_ref[...])
    o_ref[0] = out


def _ffn(x, mod_l, g, w1, w2, which, gf, *, row0, final, tm, mix=()):
    bsz, s, d = x.shape
    picked = lambda shape: pl.BlockSpec((1, 1) + shape[2:], lambda b, i: which + (0, 0),
                                        pipeline_mode=pl.Buffered(1))
    kern = functools.partial(_ffn_kernel, row0=row0, final=final, n_mix=len(mix))
    row = lambda b, i: (b, i, 0)
    return pl.pallas_call(
        kern,
        out_shape=jax.ShapeDtypeStruct(x.shape, F32),
        grid=(bsz, s // tm),
        in_specs=[
            pl.BlockSpec((1, tm, d), row),
            pl.BlockSpec((1, 9, d), lambda b, i: (b, 0, 0)),
            _resident((1, d)),
            picked(w1.shape),
            picked(w2.shape),
            _resident((1, d)),
        ] + [pl.BlockSpec((1, tm, y.shape[2]), row) for y, _ in mix] + [_resident(w.shape) for _, w in mix],
        out_specs=pl.BlockSpec((1, tm, d), row),
        compiler_params=_params(("parallel", "parallel")),
        name="ffn_final" if final else ("mix_ffn" if mix else "ffn"),
    )(x, mod_l, g.reshape(1, d), w1, w2, gf.reshape(1, d), *[y for y, _ in mix], *[w for _, w in mix])


def _inproj0_kernel(x_ref, mod_ref, g_ref, w_ref, cos_ref, sa_ref, sb_ref,
                    xg_ref, q_ref, k_ref, v_ref, cent_ref):
    m = mod_ref[0]
    h = _norm_mod(x_ref[0], g_ref[...], m[3:4], m[4:5]).astype(BF16)
    u = jnp.dot(h, w_ref[...], preferred_element_type=F32)
    o1 = 2 * LRU_WIDTH
    xg_ref[0] = u[:, :o1]
    cos, sa, sb = cos_ref[...], sa_ref[...], sb_ref[...]
    q = _rope(u[:, o1:o1 + MOBA_WIDTH], cos, sa, sb)
    k = _rope(u[:, o1 + MOBA_WIDTH:o1 + 2 * MOBA_WIDTH], cos, sa, sb)
    q_ref[0] = (q * Q_PRESCALE).astype(BF16)
    k_ref[0] = k.astype(BF16)
    v_ref[0] = u[:, o1 + 2 * MOBA_WIDTH:].astype(BF16)
    cent_ref[0, 0] = jnp.mean(k, axis=0, keepdims=True)


def _inproj0(x, mod_l, g, w, rope_tabs):
    bsz, s, d = x.shape
    tm = MOBA_BLOCK
    nb = s // tm
    n = w.shape[1]
    row = lambda b, i: (b, i, 0)
    tab = pl.BlockSpec((tm, LANES), lambda b, i: (i, 0))
    return pl.pallas_call(
        _inproj0_kernel,
        out_shape=(
            jax.ShapeDtypeStruct((bsz, s, 2 * LRU_WIDTH), F32),
            jax.ShapeDtypeStruct((bsz, s, MOBA_WIDTH), BF16),
            jax.ShapeDtypeStruct((bsz, s, MOBA_WIDTH), BF16),
            jax.ShapeDtypeStruct((bsz, s, MOBA_WIDTH), BF16),
            jax.ShapeDtypeStruct((bsz, nb, 1, MOBA_WIDTH), F32),
        ),
        grid=(bsz, nb),
        in_specs=[
            pl.BlockSpec((1, tm, d), row),
            pl.BlockSpec((1, 9, d), lambda b, i: (b, 0, 0)),
            _resident((1, d)),
            _resident((d, n)),
            tab, tab, tab,
        ],
        out_specs=(
            pl.BlockSpec((1, tm, 2 * LRU_WIDTH), row),
            pl.BlockSpec((1, tm, MOBA_WIDTH), row),
            pl.BlockSpec((1, tm, MOBA_WIDTH), row),
            pl.BlockSpec((1, tm, MOBA_WIDTH), row),
            pl.BlockSpec((1, 1, 1, MOBA_WIDTH), lambda b, i: (b, i, 0, 0)),
        ),
        compiler_params=_params(("parallel", "parallel")),
        name="inproj0",
    )(x, mod_l, g.reshape(1, d), w, *rope_tabs)


def _lru_kernel(xg_ref, cw_ref, cb_ref, wa_ref, ba_ref, wx_ref, bx_ref, lam_ref, y_ref, xbuf, hprev, *, tm):
    c = LRU_WIDTH
    halo = 8

    @pl.when(pl.program_id(1) == 0)
    def _():
        xbuf[0:halo, :] = jnp.zeros((halo, c), F32)
        hprev[...] = jnp.zeros(hprev.shape, F32)

    x = xg_ref[0, :, 0:c]
    gate_in = xg_ref[0, :, c:2 * c]
    xbuf[halo:halo + tm, :] = x
    cw = cw_ref[...]
    xc = cb_ref[...] + cw[CONV_WIDTH - 1:CONV_WIDTH] * x
    for kk in range(CONV_WIDTH - 1):
        back = CONV_WIDTH - 1 - kk
        xc = xc + cw[kk:kk + 1] * xbuf[halo - back:halo - back + tm, :]
    xbuf[0:halo, :] = x[tm - halo:tm, :]

    xcb = xc.astype(BF16)
    r = jax.nn.sigmoid(jnp.dot(xcb, wa_ref[...], preferred_element_type=F32) + ba_ref[...])
    ig = jax.nn.sigmoid(jnp.dot(xcb, wx_ref[...], preferred_element_type=F32) + bx_ref[...])
    nlam = -lam_ref[...]
    softplus = jnp.maximum(nlam, 0.0) + jnp.log1p(jnp.exp(-jnp.abs(nlam)))
    log_a = (-LRU_C * r) * softplus
    a = jnp.exp(log_a)
    mult = jnp.sqrt(-jnp.tanh(log_a) * (a * a + 1.0))
    bt = mult * (ig * xc)

    row = lax.broadcasted_iota(jnp.int32, (tm, c), 0)
    step = 1
    while step < tm:
        a_sh = pltpu.roll(a, step, axis=0)
        b_sh = pltpu.roll(bt, step, axis=0)
        valid = row >= step
        bt = jnp.where(valid, a * b_sh + bt, bt)
        a = jnp.where(valid, a * a_sh, a)
        step *= 2
    h = bt + a * hprev[0:1, :]
    hprev[0:1, :] = h[tm - 1:tm, :]
    y_ref[0] = (h * _gelu_tanh(gate_in)).astype(BF16)


def _lru(xg, cw, cb, wa_bd, ba, wx_bd, bx, lam, *, tm):
    bsz, s, _ = xg.shape
    c = LRU_WIDTH
    vec = _resident((1, c))
    return pl.pallas_call(
        functools.partial(_lru_kernel, tm=tm),
        out_shape=jax.ShapeDtypeStruct((bsz, s, c), BF16),
        grid=(bsz, s // tm),
        in_specs=[
            pl.BlockSpec((1, tm, 2 * c), lambda b, i: (b, i, 0)),
            _resident((CONV_WIDTH, c)), vec,
            _resident((c, c)), vec,
            _resident((c, c)), vec,
            vec,
        ],
        out_specs=pl.BlockSpec((1, tm, c), lambda b, i: (b, i, 0)),
        scratch_shapes=[pltpu.VMEM((tm + 8, c), F32), pltpu.VMEM((8, c), F32)],
        compiler_params=_params(("parallel", "arbitrary")),
        name="rg_lru",
    )(xg, cw, cb.reshape(1, c), wa_bd, ba.reshape(1, c), wx_bd, bx.reshape(1, c), lam.reshape(1, c))


def _moba_kernel(q_ref, k_ref, v_ref, cent_ref, hot_ref, o_ref, lhs_ref, s_ref, p_ref, m_ref, l_ref, acc_ref):
    tq = MOBA_TQ
    rows = 2 * tq
    nb_pad = -(-(k_ref.shape[1] // MOBA_BLOCK) // 8) * 8
    i = pl.program_id(2)
    lane = lax.broadcasted_iota(jnp.int32, (tq, LANES), 1)
    bid = lax.broadcasted_iota(jnp.int32, (LANES, rows), 0)
    qidx = jnp.concatenate([lax.broadcasted_iota(jnp.int32, (LANES, tq), 1)] * 2, axis=1)
    own = (tq // MOBA_BLOCK) * i + qidx // MOBA_BLOCK
    past = bid < own
    qidx_r = jnp.concatenate([lax.broadcasted_iota(jnp.int32, (nb_pad, tq), 1)] * 2, axis=1)
    past_real = lax.broadcasted_iota(jnp.int32, (nb_pad, rows), 0) < (tq // MOBA_BLOCK) * i + qidx_r // MOBA_BLOCK
    chunks = [slice(c * LANES, (c + 1) * LANES) for c in range(MOBA_CHUNKS)]

    for c, cl in enumerate(chunks):
        q = q_ref[0, :, cl]
        zero = jnp.zeros_like(q)
        q2 = jnp.concatenate([jnp.where(lane < HEAD_DIM, q, zero), jnp.where(lane >= HEAD_DIM, q, zero)], axis=0)
        gate3 = lax.dot_general(_bf16_pieces(cent_ref[0, :, cl]), q2, NT_DIMS, preferred_element_type=F32)
        gate_t = gate3[:LANES] + gate3[LANES:2 * LANES] + gate3[2 * LANES:]
        sel_t = _pick_topk_rows(jnp.where(past_real, gate_t[:nb_pad], -jnp.inf), MOBA_TOPK)
        sel_t = jnp.concatenate([sel_t, jnp.zeros((LANES - nb_pad, rows), F32)], axis=0)
        drop = jnp.where((past & (sel_t > 0.5)) | (bid == own), 0.0, -BIG).T.astype(BF16)
        lhs_ref[c] = jnp.concatenate([q2, drop], axis=1)
        m_ref[c] = jnp.full((rows, LANES), NEG, F32)
        l_ref[c] = jnp.zeros((rows, LANES), F32)
        acc_ref[c] = jnp.zeros((rows, LANES), F32)

    def tile(kt, bias):
        k0 = pl.multiple_of(kt * tq, tq)
        for c, cl in enumerate(chunks):
            rhs = jnp.concatenate([k_ref[0, pl.ds(k0, tq), cl], hot_ref[pl.ds(k0, tq), :]], axis=1)
            s_ref[c] = lax.dot_general(lhs_ref[c], rhs, NT_DIMS, preferred_element_type=F32)
        for c, cl in enumerate(chunks):
            m_old = m_ref[c]
            m_new, p_sum = _stream_softmax(s_ref.at[c], p_ref.at[c], m_old, bias, tq)
            alpha = jnp.exp2(m_old - m_new)
            m_ref[c] = m_new
            l_ref[c] = alpha * l_ref[c] + p_sum
            acc_ref[c] = alpha * acc_ref[c] + jnp.dot(p_ref[c], v_ref[0, pl.ds(k0, tq), cl],
                                                      preferred_element_type=F32)

    @pl.when(i >= 0)
    def _():
        qpos = lax.broadcasted_iota(jnp.int32, (tq, tq), 0)
        kpos = lax.broadcasted_iota(jnp.int32, (tq, tq), 1)
        tile(i, jnp.where(kpos <= qpos, 0.0, NEG))

    @pl.loop(0, i)
    def _(kt):
        tile(kt, None)

    for c, cl in enumerate(chunks):
        out = acc_ref[c] / jnp.sum(l_ref[c], axis=1, keepdims=True)
        o_ref[0, :, cl] = jnp.where(lane < HEAD_DIM, out[:tq], out[tq:]).astype(BF16)


def _moba(q, k, v, cent):
    bsz, s, w = q.shape
    tq = MOBA_TQ
    nb = s // MOBA_BLOCK
    assert nb <= LANES and s % tq == 0
    cent = jnp.pad(cent, ((0, 0), (0, LANES - nb), (0, 0)))
    hot = _block_one_hot(s, MOBA_BLOCK)
    nc, cw = MOBA_CHUNKS, MOBA_CHUNKS * LANES
    return pl.pallas_call(
        _moba_kernel,
        out_shape=jax.ShapeDtypeStruct((bsz, s, w), BF16),
        grid=(bsz, w // cw, s // tq),
        in_specs=[
            pl.BlockSpec((1, tq, cw), lambda b, hp, i: (b, i, hp)),
            pl.BlockSpec((1, s, cw), lambda b, hp, i: (b, 0, hp), pipeline_mode=pl.Buffered(1)),
            pl.BlockSpec((1, s, cw), lambda b, hp, i: (b, 0, hp), pipeline_mode=pl.Buffered(1)),
            pl.BlockSpec((1, LANES, cw), lambda b, hp, i: (b, 0, hp)),
            _resident(hot.shape),
        ],
        out_specs=pl.BlockSpec((1, tq, cw), lambda b, hp, i: (b, i, hp)),
        scratch_shapes=[pltpu.VMEM((nc, 2 * tq, 2 * LANES), BF16), pltpu.VMEM((nc, 2 * tq, tq), F32),
                        pltpu.VMEM((nc, 2 * tq, tq), BF16)] + [pltpu.VMEM((nc, 2 * tq, LANES), F32)] * 3,
        compiler_params=_params(("parallel", "parallel", "parallel")),
        name="moba_attn",
    )(q, k, v, cent, hot)


def _inproj1_kernel(x_ref, mod_ref, g_ref, w_ref, cos_ref, sa_ref, sb_ref,
                    qn_ref, qr_ref, kvc_ref, kv4_ref, gate_ref):
    m = mod_ref[0]
    h = _norm_mod(x_ref[0], g_ref[...], m[3:4], m[4:5]).astype(BF16)
    u = jnp.dot(h, w_ref[...], preferred_element_type=F32)
    cos, sa, sb = cos_ref[...], sa_ref[...], sb_ref[...]
    q = u[:, :NSA_QD] * Q_PRESCALE
    qn_ref[0] = q.astype(BF16)
    qr_ref[0] = _rope(q, cos, sa, sb).astype(BF16)
    o = NSA_QD
    kvc_ref[0, 0] = u[:, o:o + NSA_KVD]
    kvc_ref[0, 1] = u[:, o + NSA_KVD:o + 2 * NSA_KVD]
    o += 2 * NSA_KVD
    ks = _rope(u[:, o:o + NSA_KVD], cos, sa, sb)
    vs = u[:, o + NSA_KVD:o + 2 * NSA_KVD]
    kw = _rope(u[:, o + 2 * NSA_KVD:o + 3 * NSA_KVD], cos, sa, sb)
    vw = u[:, o + 3 * NSA_KVD:o + 4 * NSA_KVD]
    kv4_ref[0] = jnp.concatenate([ks, vs, kw, vw], axis=1).astype(BF16)
    gate_ref[0] = jax.nn.sigmoid(u[:, o + 4 * NSA_KVD:])


def _inproj1(x, mod_l, g, w, rope_tabs, *, tm):
    bsz, s, d = x.shape
    n = w.shape[1]
    row = lambda b, i: (b, i, 0)
    tab = pl.BlockSpec((tm, LANES), lambda b, i: (i, 0))
    widths = (NSA_QD, NSA_QD, None, 4 * NSA_KVD, LANES)
    dtypes = (BF16, BF16, F32, BF16, F32)
    shape = lambda wd: (bsz, s, wd) if wd else (bsz, 2, s, NSA_KVD)
    spec = lambda wd: pl.BlockSpec((1, tm, wd), row) if wd else pl.BlockSpec((1, 2, tm, NSA_KVD),
                                                                             lambda b, i: (b, 0, i, 0))
    return pl.pallas_call(
        _inproj1_kernel,
        out_shape=tuple(jax.ShapeDtypeStruct(shape(wd), dt) for wd, dt in zip(widths, dtypes)),
        grid=(bsz, s // tm),
        in_specs=[
            pl.BlockSpec((1, tm, d), row),
            pl.BlockSpec((1, 9, d), lambda b, i: (b, 0, 0)),
            _resident((1, d)),
            _resident((d, n)),
            tab, tab, tab,
        ],
        out_specs=tuple(spec(wd) for wd in widths),
        compiler_params=_params(("parallel", "parallel")),
        name="inproj1",
    )(x, mod_l, g.reshape(1, d), w, *rope_tabs)


def _compress_kernel(kvc_ref, pos_ref, w1_ref, w2_ref, o_ref):
    nrow = o_ref.shape[2]
    for kv in range(2):
        top = jnp.zeros((nrow, w1_ref.shape[3]), F32)
        bot = jnp.zeros((nrow, w1_ref.shape[3]), F32)
        for r in range(CMP_STRIDE):
            x = kvc_ref[0, kv, pl.ds(r, nrow, stride=CMP_STRIDE), :]
            top += jnp.dot((x + pos_ref[kv, r:r + 1]).astype(BF16), w1_ref[kv, r], preferred_element_type=F32)
            bot += jnp.dot((x + pos_ref[kv, CMP_STRIDE + r:CMP_STRIDE + r + 1]).astype(BF16),
                           w1_ref[kv, CMP_STRIDE + r], preferred_element_type=F32)
        pre = top + pltpu.roll(bot, nrow - 1, axis=0)
        o_ref[0, kv] = jnp.dot(_gelu_tanh(pre).astype(BF16), w2_ref[kv], preferred_element_type=F32).astype(BF16)


def _compress(kvc, pos, w1, w2):
    bsz, _, s, _ = kvc.shape
    nrow = s // CMP_STRIDE
    eye = jnp.eye(NSA_KV_GROUPS, dtype=F32)
    pos2 = jnp.tile(pos, (1, 1, NSA_KV_GROUPS))
    hidden = w1.shape[-1]
    w1r = w1.reshape(2, CMP_LEN, HEAD_DIM, hidden)
    w1bd = jnp.einsum('gh,kric->krgihc', eye, w1r).reshape(2, CMP_LEN, NSA_KVD, NSA_KV_GROUPS * hidden)
    w2bd = jnp.einsum('gh,kcd->kgchd', eye, w2).reshape(2, NSA_KV_GROUPS * hidden, NSA_KVD)
    return pl.pallas_call(
        _compress_kernel,
        out_shape=jax.ShapeDtypeStruct((bsz, 2, nrow, NSA_KVD), BF16),
        grid=(bsz,),
        in_specs=[
            pl.BlockSpec((1, 2, s, NSA_KVD), lambda b: (b, 0, 0, 0)),
            _resident(pos2.shape),
            _resident(w1bd.shape),
            _resident(w2bd.shape),
        ],
        out_specs=pl.BlockSpec((1, 2, nrow, NSA_KVD), lambda b: (b, 0, 0, 0)),
        compiler_params=_params(("parallel",)),
        name="nsa_compress",
    )(kvc, pos2, w1bd.astype(BF16), w2bd.astype(BF16))


def _nsa_kernel(qn_ref, qr_ref, kv4_ref, cmp_ref, gate_ref, ovl_ref, hot_ref, gexp_ref, o_ref,
                lhs_ref, s_ref, p_ref, m_ref, acc_ref, *, seq, tk):
    tq = NSA_TQ
    nrow = cmp_ref.shape[2]
    ns = seq // SEL_BLOCK
    n_g, n_r = NSA_KV_GROUPS, NSA_HPG
    n_h = n_g * n_r
    rows = n_h * tq
    t0 = pl.program_id(1) * tq
    lane = lax.broadcasted_iota(jnp.int32, (tq, LANES), 1)
    gates = gate_ref[0]

    def stack(q_ref):
        parts = []
        for g in range(n_g):
            in_grp = (lane >= g * HEAD_DIM) & (lane < (g + 1) * HEAD_DIM)
            for r in range(n_r):
                ch = q_ref[0, :, r * LANES:(r + 1) * LANES]
                parts.append(jnp.where(in_grp, ch, jnp.zeros_like(ch)))
        return jnp.concatenate(parts, axis=0)

    half = rows // n_g
    neg_stats = jnp.full((rows, LANES), NEG, F32)

    def times_values(p_view, v, g, extra=None):
        vl = lax.broadcasted_iota(jnp.int32, v.shape, 1)
        vg = jnp.where((vl >= g * HEAD_DIM) & (vl < (g + 1) * HEAD_DIM), v, jnp.ones_like(v))
        if extra is not None:
            vg = jnp.concatenate([vg, extra], axis=1)
        return jnp.dot(p_view[g * half:(g + 1) * half, :], vg, preferred_element_type=F32)

    sc_ref, pc_ref = s_ref.at[:, 0:nrow], p_ref.at[:, 0:nrow]
    tq_c = t0 + lax.broadcasted_iota(jnp.int32, (tq, nrow), 0)
    cend = lax.broadcasted_iota(jnp.int32, (tq, nrow), 1) * CMP_STRIDE + (CMP_LEN - 1)
    sc_ref[...] = lax.dot_general(stack(qn_ref), cmp_ref[0, 0], NT_DIMS, preferred_element_type=F32)
    _stream_softmax(sc_ref, pc_ref, neg_stats, jnp.where(cend <= tq_c, 0.0, NEG), tq)
    both = jnp.concatenate([times_values(pc_ref, cmp_ref[0, 1], g, extra=ovl_ref[...]) for g in range(n_g)], axis=0)
    acc_c, imp_h = both[:, 0:LANES], both[:, LANES:]
    lane_r = lax.broadcasted_iota(jnp.int32, (rows, LANES), 1)
    grp_r = lax.broadcasted_iota(jnp.int32, (rows, LANES), 0) // half
    den_c = jnp.where(lane_r // HEAD_DIM == grp_r, pltpu.roll(acc_c, HEAD_DIM, axis=1), acc_c)
    has_block = jnp.where(t0 + lax.broadcasted_iota(jnp.int32, (tq, LANES), 0) >= CMP_LEN - 1, 1.0, 0.0)
    inv_c = jnp.concatenate([has_block] * n_h, axis=0) / den_c
    o_c = acc_c * inv_c

    imp_h = imp_h * inv_c
    imp_t = jnp.concatenate([functools.reduce(jnp.add, [imp_h[(g * n_r + r) * tq:(g * n_r + r + 1) * tq]
                                                        for r in range(n_r)]) for g in range(n_g)], axis=0).T
    jn = lax.broadcasted_iota(jnp.int32, (LANES, n_g * tq), 0)
    tpos = t0 + lax.broadcasted_iota(jnp.int32, (LANES, tq), 1)
    blk_t = jnp.concatenate([tpos] * n_g, axis=1) // SEL_BLOCK
    valid = jn <= blk_t
    forced = (jn == 0) | (jn == blk_t) | (jn == blk_t - 1)
    others = _pick_topk_rows(jnp.where(valid & jnp.logical_not(forced), imp_t, -jnp.inf), min(SEL_TOPK, ns) - 3)
    keep = valid & (forced | (others > 0.5))

    lhs_ref[:, 0:LANES] = stack(qr_ref)
    wk = WINDOW + tq
    sw_ref, pw_ref = s_ref.at[:, 0:wk], p_ref.at[:, 0:wk]
    kstart = pl.multiple_of(jnp.clip(t0 - WINDOW, 0, seq - wk), tq)
    diff = (t0 + lax.broadcasted_iota(jnp.int32, (tq, wk), 0)) - (
        kstart + lax.broadcasted_iota(jnp.int32, (tq, wk), 1))
    sw_ref[...] = lax.dot_general(lhs_ref[:, 0:LANES], kv4_ref[0, pl.ds(kstart, wk), 2 * LANES:3 * LANES], NT_DIMS,
                                  preferred_element_type=F32)
    _stream_softmax(sw_ref, pw_ref, neg_stats, jnp.where((diff >= 0) & (diff < WINDOW), 0.0, NEG), tq)
    acc_w = jnp.concatenate([times_values(pw_ref, kv4_ref[0, pl.ds(kstart, wk), 3 * LANES:4 * LANES], g)
                             for g in range(n_g)], axis=0)

    drop = jnp.where(keep, 0.0, -BIG).T.astype(BF16)
    lhs_ref[:, LANES:2 * LANES] = jnp.concatenate(
        [drop[g * tq:(g + 1) * tq] for g in range(n_g) for _ in range(n_r)], axis=0)

    m_ref[...] = neg_stats
    acc_ref[...] = jnp.zeros(acc_ref.shape, F32)

    def sel_tile(k0, width, bias):
        rhs = jnp.concatenate([kv4_ref[0, pl.ds(k0, width), 0:LANES], hot_ref[pl.ds(k0, width), :]], axis=1)
        views = []
        for g in range(n_g):
            gs = slice(g * half, (g + 1) * half)
            sg_ref = s_ref.at[gs, 0:width]
            sg_ref[...] = lax.dot_general(lhs_ref[gs, :], rhs, NT_DIMS, preferred_element_type=F32)
            views.append((gs, sg_ref, p_ref.at[gs, 0:width]))
        for g, (gs, sg_ref, pg_ref) in enumerate(views):
            m_old = m_ref[gs, :]
            m_new, _ = _stream_softmax(sg_ref, pg_ref, m_old, bias, tq)
            m_ref[gs, :] = m_new
            vl = lax.broadcasted_iota(jnp.int32, (width, LANES), 1)
            v = kv4_ref[0, pl.ds(k0, width), LANES:2 * LANES]
            vg = jnp.where((vl >= g * HEAD_DIM) & (vl < (g + 1) * HEAD_DIM), v, jnp.ones_like(v))
            acc_ref[gs, :] = jnp.exp2(m_old - m_new) * acc_ref[gs, :] + jnp.dot(pg_ref[...], vg,
                                                                                preferred_element_type=F32)

    last = t0 // tk

    @pl.loop(0, last)
    def _(kt):
        sel_tile(pl.multiple_of(kt * tk, tk), tk, None)

    k_last = pl.multiple_of(last * tk, tk)

    def last_tile(width):
        kpos = k_last + lax.broadcasted_iota(jnp.int32, (tq, width), 1)
        tq_s = t0 + lax.broadcasted_iota(jnp.int32, (tq, width), 0)
        sel_tile(k_last, width, jnp.where(kpos <= tq_s, 0.0, NEG))

    in_first_half = t0 - k_last + tq <= tk // 2
    pl.when(in_first_half)(lambda: last_tile(tk // 2))
    pl.when(jnp.logical_not(in_first_half))(lambda: last_tile(tk))

    acc_s = acc_ref[...]

    branches = (o_c, acc_s / pltpu.roll(acc_s, HEAD_DIM, axis=1), acc_w / pltpu.roll(acc_w, HEAD_DIM, axis=1))
    gates_b = gates.astype(BF16)
    spread = [jnp.dot(gates_b, gexp_ref[c], preferred_element_type=F32) for c in range(3)]
    for r in range(n_r):
        y = None
        for c in range(3):
            heads = [branches[c][(g * n_r + r) * tq:(g * n_r + r + 1) * tq] for g in range(n_g)]
            term = spread[c][:, r * LANES:(r + 1) * LANES] * jnp.where(lane < HEAD_DIM, heads[0], heads[1])
            y = term if y is None else y + term
        o_ref[0, :, r * LANES:(r + 1) * LANES] = y.astype(BF16)


def _nsa(qn, qr, kv4, cmp, gates, *, tk):
    bsz, s, _ = qn.shape
    tq = NSA_TQ
    ns = s // SEL_BLOCK
    assert ns <= LANES
    assert min(SEL_TOPK, ns) >= 3
    ovl = jnp.pad(_overlap(cmp.shape[2], ns), ((0, 0), (0, LANES - ns))).astype(BF16)
    hot = _block_one_hot(s, SEL_BLOCK)
    head_of_p = _nsa_head_perm() // HEAD_DIM
    gexp = (jnp.arange(LANES)[None, :, None] == 3 * head_of_p[None, None, :] + jnp.arange(3)[:, None, None]).astype(BF16)
    rows = NSA_HEADS * tq
    width = max(cmp.shape[2], WINDOW + tq, tk)
    row = lambda b, i: (b, i, 0)
    return pl.pallas_call(
        functools.partial(_nsa_kernel, seq=s, tk=tk),
        out_shape=jax.ShapeDtypeStruct((bsz, s, NSA_QD), BF16),
        grid=(bsz, s // tq),
        in_specs=[
            pl.BlockSpec((1, tq, NSA_QD), row),
            pl.BlockSpec((1, tq, NSA_QD), row),
            pl.BlockSpec((1, s, 4 * NSA_KVD), lambda b, i: (b, 0, 0), pipeline_mode=pl.Buffered(1)),
            pl.BlockSpec((1, 2) + cmp.shape[2:], lambda b, i: (b, 0, 0, 0), pipeline_mode=pl.Buffered(1)),
            pl.BlockSpec((1, tq, LANES), row),
            _resident(ovl.shape),
            _resident(hot.shape),
            _resident(gexp.shape),
        ],
        out_specs=pl.BlockSpec((1, tq, NSA_QD), row),
        scratch_shapes=[pltpu.VMEM((rows, 2 * LANES), BF16), pltpu.VMEM((rows, width), F32),
                        pltpu.VMEM((rows, width), BF16),
                        pltpu.VMEM((rows, LANES), F32), pltpu.VMEM((rows, LANES), F32)],
        compiler_params=_params(("parallel", "parallel")),
        name="nsa_attn",
    )(qn, qr, kv4, cmp, gates, ovl, hot, gexp)


def _rope_tables(s):
    half = ROT_DIM // 2
    pos = jnp.arange(s, dtype=F32)
    freqs = ROPE_THETA ** (-jnp.arange(half, dtype=F32) * 2.0 / ROT_DIM)
    ang = pos[:, None] * freqs[None, :]
    cos, sin = jnp.cos(ang), jnp.sin(ang)
    ones = jnp.ones((s, HEAD_DIM - ROT_DIM), F32)
    zeros = jnp.zeros((s, HEAD_DIM - ROT_DIM), F32)
    zh = jnp.zeros((s, half), F32)
    cos_t = jnp.concatenate([cos, cos, ones], axis=1)
    sa_t = jnp.concatenate([-sin, zh, zeros], axis=1)
    sb_t = jnp.concatenate([zh, sin, zeros], axis=1)
    return tuple(jnp.tile(t, (1, LANES // HEAD_DIM)) for t in (cos_t, sa_t, sb_t))


def _block_one_hot(s, block):
    return (jnp.arange(s)[:, None] // block == jnp.arange(LANES)[None, :]).astype(BF16)


def _block_diag(w):
    n, a, b = w.shape
    eye = jnp.eye(n, dtype=w.dtype)
    return (eye[:, None, :, None] * w[:, :, None, :]).reshape(n * a, n * b)


def _nsa_head_perm():
    r = jnp.arange(NSA_HPG)[:, None, None]
    g = jnp.arange(NSA_KV_GROUPS)[None, :, None]
    dd = jnp.arange(HEAD_DIM)[None, None, :]
    return (HEAD_DIM * (NSA_HPG * g + r) + dd).reshape(-1)


def _overlap(nrow, ns):
    ci = jnp.arange(nrow)[:, None] * CMP_STRIDE
    sj = jnp.arange(ns)[None, :] * SEL_BLOCK
    ov = jnp.clip(jnp.minimum(ci + CMP_LEN, sj + SEL_BLOCK) - jnp.maximum(ci, sj), 0, None)
    return ov.astype(F32) / CMP_LEN


def kernel(x, c, mod_w, mod_b, norm_g, ffn_w1, ffn_w2, mix0_in_w, lru_conv_w, lru_conv_b, lru_wa, lru_ba,
           lru_wx, lru_bx, lru_lambda, mix0_out_w, mix1_in_w, cmp_pos, cmp_w1, cmp_w2, mix1_out_w, final_norm_g):
    bsz, s, d = x.shape
    depth = mod_w.shape[0]
    assert d == D_MODEL and s % 512 == 0 and s >= WINDOW + NSA_TQ
    tm = 512

    mod = _modulation(c, mod_w, mod_b)
    tabs = _rope_tables(s)
    w1 = ffn_w1.astype(BF16)
    w2 = ffn_w2.astype(BF16)

    for l in range(depth):
        j = l // 2
        x = _ffn(x, mod[l], norm_g[l, 0], w1, w2, (l, 0), final_norm_g, row0=0, final=False, tm=tm)
        if l % 2 == 0:
            xg, q, k, v, cent = _inproj0(x, mod[l], norm_g[l, 1], mix0_in_w[j].astype(BF16), tabs)
            y_lru = _lru(xg, lru_conv_w[j], lru_conv_b[j],
                         _block_diag(lru_wa[j]).astype(BF16), lru_ba[j].reshape(-1),
                         _block_diag(lru_wx[j]).astype(BF16), lru_bx[j].reshape(-1), lru_lambda[j], tm=256)
            y_att = _moba(q, k, v, cent.reshape(bsz, -1, MOBA_WIDTH))
            wo = mix0_out_w[j].astype(BF16)
            mix = ((y_lru, wo[:LRU_WIDTH]), (y_att, wo[LRU_WIDTH:]))
        else:
            perm = _nsa_head_perm()
            w_in = mix1_in_w[j]
            w_in = jnp.concatenate([w_in[:, :NSA_QD][:, perm], w_in[:, NSA_QD:]], axis=1)
            w_in = jnp.pad(w_in, ((0, 0), (0, IN1_PAD - w_in.shape[1]))).astype(BF16)
            qn, qr, kvc, kv4, gates = _inproj1(x, mod[l], norm_g[l, 1], w_in, tabs, tm=tm)
            cmp = _compress(kvc, cmp_pos[j], cmp_w1[j], cmp_w2[j])
            y = _nsa(qn, qr, kv4, cmp, gates, tk=512)
            mix = ((y, mix1_out_w[j][perm].astype(BF16)),)
        x = _ffn(x, mod[l], norm_g[l, 2], w1, w2, (l, 1), final_norm_g,
                 row0=6, final=(l == depth - 1), tm=tm, mix=mix)
    return x
```

```python
import functools
import math

import jax
import jax.numpy as jnp
from jax import lax
from jax.experimental import pallas as pl
from jax.experimental.pallas import tpu as pltpu

F32 = jnp.float32
BF16 = jnp.bfloat16

D_MODEL = 1024
EPS = 1e-6
ROPE_THETA = 500000.0
HEAD_DIM = 64
ROT_DIM = HEAD_DIM // 4
NEG = -1e30
BIG = 1e30
LANES = 128

LRU_WIDTH = D_MODEL // 2
LRU_BLOCKS = 8
CONV_WIDTH = 4
LRU_C = 8.0
MOBA_WIDTH = D_MODEL // 2
MOBA_BLOCK = 256
MOBA_TOPK = 3
MOBA_TQ = 2 * MOBA_BLOCK
MOBA_CHUNKS = 4

NSA_HEADS = D_MODEL // HEAD_DIM
NSA_KV_GROUPS = 2
NSA_HPG = NSA_HEADS // NSA_KV_GROUPS
CMP_LEN = 32
CMP_STRIDE = 16
SEL_BLOCK = 64
SEL_TOPK = 16
WINDOW = 512
NSA_QD = NSA_HEADS * HEAD_DIM
NSA_KVD = NSA_KV_GROUPS * HEAD_DIM
IN1_PAD = NSA_QD + 6 * NSA_KVD + LANES
NSA_TQ = 128

VMEM_LIMIT = 52 * 1024 * 1024

NT_DIMS = (((1,), (1,)), ((), ()))
Q_PRESCALE = HEAD_DIM ** -0.5 * math.log2(math.e)


def _params(sem):
    return pltpu.CompilerParams(dimension_semantics=sem, vmem_limit_bytes=VMEM_LIMIT)


def _resident(shape):
    nd = len(shape)
    return pl.BlockSpec(shape, lambda *_: (0,) * nd, pipeline_mode=pl.Buffered(1))


def _sigmoid(z):
    return 0.5 * jnp.tanh(0.5 * z) + 0.5


def _silu(a):
    return a * jax.nn.sigmoid(a)


def _gelu_tanh(x):
    cdf = 0.5 * (1.0 + jnp.tanh(math.sqrt(2.0 / math.pi) * (x + 0.044715 * (x * x * x))))
    return x * cdf


def _rmsnorm(x, g):
    return x * lax.rsqrt(jnp.mean(x * x, axis=-1, keepdims=True) + EPS) * g


def _norm_mod(x, g, shift, scale):
    return _rmsnorm(x, g) * (1.0 + scale) + shift


def _rope(u, cos, sa, sb):
    outs = []
    for j in range(u.shape[1] // LANES):
        c = u[:, j * LANES:(j + 1) * LANES]
        up = pltpu.roll(c, LANES - ROT_DIM // 2, axis=1)
        dn = pltpu.roll(c, ROT_DIM // 2, axis=1)
        outs.append(c * cos + up * sa + dn * sb)
    return jnp.concatenate(outs, axis=1)


def _bf16_pieces(x):
    hi = x.astype(BF16)
    r1 = x - hi.astype(F32)
    mid = r1.astype(BF16)
    lo = (r1 - mid.astype(F32)).astype(BF16)
    return jnp.concatenate([hi, mid, lo], axis=0)


SOFTMAX_CHUNK = 64


def _stream_softmax(s_ref, p_ref, m_old, bias, tq):
    rows, n = s_ref.shape
    n_chunks, n_l = rows // SOFTMAX_CHUNK, n // LANES

    def chunk(c):
        sl = slice(c * SOFTMAX_CHUNK, (c + 1) * SOFTMAX_CHUNK)
        sc = s_ref[sl, :]
        if bias is not None:
            q0 = (c * SOFTMAX_CHUNK) % tq
            sc = sc + bias[q0:q0 + SOFTMAX_CHUNK]
        return sl, [sc[:, j * LANES:(j + 1) * LANES] for j in range(n_l)]

    partial_max = jnp.concatenate([functools.reduce(jnp.maximum, chunk(c)[1]) for c in range(n_chunks)], axis=0)
    m_new = jnp.maximum(m_old, jnp.broadcast_to(jnp.max(partial_max, axis=1, keepdims=True), (rows, LANES)))
    sums = []
    for c in range(n_chunks):
        sl, cols = chunk(c)
        m_c = m_new[sl]
        ps = [jnp.exp2(col - m_c) for col in cols]
        for j, pj in enumerate(ps):
            p_ref[sl, j * LANES:(j + 1) * LANES] = pj.astype(BF16)
        sums.append(functools.reduce(jnp.add, ps))
    return m_new, jnp.concatenate(sums, axis=0)


def _pick_topk_rows(g, k):
    n_ids = g.shape[0]
    ids = lax.broadcasted_iota(jnp.int32, g.shape, 0).astype(F32)
    picked = jnp.zeros(g.shape, F32)
    for _ in range(k):
        mx = jnp.max(g, axis=0, keepdims=True)
        first = jnp.min(jnp.where(g == mx, ids, float(n_ids)), axis=0, keepdims=True)
        hit = ids == first
        picked = jnp.where(hit, 1.0, picked)
        g = jnp.where(hit, -jnp.inf, g)
    return picked


def _mod_kernel(c_ref, w_ref, b_ref, o_ref):
    cond = _silu(c_ref[...])
    o_ref[0] = jnp.dot(cond.astype(BF16), w_ref[0].astype(BF16), preferred_element_type=F32) + b_ref[0]


def _modulation(c, mod_w, mod_b):
    depth, d, n = mod_w.shape
    bsz = c.shape[0]
    bp = -(-bsz // 8) * 8
    cp = jnp.pad(c, ((0, bp - bsz), (0, 0)))
    tn = 1152
    out = pl.pallas_call(
        _mod_kernel,
        out_shape=jax.ShapeDtypeStruct((depth, bp, n), F32),
        grid=(depth, n // tn),
        in_specs=[
            pl.BlockSpec((bp, d), lambda l, j: (0, 0)),
            pl.BlockSpec((1, d, tn), lambda l, j: (l, 0, j)),
            pl.BlockSpec((1, 1, tn), lambda l, j: (l, 0, j)),
        ],
        out_specs=pl.BlockSpec((1, bp, tn), lambda l, j: (l, 0, j)),
        compiler_params=_params(("parallel", "parallel")),
        name="adaln_mod",
    )(cp, mod_w, mod_b.reshape(depth, 1, n))
    return out[:, :bsz].reshape(depth, bsz, 9, d)


def _ffn_kernel(x_ref, mod_ref, g_ref, w1_ref, w2_ref, gf_ref, *rest, row0, final, n_mix):
    o_ref = rest[-1]
    x = x_ref[0]
    m = mod_ref[0]
    if n_mix:
        mixed = sum(jnp.dot(y_ref[0], w_ref[...], preferred_element_type=F32)
                    for y_ref, w_ref in zip(rest[:n_mix], rest[n_mix:2 * n_mix]))
        x = x + m[5:6] * mixed
    h = _norm_mod(x, g_ref[...], m[row0:row0 + 1], m[row0 + 1:row0 + 2]).astype(BF16)
    d_ff = w2_ref.shape[2]
    a = jnp.dot(h, w1_ref[0, 0, :, 0:d_ff], preferred_element_type=F32)
    b = jnp.dot(h, w1_ref[0, 0, :, d_ff:2 * d_ff], preferred_element_type=F32)
    act = (_silu(a) * b).astype(BF16)
    y = jnp.dot(act, w2_ref[0, 0], preferred_element_type=F32)
    out = x + (0.5 * m[row0 + 2:row0 + 3]) * y
    if final:
        out = _rmsnorm(out, gf_ref[...])
    o_ref[0] = out


def _ffn(x, mod_l, g, w1, w2, which, gf, *, row0, final, tm, mix=()):
    bsz, s, d = x.shape
    picked = lambda shape: pl.BlockSpec((1, 1) + shape[2:], lambda b, i: which + (0, 0),
                                        pipeline_mode=pl.Buffered(1))
    kern = functools.partial(_ffn_kernel, row0=row0, final=final, n_mix=len(mix))
    row = lambda b, i: (b, i, 0)
    return pl.pallas_call(
        kern,
        out_shape=jax.ShapeDtypeStruct(x.shape, F32),
        grid=(bsz, s // tm),
        in_specs=[
            pl.BlockSpec((1, tm, d), row),
            pl.BlockSpec((1, 9, d), lambda b, i: (b, 0, 0)),
            _resident((1, d)),
            picked(w1.shape),
            picked(w2.shape),
            _resident((1, d)),
        ] + [pl.BlockSpec((1, tm, y.shape[2]), row) for y, _ in mix] + [_resident(w.shape) for _, w in mix],
        out_specs=pl.BlockSpec((1, tm, d), row),
        compiler_params=_params(("parallel", "parallel")),
        name="ffn_final" if final else ("mix_ffn" if mix else "ffn"),
    )(x, mod_l, g.reshape(1, d), w1, w2, gf.reshape(1, d), *[y for y, _ in mix], *[w for _, w in mix])


def _inproj0_kernel(x_ref, mod_ref, g_ref, w_ref, cos_ref, sa_ref, sb_ref,
                    xg_ref, q_ref, k_ref, v_ref, cent_ref):
    m = mod_ref[0]
    h = _norm_mod(x_ref[0], g_ref[...], m[3:4], m[4:5]).astype(BF16)
    u = jnp.dot(h, w_ref[...], preferred_element_type=F32)
    o1 = 2 * LRU_WIDTH
    xg_ref[0] = u[:, :o1]
    cos, sa, sb = cos_ref[...], sa_ref[...], sb_ref[...]
    q = _rope(u[:, o1:o1 + MOBA_WIDTH], cos, sa, sb)
    k = _rope(u[:, o1 + MOBA_WIDTH:o1 + 2 * MOBA_WIDTH], cos, sa, sb)
    q_ref[0] = (q * Q_PRESCALE).astype(BF16)
    k_ref[0] = k.astype(BF16)
    v_ref[0] = u[:, o1 + 2 * MOBA_WIDTH:].astype(BF16)
    cent_ref[0, 0] = jnp.mean(k, axis=0, keepdims=True)


def _inproj0(x, mod_l, g, w, rope_tabs):
    bsz, s, d = x.shape
    tm = MOBA_BLOCK
    nb = s // tm
    n = w.shape[1]
    row = lambda b, i: (b, i, 0)
    tab = pl.BlockSpec((tm, LANES), lambda b, i: (i, 0))
    return pl.pallas_call(
        _inproj0_kernel,
        out_shape=(
            jax.ShapeDtypeStruct((bsz, s, 2 * LRU_WIDTH), F32),
            jax.ShapeDtypeStruct((bsz, s, MOBA_WIDTH), BF16),
            jax.ShapeDtypeStruct((bsz, s, MOBA_WIDTH), BF16),
            jax.ShapeDtypeStruct((bsz, s, MOBA_WIDTH), BF16),
            jax.ShapeDtypeStruct((bsz, nb, 1, MOBA_WIDTH), F32),
        ),
        grid=(bsz, nb),
        in_specs=[
            pl.BlockSpec((1, tm, d), row),
            pl.BlockSpec((1, 9, d), lambda b, i: (b, 0, 0)),
            _resident((1, d)),
            _resident((d, n)),
            tab, tab, tab,
        ],
        out_specs=(
            pl.BlockSpec((1, tm, 2 * LRU_WIDTH), row),
            pl.BlockSpec((1, tm, MOBA_WIDTH), row),
            pl.BlockSpec((1, tm, MOBA_WIDTH), row),
            pl.BlockSpec((1, tm, MOBA_WIDTH), row),
            pl.BlockSpec((1, 1, 1, MOBA_WIDTH), lambda b, i: (b, i, 0, 0)),
        ),
        compiler_params=_params(("parallel", "parallel")),
        name="inproj0",
    )(x, mod_l, g.reshape(1, d), w, *rope_tabs)


def _lru_kernel(xg_ref, cw_ref, cb_ref, wa_ref, ba_ref, wx_ref, bx_ref, lam_ref, y_ref, xbuf, hprev, *, tm):
    c = LRU_WIDTH
    halo = 8

    @pl.when(pl.program_id(1) == 0)
    def _():
        xbuf[0:halo, :] = jnp.zeros((halo, c), F32)
        hprev[...] = jnp.zeros(hprev.shape, F32)

    x = xg_ref[0, :, 0:c]
    gate_in = xg_ref[0, :, c:2 * c]
    xbuf[halo:halo + tm, :] = x
    cw = cw_ref[...]
    xc = cb_ref[...] + cw[CONV_WIDTH - 1:CONV_WIDTH] * x
    for kk in range(CONV_WIDTH - 1):
        back = CONV_WIDTH - 1 - kk
        xc = xc + cw[kk:kk + 1] * xbuf[halo - back:halo - back + tm, :]
    xbuf[0:halo, :] = x[tm - halo:tm, :]

    xcb = xc.astype(BF16)
    r = _sigmoid(jnp.dot(xcb, wa_ref[...], preferred_element_type=F32) + ba_ref[...])
    ig = _sigmoid(jnp.dot(xcb, wx_ref[...], preferred_element_type=F32) + bx_ref[...])
    nlam = -lam_ref[...]
    softplus = jnp.maximum(nlam, 0.0) + jnp.log1p(jnp.exp(-jnp.abs(nlam)))
    log_a = (-LRU_C * r) * softplus
    a = jnp.exp(log_a)
    gain2 = -jnp.tanh(log_a) * (a * a + 1.0)
    mult = jnp.where(gain2 > 0.0, gain2 * lax.rsqrt(gain2), 0.0)
    bt = mult * (ig * xc)

    row = lax.broadcasted_iota(jnp.int32, (tm, c), 0)
    step = 1
    while step < tm:
        a_sh = pltpu.roll(a, step, axis=0)
        b_sh = pltpu.roll(bt, step, axis=0)
        valid = row >= step
        bt = jnp.where(valid, a * b_sh + bt, bt)
        a = jnp.where(valid, a * a_sh, a)
        step *= 2
    h = bt + a * hprev[0:1, :]
    hprev[0:1, :] = h[tm - 1:tm, :]
    y_ref[0] = (h * _gelu_tanh(gate_in)).astype(BF16)


def _lru(xg, cw, cb, wa_bd, ba, wx_bd, bx, lam, *, tm):
    bsz, s, _ = xg.shape
    c = LRU_WIDTH
    vec = _resident((1, c))
    return pl.pallas_call(
        functools.partial(_lru_kernel, tm=tm),
        out_shape=jax.ShapeDtypeStruct((bsz, s, c), BF16),
        grid=(bsz, s // tm),
        in_specs=[
            pl.BlockSpec((1, tm, 2 * c), lambda b, i: (b, i, 0)),
            _resident((CONV_WIDTH, c)), vec,
            _resident((c, c)), vec,
            _resident((c, c)), vec,
            vec,
        ],
        out_specs=pl.BlockSpec((1, tm, c), lambda b, i: (b, i, 0)),
        scratch_shapes=[pltpu.VMEM((tm + 8, c), F32), pltpu.VMEM((8, c), F32)],
        compiler_params=_params(("parallel", "arbitrary")),
        name="rg_lru",
    )(xg, cw, cb.reshape(1, c), wa_bd, ba.reshape(1, c), wx_bd, bx.reshape(1, c), lam.reshape(1, c))


def _moba_kernel(q_ref, k_ref, v_ref, cent_ref, hot_ref, o_ref, lhs_ref, s_ref, p_ref, m_ref, l_ref, acc_ref):
    tq = MOBA_TQ
    rows = 2 * tq
    nb_pad = -(-(k_ref.shape[1] // MOBA_BLOCK) // 8) * 8
    i = pl.program_id(2)
    lane = lax.broadcasted_iota(jnp.int32, (tq, LANES), 1)
    bid = lax.broadcasted_iota(jnp.int32, (LANES, rows), 0)
    qidx = jnp.concatenate([lax.broadcasted_iota(jnp.int32, (LANES, tq), 1)] * 2, axis=1)
    own = (tq // MOBA_BLOCK) * i + qidx // MOBA_BLOCK
    past = bid < own
    qidx_r = jnp.concatenate([lax.broadcasted_iota(jnp.int32, (nb_pad, tq), 1)] * 2, axis=1)
    past_real = lax.broadcasted_iota(jnp.int32, (nb_pad, rows), 0) < (tq // MOBA_BLOCK) * i + qidx_r // MOBA_BLOCK
    chunks = [slice(c * LANES, (c + 1) * LANES) for c in range(MOBA_CHUNKS)]

    for c, cl in enumerate(chunks):
        q = q_ref[0, :, cl]
        zero = jnp.zeros_like(q)
        q2 = jnp.concatenate([jnp.where(lane < HEAD_DIM, q, zero), jnp.where(lane >= HEAD_DIM, q, zero)], axis=0)
        gate3 = lax.dot_general(_bf16_pieces(cent_ref[0, :, cl]), q2, NT_DIMS, preferred_element_type=F32)
        gate_t = gate3[:LANES] + gate3[LANES:2 * LANES] + gate3[2 * LANES:]
        sel_t = _pick_topk_rows(jnp.where(past_real, gate_t[:nb_pad], -jnp.inf), MOBA_TOPK)
        sel_t = jnp.concatenate([sel_t, jnp.zeros((LANES - nb_pad, rows), F32)], axis=0)
        drop = jnp.where((past & (sel_t > 0.5)) | (bid == own), 0.0, -BIG).T.astype(BF16)
        lhs_ref[c] = jnp.concatenate([q2, drop], axis=1)
        m_ref[c] = jnp.full((rows, LANES), NEG, F32)
        l_ref[c] = jnp.zeros((rows, LANES), F32)
        acc_ref[c] = jnp.zeros((rows, LANES), F32)

    def tile(kt, bias):
        k0 = pl.multiple_of(kt * tq, tq)
        for c, cl in enumerate(chunks):
            rhs = jnp.concatenate([k_ref[0, pl.ds(k0, tq), cl], hot_ref[pl.ds(k0, tq), :]], axis=1)
            s_ref[c] = lax.dot_general(lhs_ref[c], rhs, NT_DIMS, preferred_element_type=F32)
        for c, cl in enumerate(chunks):
            m_old = m_ref[c]
            m_new, p_sum = _stream_softmax(s_ref.at[c], p_ref.at[c], m_old, bias, tq)
            alpha = jnp.exp2(m_old - m_new)
            m_ref[c] = m_new
            l_ref[c] = alpha * l_ref[c] + p_sum
            acc_ref[c] = alpha * acc_ref[c] + jnp.dot(p_ref[c], v_ref[0, pl.ds(k0, tq), cl],
                                                      preferred_element_type=F32)

    @pl.when(i >= 0)
    def _():
        qpos = lax.broadcasted_iota(jnp.int32, (tq, tq), 0)
        kpos = lax.broadcasted_iota(jnp.int32, (tq, tq), 1)
        tile(i, jnp.where(kpos <= qpos, 0.0, NEG))

    @pl.loop(0, i)
    def _(kt):
        tile(kt, None)

    for c, cl in enumerate(chunks):
        out = acc_ref[c] / jnp.sum(l_ref[c], axis=1, keepdims=True)
        o_ref[0, :, cl] = jnp.where(lane < HEAD_DIM, out[:tq], out[tq:]).astype(BF16)


def _moba(q, k, v, cent):
    bsz, s, w = q.shape
    tq = MOBA_TQ
    nb = s // MOBA_BLOCK
    assert nb <= LANES and s % tq == 0
    cent = jnp.pad(cent, ((0, 0), (0, LANES - nb), (0, 0)))
    hot = _block_one_hot(s, MOBA_BLOCK)
    nc, cw = MOBA_CHUNKS, MOBA_CHUNKS * LANES
    return pl.pallas_call(
        _moba_kernel,
        out_shape=jax.ShapeDtypeStruct((bsz, s, w), BF16),
        grid=(bsz, w // cw, s // tq),
        in_specs=[
            pl.BlockSpec((1, tq, cw), lambda b, hp, i: (b, i, hp)),
            pl.BlockSpec((1, s, cw), lambda b, hp, i: (b, 0, hp), pipeline_mode=pl.Buffered(1)),
            pl.BlockSpec((1, s, cw), lambda b, hp, i: (b, 0, hp), pipeline_mode=pl.Buffered(1)),
            pl.BlockSpec((1, LANES, cw), lambda b, hp, i: (b, 0, hp)),
            _resident(hot.shape),
        ],
        out_specs=pl.BlockSpec((1, tq, cw), lambda b, hp, i: (b, i, hp)),
        scratch_shapes=[pltpu.VMEM((nc, 2 * tq, 2 * LANES), BF16), pltpu.VMEM((nc, 2 * tq, tq), F32),
                        pltpu.VMEM((nc, 2 * tq, tq), BF16)] + [pltpu.VMEM((nc, 2 * tq, LANES), F32)] * 3,
        compiler_params=_params(("parallel", "parallel", "parallel")),
        name="moba_attn",
    )(q, k, v, cent, hot)


def _inproj1_kernel(x_ref, mod_ref, g_ref, w_ref, cos_ref, sa_ref, sb_ref,
                    qn_ref, qr_ref, kvc_ref, kv4_ref, gate_ref):
    m = mod_ref[0]
    h = _norm_mod(x_ref[0], g_ref[...], m[3:4], m[4:5]).astype(BF16)
    u = jnp.dot(h, w_ref[...], preferred_element_type=F32)
    cos, sa, sb = cos_ref[...], sa_ref[...], sb_ref[...]
    q = u[:, :NSA_QD] * Q_PRESCALE
    qn_ref[0] = q.astype(BF16)
    qr_ref[0] = _rope(q, cos, sa, sb).astype(BF16)
    o = NSA_QD
    kvc_ref[0, 0] = u[:, o:o + NSA_KVD]
    kvc_ref[0, 1] = u[:, o + NSA_KVD:o + 2 * NSA_KVD]
    o += 2 * NSA_KVD
    ks = _rope(u[:, o:o + NSA_KVD], cos, sa, sb)
    vs = u[:, o + NSA_KVD:o + 2 * NSA_KVD]
    kw = _rope(u[:, o + 2 * NSA_KVD:o + 3 * NSA_KVD], cos, sa, sb)
    vw = u[:, o + 3 * NSA_KVD:o + 4 * NSA_KVD]
    kv4_ref[0] = jnp.concatenate([ks, vs, kw, vw], axis=1).astype(BF16)
    gate_ref[0] = jax.nn.sigmoid(u[:, o + 4 * NSA_KVD:])


def _inproj1(x, mod_l, g, w, rope_tabs, *, tm):
    bsz, s, d = x.shape
    n = w.shape[1]
    row = lambda b, i: (b, i, 0)
    tab = pl.BlockSpec((tm, LANES), lambda b, i: (i, 0))
    widths = (NSA_QD, NSA_QD, None, 4 * NSA_KVD, LANES)
    dtypes = (BF16, BF16, F32, BF16, F32)
    shape = lambda wd: (bsz, s, wd) if wd else (bsz, 2, s, NSA_KVD)
    spec = lambda wd: pl.BlockSpec((1, tm, wd), row) if wd else pl.BlockSpec((1, 2, tm, NSA_KVD),
                                                                             lambda b, i: (b, 0, i, 0))
    return pl.pallas_call(
        _inproj1_kernel,
        out_shape=tuple(jax.ShapeDtypeStruct(shape(wd), dt) for wd, dt in zip(widths, dtypes)),
        grid=(bsz, s // tm),
        in_specs=[
            pl.BlockSpec((1, tm, d), row),
            pl.BlockSpec((1, 9, d), lambda b, i: (b, 0, 0)),
            _resident((1, d)),
            _resident((d, n)),
            tab, tab, tab,
        ],
        out_specs=tuple(spec(wd) for wd in widths),
        compiler_params=_params(("parallel", "parallel")),
        name="inproj1",
    )(x, mod_l, g.reshape(1, d), w, *rope_tabs)


def _compress_kernel(kvc_ref, pos_ref, w1_ref, w2_ref, o_ref):
    nrow = o_ref.shape[2]
    for kv in range(2):
        top = jnp.zeros((nrow, w1_ref.shape[3]), F32)
        bot = jnp.zeros((nrow, w1_ref.shape[3]), F32)
        for r in range(CMP_STRIDE):
            x = kvc_ref[0, kv, pl.ds(r, nrow, stride=CMP_STRIDE), :]
            top += jnp.dot((x + pos_ref[kv, r:r + 1]).astype(BF16), w1_ref[kv, r], preferred_element_type=F32)
            bot += jnp.dot((x + pos_ref[kv, CMP_STRIDE + r:CMP_STRIDE + r + 1]).astype(BF16),
                           w1_ref[kv, CMP_STRIDE + r], preferred_element_type=F32)
        pre = top + pltpu.roll(bot, nrow - 1, axis=0)
        o_ref[0, kv] = jnp.dot(_gelu_tanh(pre).astype(BF16), w2_ref[kv], preferred_element_type=F32).astype(BF16)


def _compress(kvc, pos, w1, w2):
    bsz, _, s, _ = kvc.shape
    nrow = s // CMP_STRIDE
    eye = jnp.eye(NSA_KV_GROUPS, dtype=F32)
    pos2 = jnp.tile(pos, (1, 1, NSA_KV_GROUPS))
    hidden = w1.shape[-1]
    w1r = w1.reshape(2, CMP_LEN, HEAD_DIM, hidden)
    w1bd = jnp.einsum('gh,kric->krgihc', eye, w1r).reshape(2, CMP_LEN, NSA_KVD, NSA_KV_GROUPS * hidden)
    w2bd = jnp.einsum('gh,kcd->kgchd', eye, w2).reshape(2, NSA_KV_GROUPS * hidden, NSA_KVD)
    return pl.pallas_call(
        _compress_kernel,
        out_shape=jax.ShapeDtypeStruct((bsz, 2, nrow, NSA_KVD), BF16),
        grid=(bsz,),
        in_specs=[
            pl.BlockSpec((1, 2, s, NSA_KVD), lambda b: (b, 0, 0, 0)),
            _resident(pos2.shape),
            _resident(w1bd.shape),
            _resident(w2bd.shape),
        ],
        out_specs=pl.BlockSpec((1, 2, nrow, NSA_KVD), lambda b: (b, 0, 0, 0)),
        compiler_params=_params(("parallel",)),
        name="nsa_compress",
    )(kvc, pos2, w1bd.astype(BF16), w2bd.astype(BF16))


def _nsa_kernel(qn_ref, qr_ref, kv4_ref, cmp_ref, gate_ref, ovl_ref, hot_ref, gexp_ref, o_ref,
                lhs_ref, s_ref, p_ref, m_ref, acc_ref, *, seq, tk):
    tq = NSA_TQ
    nrow = cmp_ref.shape[2]
    ns = seq // SEL_BLOCK
    n_g, n_r = NSA_KV_GROUPS, NSA_HPG
    n_h = n_g * n_r
    rows = n_h * tq
    t0 = pl.program_id(1) * tq
    lane = lax.broadcasted_iota(jnp.int32, (tq, LANES), 1)
    gates = gate_ref[0]

    def stack(q_ref):
        parts = []
        for g in range(n_g):
            in_grp = (lane >= g * HEAD_DIM) & (lane < (g + 1) * HEAD_DIM)
            for r in range(n_r):
                ch = q_ref[0, :, r * LANES:(r + 1) * LANES]
                parts.append(jnp.where(in_grp, ch, jnp.zeros_like(ch)))
        return jnp.concatenate(parts, axis=0)

    half = rows // n_g
    neg_stats = jnp.full((rows, LANES), NEG, F32)

    def times_values(p_view, v, g, extra=None):
        vl = lax.broadcasted_iota(jnp.int32, v.shape, 1)
        vg = jnp.where((vl >= g * HEAD_DIM) & (vl < (g + 1) * HEAD_DIM), v, jnp.ones_like(v))
        if extra is not None:
            vg = jnp.concatenate([vg, extra], axis=1)
        return jnp.dot(p_view[g * half:(g + 1) * half, :], vg, preferred_element_type=F32)

    sc_ref, pc_ref = s_ref.at[:, 0:nrow], p_ref.at[:, 0:nrow]
    tq_c = t0 + lax.broadcasted_iota(jnp.int32, (tq, nrow), 0)
    cend = lax.broadcasted_iota(jnp.int32, (tq, nrow), 1) * CMP_STRIDE + (CMP_LEN - 1)
    sc_ref[...] = lax.dot_general(stack(qn_ref), cmp_ref[0, 0], NT_DIMS, preferred_element_type=F32)
    _stream_softmax(sc_ref, pc_ref, neg_stats, jnp.where(cend <= tq_c, 0.0, NEG), tq)
    both = jnp.concatenate([times_values(pc_ref, cmp_ref[0, 1], g, extra=ovl_ref[...]) for g in range(n_g)], axis=0)
    acc_c, imp_h = both[:, 0:LANES], both[:, LANES:]
    lane_r = lax.broadcasted_iota(jnp.int32, (rows, LANES), 1)
    grp_r = lax.broadcasted_iota(jnp.int32, (rows, LANES), 0) // half
    den_c = jnp.where(lane_r // HEAD_DIM == grp_r, pltpu.roll(acc_c, HEAD_DIM, axis=1), acc_c)
    has_block = jnp.where(t0 + lax.broadcasted_iota(jnp.int32, (tq, LANES), 0) >= CMP_LEN - 1, 1.0, 0.0)
    inv_c = jnp.concatenate([has_block] * n_h, axis=0) / den_c
    o_c = acc_c * inv_c

    imp_h = imp_h * inv_c
    imp_t = jnp.concatenate([functools.reduce(jnp.add, [imp_h[(g * n_r + r) * tq:(g * n_r + r + 1) * tq]
                                                        for r in range(n_r)]) for g in range(n_g)], axis=0).T
    jn = lax.broadcasted_iota(jnp.int32, (LANES, n_g * tq), 0)
    tpos = t0 + lax.broadcasted_iota(jnp.int32, (LANES, tq), 1)
    blk_t = jnp.concatenate([tpos] * n_g, axis=1) // SEL_BLOCK
    valid = jn <= blk_t
    forced = (jn == 0) | (jn == blk_t) | (jn == blk_t - 1)
    others = _pick_topk_rows(jnp.where(valid & jnp.logical_not(forced), imp_t, -jnp.inf), min(SEL_TOPK, ns) - 3)
    keep = valid & (forced | (others > 0.5))

    lhs_ref[:, 0:LANES] = stack(qr_ref)
    wk = WINDOW + tq
    sw_ref, pw_ref = s_ref.at[:, 0:wk], p_ref.at[:, 0:wk]
    kstart = pl.multiple_of(jnp.clip(t0 - WINDOW, 0, seq - wk), tq)
    diff = (t0 + lax.broadcasted_iota(jnp.int32, (tq, wk), 0)) - (
        kstart + lax.broadcasted_iota(jnp.int32, (tq, wk), 1))
    sw_ref[...] = lax.dot_general(lhs_ref[:, 0:LANES], kv4_ref[0, pl.ds(kstart, wk), 2 * LANES:3 * LANES], NT_DIMS,
                                  preferred_element_type=F32)
    _stream_softmax(sw_ref, pw_ref, neg_stats, jnp.where((diff >= 0) & (diff < WINDOW), 0.0, NEG), tq)
    acc_w = jnp.concatenate([times_values(pw_ref, kv4_ref[0, pl.ds(kstart, wk), 3 * LANES:4 * LANES], g)
                             for g in range(n_g)], axis=0)

    drop = jnp.where(keep, 0.0, -BIG).T.astype(BF16)
    lhs_ref[:, LANES:2 * LANES] = jnp.concatenate(
        [drop[g * tq:(g + 1) * tq] for g in range(n_g) for _ in range(n_r)], axis=0)

    m_ref[...] = neg_stats
    acc_ref[...] = jnp.zeros(acc_ref.shape, F32)

    def sel_tile(k0, width, bias):
        rhs = jnp.concatenate([kv4_ref[0, pl.ds(k0, width), 0:LANES], hot_ref[pl.ds(k0, width), :]], axis=1)
        views = []
        for g in range(n_g):
            gs = slice(g * half, (g + 1) * half)
            sg_ref = s_ref.at[gs, 0:width]
            sg_ref[...] = lax.dot_general(lhs_ref[gs, :], rhs, NT_DIMS, preferred_element_type=F32)
            views.append((gs, sg_ref, p_ref.at[gs, 0:width]))
        for g, (gs, sg_ref, pg_ref) in enumerate(views):
            m_old = m_ref[gs, :]
            m_new, _ = _stream_softmax(sg_ref, pg_ref, m_old, bias, tq)
            m_ref[gs, :] = m_new
            vl = lax.broadcasted_iota(jnp.int32, (width, LANES), 1)
            v = kv4_ref[0, pl.ds(k0, width), LANES:2 * LANES]
            vg = jnp.where((vl >= g * HEAD_DIM) & (vl < (g + 1) * HEAD_DIM), v, jnp.ones_like(v))
            acc_ref[gs, :] = jnp.exp2(m_old - m_new) * acc_ref[gs, :] + jnp.dot(pg_ref[...], vg,
                                                                                preferred_element_type=F32)

    last = t0 // tk

    @pl.loop(0, last)
    def _(kt):
        sel_tile(pl.multiple_of(kt * tk, tk), tk, None)

    k_last = pl.multiple_of(last * tk, tk)

    def last_tile(width):
        kpos = k_last + lax.broadcasted_iota(jnp.int32, (tq, width), 1)
        tq_s = t0 + lax.broadcasted_iota(jnp.int32, (tq, width), 0)
        sel_tile(k_last, width, jnp.where(kpos <= tq_s, 0.0, NEG))

    in_first_half = t0 - k_last + tq <= tk // 2
    pl.when(in_first_half)(lambda: last_tile(tk // 2))
    pl.when(jnp.logical_not(in_first_half))(lambda: last_tile(tk))

    acc_s = acc_ref[...]

    gates_b = gates.astype(BF16)
    spread = [jnp.dot(gates_b, gexp_ref[c], preferred_element_type=F32) for c in range(3)]
    first_half = lane < HEAD_DIM
    for r in range(n_r):
        rows_g = [slice((g * n_r + r) * tq, (g * n_r + r + 1) * tq) for g in range(n_g)]
        y = spread[0][:, r * LANES:(r + 1) * LANES] * jnp.where(first_half, o_c[rows_g[0]], o_c[rows_g[1]])
        for c, acc in ((1, acc_s), (2, acc_w)):
            vals = jnp.where(first_half, acc[rows_g[0]], acc[rows_g[1]])
            dens = pltpu.roll(jnp.where(first_half, acc[rows_g[1]], acc[rows_g[0]]), HEAD_DIM, axis=1)
            y = y + spread[c][:, r * LANES:(r + 1) * LANES] * (vals / dens)
        o_ref[0, :, r * LANES:(r + 1) * LANES] = y.astype(BF16)


def _nsa(qn, qr, kv4, cmp, gates, *, tk):
    bsz, s, _ = qn.shape
    tq = NSA_TQ
    ns = s // SEL_BLOCK
    assert ns <= LANES
    assert min(SEL_TOPK, ns) >= 3
    ovl = jnp.pad(_overlap(cmp.shape[2], ns), ((0, 0), (0, LANES - ns))).astype(BF16)
    hot = _block_one_hot(s, SEL_BLOCK)
    head_of_p = _nsa_head_perm() // HEAD_DIM
    gexp = (jnp.arange(LANES)[None, :, None] == 3 * head_of_p[None, None, :] + jnp.arange(3)[:, None, None]).astype(BF16)
    rows = NSA_HEADS * tq
    width = max(cmp.shape[2], WINDOW + tq, tk)
    row = lambda b, i: (b, i, 0)
    return pl.pallas_call(
        functools.partial(_nsa_kernel, seq=s, tk=tk),
        out_shape=jax.ShapeDtypeStruct((bsz, s, NSA_QD), BF16),
        grid=(bsz, s // tq),
        in_specs=[
            pl.BlockSpec((1, tq, NSA_QD), row),
            pl.BlockSpec((1, tq, NSA_QD), row),
            pl.BlockSpec((1, s, 4 * NSA_KVD), lambda b, i: (b, 0, 0), pipeline_mode=pl.Buffered(1)),
            pl.BlockSpec((1, 2) + cmp.shape[2:], lambda b, i: (b, 0, 0, 0), pipeline_mode=pl.Buffered(1)),
            pl.BlockSpec((1, tq, LANES), row),
            _resident(ovl.shape),
            _resident(hot.shape),
            _resident(gexp.shape),
        ],
        out_specs=pl.BlockSpec((1, tq, NSA_QD), row),
        scratch_shapes=[pltpu.VMEM((rows, 2 * LANES), BF16), pltpu.VMEM((rows, width), F32),
                        pltpu.VMEM((rows, width), BF16),
                        pltpu.VMEM((rows, LANES), F32), pltpu.VMEM((rows, LANES), F32)],
        compiler_params=_params(("parallel", "parallel")),
        name="nsa_attn",
    )(qn, qr, kv4, cmp, gates, ovl, hot, gexp)


def _rope_tables(s):
    half = ROT_DIM // 2
    pos = jnp.arange(s, dtype=F32)
    freqs = ROPE_THETA ** (-jnp.arange(half, dtype=F32) * 2.0 / ROT_DIM)
    ang = pos[:, None] * freqs[None, :]
    cos, sin = jnp.cos(ang), jnp.sin(ang)
    ones = jnp.ones((s, HEAD_DIM - ROT_DIM), F32)
    zeros = jnp.zeros((s, HEAD_DIM - ROT_DIM), F32)
    zh = jnp.zeros((s, half), F32)
    cos_t = jnp.concatenate([cos, cos, ones], axis=1)
    sa_t = jnp.concatenate([-sin, zh, zeros], axis=1)
    sb_t = jnp.concatenate([zh, sin, zeros], axis=1)
    return tuple(jnp.tile(t, (1, LANES // HEAD_DIM)) for t in (cos_t, sa_t, sb_t))


def _block_one_hot(s, block):
    return (jnp.arange(s)[:, None] // block == jnp.arange(LANES)[None, :]).astype(BF16)


def _block_diag(w):
    n, a, b = w.shape
    eye = jnp.eye(n, dtype=w.dtype)
    return (eye[:, None, :, None] * w[:, :, None, :]).reshape(n * a, n * b)


def _nsa_head_perm():
    r = jnp.arange(NSA_HPG)[:, None, None]
    g = jnp.arange(NSA_KV_GROUPS)[None, :, None]
    dd = jnp.arange(HEAD_DIM)[None, None, :]
    return (HEAD_DIM * (NSA_HPG * g + r) + dd).reshape(-1)


def _overlap(nrow, ns):
    ci = jnp.arange(nrow)[:, None] * CMP_STRIDE
    sj = jnp.arange(ns)[None, :] * SEL_BLOCK
    ov = jnp.clip(jnp.minimum(ci + CMP_LEN, sj + SEL_BLOCK) - jnp.maximum(ci, sj), 0, None)
    return ov.astype(F32) / CMP_LEN


def kernel(x, c, mod_w, mod_b, norm_g, ffn_w1, ffn_w2, mix0_in_w, lru_conv_w, lru_conv_b, lru_wa, lru_ba,
           lru_wx, lru_bx, lru_lambda, mix0_out_w, mix1_in_w, cmp_pos, cmp_w1, cmp_w2, mix1_out_w, final_norm_g):
    bsz, s, d = x.shape
    depth = mod_w.shape[0]
    assert d == D_MODEL and s % 512 == 0 and s >= WINDOW + NSA_TQ
    tm = 512

    mod = _modulation(c, mod_w, mod_b)
    tabs = _rope_tables(s)
    w1 = ffn_w1.astype(BF16)
    w2 = ffn_w2.astype(BF16)

    for l in range(depth):
        j = l // 2
        x = _ffn(x, mod[l], norm_g[l, 0], w1, w2, (l, 0), final_norm_g, row0=0, final=False, tm=tm)
        if l % 2 == 0:
            xg, q, k, v, cent = _inproj0(x, mod[l], norm_g[l, 1], mix0_in_w[j].astype(BF16), tabs)
            y_lru = _lru(xg, lru_conv_w[j], lru_conv_b[j],
                         _block_diag(lru_wa[j]).astype(BF16), lru_ba[j].reshape(-1),
                         _block_diag(lru_wx[j]).astype(BF16), lru_bx[j].reshape(-1), lru_lambda[j], tm=256)
            y_att = _moba(q, k, v, cent.reshape(bsz, -1, MOBA_WIDTH))
            wo = mix0_out_w[j].astype(BF16)
            mix = ((y_lru, wo[:LRU_WIDTH]), (y_att, wo[LRU_WIDTH:]))
        else:
            perm = _nsa_head_perm()
            w_in = mix1_in_w[j]
            w_in = jnp.concatenate([w_in[:, :NSA_QD][:, perm], w_in[:, NSA_QD:]], axis=1)
            w_in = jnp.pad(w_in, ((0, 0), (0, IN1_PAD - w_in.shape[1]))).astype(BF16)
            qn, qr, kvc, kv4, gates = _inproj1(x, mod[l], norm_g[l, 1], w_in, tabs, tm=tm)
            cmp = _compress(kvc, cmp_pos[j], cmp_w1[j], cmp_w2[j])
            y = _nsa(qn, qr, kv4, cmp, gates, tk=512)
            mix = ((y, mix1_out_w[j][perm].astype(BF16)),)
        x = _ffn(x, mod[l], norm_g[l, 2], w1, w2, (l, 1), final_norm_g,
                 row0=6, final=(l == depth - 1), tm=tm, mix=mix)
    return x
```

```python
import functools
import math

import jax
import jax.numpy as jnp
from jax import lax
from jax.experimental import pallas as pl
from jax.experimental.pallas import tpu as pltpu

F32 = jnp.float32
BF16 = jnp.bfloat16

D_MODEL = 1024
EPS = 1e-6
ROPE_THETA = 500000.0
HEAD_DIM = 64
ROT_DIM = HEAD_DIM // 4
NEG = -1e30
BIG = 1e30
LANES = 128

LRU_WIDTH = D_MODEL // 2
LRU_BLOCKS = 8
CONV_WIDTH = 4
LRU_C = 8.0
MOBA_WIDTH = D_MODEL // 2
MOBA_BLOCK = 256
MOBA_TOPK = 3
MOBA_TQ = 2 * MOBA_BLOCK
MOBA_CHUNKS = 4

NSA_HEADS = D_MODEL // HEAD_DIM
NSA_KV_GROUPS = 2
NSA_HPG = NSA_HEADS // NSA_KV_GROUPS
CMP_LEN = 32
CMP_STRIDE = 16
SEL_BLOCK = 64
SEL_TOPK = 16
WINDOW = 512
NSA_QD = NSA_HEADS * HEAD_DIM
NSA_KVD = NSA_KV_GROUPS * HEAD_DIM
IN1_PAD = NSA_QD + 6 * NSA_KVD + LANES
NSA_TQ = 128

VMEM_LIMIT = 52 * 1024 * 1024

NT_DIMS = (((1,), (1,)), ((), ()))
Q_PRESCALE = HEAD_DIM ** -0.5 * math.log2(math.e)


def _params(sem):
    return pltpu.CompilerParams(dimension_semantics=sem, vmem_limit_bytes=VMEM_LIMIT)


def _resident(shape):
    nd = len(shape)
    return pl.BlockSpec(shape, lambda *_: (0,) * nd, pipeline_mode=pl.Buffered(1))


def _sigmoid(z):
    return 0.5 * jnp.tanh(0.5 * z) + 0.5


def _silu(a):
    return a * jax.nn.sigmoid(a)


def _gelu_tanh(x):
    cdf = 0.5 * (1.0 + jnp.tanh(math.sqrt(2.0 / math.pi) * (x + 0.044715 * (x * x * x))))
    return x * cdf


def _rmsnorm(x, g):
    return x * lax.rsqrt(jnp.mean(x * x, axis=-1, keepdims=True) + EPS) * g


def _norm_mod(x, g, shift, scale):
    return _rmsnorm(x, g) * (1.0 + scale) + shift


def _rope(u, cos, sa, sb):
    outs = []
    for j in range(u.shape[1] // LANES):
        c = u[:, j * LANES:(j + 1) * LANES]
        up = pltpu.roll(c, LANES - ROT_DIM // 2, axis=1)
        dn = pltpu.roll(c, ROT_DIM // 2, axis=1)
        outs.append(c * cos + up * sa + dn * sb)
    return jnp.concatenate(outs, axis=1)


def _bf16_pieces(x):
    hi = x.astype(BF16)
    r1 = x - hi.astype(F32)
    mid = r1.astype(BF16)
    lo = (r1 - mid.astype(F32)).astype(BF16)
    return jnp.concatenate([hi, mid, lo], axis=0)


SOFTMAX_CHUNK = 64


def _stream_softmax(s_ref, p_ref, m_old, bias, tq):
    rows, n = s_ref.shape
    n_chunks, n_l = rows // SOFTMAX_CHUNK, n // LANES

    def chunk(c):
        sl = slice(c * SOFTMAX_CHUNK, (c + 1) * SOFTMAX_CHUNK)
        sc = s_ref[sl, :]
        if bias is not None:
            q0 = (c * SOFTMAX_CHUNK) % tq
            sc = sc + bias[q0:q0 + SOFTMAX_CHUNK]
        return sl, [sc[:, j * LANES:(j + 1) * LANES] for j in range(n_l)]

    partial_max = jnp.concatenate([functools.reduce(jnp.maximum, chunk(c)[1]) for c in range(n_chunks)], axis=0)
    m_new = jnp.maximum(m_old, jnp.broadcast_to(jnp.max(partial_max, axis=1, keepdims=True), (rows, LANES)))
    sums = []
    for c in range(n_chunks):
        sl, cols = chunk(c)
        m_c = m_new[sl]
        ps = [jnp.exp2(col - m_c) for col in cols]
        for j, pj in enumerate(ps):
            p_ref[sl, j * LANES:(j + 1) * LANES] = pj.astype(BF16)
        sums.append(functools.reduce(jnp.add, ps))
    return m_new, jnp.concatenate(sums, axis=0)


def _pick_topk_rows(g, k):
    n_ids = g.shape[0]
    ids = lax.broadcasted_iota(jnp.int32, g.shape, 0).astype(F32)
    picked = jnp.zeros(g.shape, F32)
    for _ in range(k):
        mx = jnp.max(g, axis=0, keepdims=True)
        first = jnp.min(jnp.where(g == mx, ids, float(n_ids)), axis=0, keepdims=True)
        hit = ids == first
        picked = jnp.where(hit, 1.0, picked)
        g = jnp.where(hit, -jnp.inf, g)
    return picked


def _mod_kernel(c_ref, w_ref, b_ref, o_ref):
    cond = _silu(c_ref[...])
    o_ref[0] = jnp.dot(cond.astype(BF16), w_ref[0].astype(BF16), preferred_element_type=F32) + b_ref[0]


def _modulation(c, mod_w, mod_b):
    depth, d, n = mod_w.shape
    bsz = c.shape[0]
    bp = -(-bsz // 8) * 8
    cp = jnp.pad(c, ((0, bp - bsz), (0, 0)))
    tn = 1152
    out = pl.pallas_call(
        _mod_kernel,
        out_shape=jax.ShapeDtypeStruct((depth, bp, n), F32),
        grid=(depth, n // tn),
        in_specs=[
            pl.BlockSpec((bp, d), lambda l, j: (0, 0)),
            pl.BlockSpec((1, d, tn), lambda l, j: (l, 0, j)),
            pl.BlockSpec((1, 1, tn), lambda l, j: (l, 0, j)),
        ],
        out_specs=pl.BlockSpec((1, bp, tn), lambda l, j: (l, 0, j)),
        compiler_params=_params(("parallel", "parallel")),
        name="adaln_mod",
    )(cp, mod_w, mod_b.reshape(depth, 1, n))
    return out[:, :bsz].reshape(depth, bsz, 9, d)


def _ffn_kernel(x_ref, mod_ref, g_ref, w1_ref, w2_ref, gf_ref, *rest, row0, final, n_mix):
    o_ref = rest[-1]
    x = x_ref[0]
    m = mod_ref[0]
    if n_mix:
        mixed = sum(jnp.dot(y_ref[0], w_ref[...], preferred_element_type=F32)
                    for y_ref, w_ref in zip(rest[:n_mix], rest[n_mix:2 * n_mix]))
        x = x + m[5:6] * mixed
    h = _norm_mod(x, g_ref[...], m[row0:row0 + 1], m[row0 + 1:row0 + 2]).astype(BF16)
    d_ff = w2_ref.shape[2]
    a = jnp.dot(h, w1_ref[0, 0, :, 0:d_ff], preferred_element_type=F32)
    b = jnp.dot(h, w1_ref[0, 0, :, d_ff:2 * d_ff], preferred_element_type=F32)
    act = (_silu(a) * b).astype(BF16)
    y = jnp.dot(act, w2_ref[0, 0], preferred_element_type=F32)
    out = x + (0.5 * m[row0 + 2:row0 + 3]) * y
    if final:
        out = _rmsnorm(out, gf_ref[...])
    o_ref[0] = out


def _ffn(x, mod_l, g, w1, w2, which, gf, *, row0, final, tm, mix=()):
    bsz, s, d = x.shape
    picked = lambda shape: pl.BlockSpec((1, 1) + shape[2:], lambda b, i: which + (0, 0),
                                        pipeline_mode=pl.Buffered(1))
    kern = functools.partial(_ffn_kernel, row0=row0, final=final, n_mix=len(mix))
    row = lambda b, i: (b, i, 0)
    return pl.pallas_call(
        kern,
        out_shape=jax.ShapeDtypeStruct(x.shape, F32),
        grid=(bsz, s // tm),
        in_specs=[
            pl.BlockSpec((1, tm, d), row),
            pl.BlockSpec((1, 9, d), lambda b, i: (b, 0, 0)),
            _resident((1, d)),
            picked(w1.shape),
            picked(w2.shape),
            _resident((1, d)),
        ] + [pl.BlockSpec((1, tm, y.shape[2]), row) for y, _ in mix] + [_resident(w.shape) for _, w in mix],
        out_specs=pl.BlockSpec((1, tm, d), row),
        compiler_params=_params(("parallel", "parallel")),
        name="ffn_final" if final else ("mix_ffn" if mix else "ffn"),
    )(x, mod_l, g.reshape(1, d), w1, w2, gf.reshape(1, d), *[y for y, _ in mix], *[w for _, w in mix])


def _inproj0_kernel(x_ref, mod_ref, g_ref, w_ref, cos_ref, sa_ref, sb_ref,
                    xg_ref, q_ref, k_ref, v_ref, cent_ref):
    m = mod_ref[0]
    h = _norm_mod(x_ref[0], g_ref[...], m[3:4], m[4:5]).astype(BF16)
    u = jnp.dot(h, w_ref[...], preferred_element_type=F32)
    o1 = 2 * LRU_WIDTH
    xg_ref[0] = u[:, :o1]
    cos, sa, sb = cos_ref[...], sa_ref[...], sb_ref[...]
    q = _rope(u[:, o1:o1 + MOBA_WIDTH], cos, sa, sb)
    k = _rope(u[:, o1 + MOBA_WIDTH:o1 + 2 * MOBA_WIDTH], cos, sa, sb)
    q_ref[0] = (q * Q_PRESCALE).astype(BF16)
    k_ref[0] = k.astype(BF16)
    v_ref[0] = u[:, o1 + 2 * MOBA_WIDTH:].astype(BF16)
    for j in range(k.shape[0] // MOBA_BLOCK):
        cent_ref[0, j] = jnp.mean(k[j * MOBA_BLOCK:(j + 1) * MOBA_BLOCK], axis=0, keepdims=True)


def _inproj0(x, mod_l, g, w, rope_tabs, *, tm):
    bsz, s, d = x.shape
    assert tm % MOBA_BLOCK == 0
    nb = s // MOBA_BLOCK
    n = w.shape[1]
    row = lambda b, i: (b, i, 0)
    tab = pl.BlockSpec((tm, LANES), lambda b, i: (i, 0))
    return pl.pallas_call(
        _inproj0_kernel,
        out_shape=(
            jax.ShapeDtypeStruct((bsz, s, 2 * LRU_WIDTH), F32),
            jax.ShapeDtypeStruct((bsz, s, MOBA_WIDTH), BF16),
            jax.ShapeDtypeStruct((bsz, s, MOBA_WIDTH), BF16),
            jax.ShapeDtypeStruct((bsz, s, MOBA_WIDTH), BF16),
            jax.ShapeDtypeStruct((bsz, nb, 1, MOBA_WIDTH), F32),
        ),
        grid=(bsz, s // tm),
        in_specs=[
            pl.BlockSpec((1, tm, d), row),
            pl.BlockSpec((1, 9, d), lambda b, i: (b, 0, 0)),
            _resident((1, d)),
            _resident((d, n)),
            tab, tab, tab,
        ],
        out_specs=(
            pl.BlockSpec((1, tm, 2 * LRU_WIDTH), row),
            pl.BlockSpec((1, tm, MOBA_WIDTH), row),
            pl.BlockSpec((1, tm, MOBA_WIDTH), row),
            pl.BlockSpec((1, tm, MOBA_WIDTH), row),
            pl.BlockSpec((1, tm // MOBA_BLOCK, 1, MOBA_WIDTH), lambda b, i: (b, i, 0, 0)),
        ),
        compiler_params=_params(("parallel", "parallel")),
        name="inproj0",
    )(x, mod_l, g.reshape(1, d), w, *rope_tabs)


def _lru_kernel(xg_ref, cw_ref, cb_ref, wa_ref, ba_ref, wx_ref, bx_ref, lam_ref, y_ref, xbuf, hprev, *, tm):
    c = LRU_WIDTH
    halo = 8

    @pl.when(pl.program_id(1) == 0)
    def _():
        xbuf[0:halo, :] = jnp.zeros((halo, c), F32)
        hprev[...] = jnp.zeros(hprev.shape, F32)

    x = xg_ref[0, :, 0:c]
    gate_in = xg_ref[0, :, c:2 * c]
    xbuf[halo:halo + tm, :] = x
    cw = cw_ref[...]
    xc = cb_ref[...] + cw[CONV_WIDTH - 1:CONV_WIDTH] * x
    for kk in range(CONV_WIDTH - 1):
        back = CONV_WIDTH - 1 - kk
        xc = xc + cw[kk:kk + 1] * xbuf[halo - back:halo - back + tm, :]
    xbuf[0:halo, :] = x[tm - halo:tm, :]

    xcb = xc.astype(BF16)
    r = _sigmoid(jnp.dot(xcb, wa_ref[...], preferred_element_type=F32) + ba_ref[...])
    ig = _sigmoid(jnp.dot(xcb, wx_ref[...], preferred_element_type=F32) + bx_ref[...])
    nlam = -lam_ref[...]
    softplus = jnp.maximum(nlam, 0.0) + jnp.log1p(jnp.exp(-jnp.abs(nlam)))
    log_a = (-LRU_C * r) * softplus
    a = jnp.exp(log_a)
    gain2 = -jnp.tanh(log_a) * (a * a + 1.0)
    mult = jnp.where(gain2 > 0.0, gain2 * lax.rsqrt(gain2), 0.0)
    bt = mult * (ig * xc)

    row = lax.broadcasted_iota(jnp.int32, (tm, c), 0)
    step = 1
    while step < tm:
        a_sh = pltpu.roll(a, step, axis=0)
        b_sh = pltpu.roll(bt, step, axis=0)
        valid = row >= step
        bt = jnp.where(valid, a * b_sh + bt, bt)
        a = jnp.where(valid, a * a_sh, a)
        step *= 2
    h = bt + a * hprev[0:1, :]
    hprev[0:1, :] = h[tm - 1:tm, :]
    y_ref[0] = (h * _gelu_tanh(gate_in)).astype(BF16)


def _lru(xg, cw, cb, wa_bd, ba, wx_bd, bx, lam, *, tm):
    bsz, s, _ = xg.shape
    c = LRU_WIDTH
    vec = _resident((1, c))
    return pl.pallas_call(
        functools.partial(_lru_kernel, tm=tm),
        out_shape=jax.ShapeDtypeStruct((bsz, s, c), BF16),
        grid=(bsz, s // tm),
        in_specs=[
            pl.BlockSpec((1, tm, 2 * c), lambda b, i: (b, i, 0)),
            _resident((CONV_WIDTH, c)), vec,
            _resident((c, c)), vec,
            _resident((c, c)), vec,
            vec,
        ],
        out_specs=pl.BlockSpec((1, tm, c), lambda b, i: (b, i, 0)),
        scratch_shapes=[pltpu.VMEM((tm + 8, c), F32), pltpu.VMEM((8, c), F32)],
        compiler_params=_params(("parallel", "arbitrary")),
        name="rg_lru",
    )(xg, cw, cb.reshape(1, c), wa_bd, ba.reshape(1, c), wx_bd, bx.reshape(1, c), lam.reshape(1, c))


def _moba_kernel(q_ref, k_ref, v_ref, cent_ref, hot_ref, o_ref, lhs_ref, s_ref, p_ref, m_ref, l_ref, acc_ref):
    tq = MOBA_TQ
    rows = 2 * tq
    nb_pad = -(-(k_ref.shape[1] // MOBA_BLOCK) // 8) * 8
    i = pl.program_id(2)
    lane = lax.broadcasted_iota(jnp.int32, (tq, LANES), 1)
    bid = lax.broadcasted_iota(jnp.int32, (LANES, rows), 0)
    qidx = jnp.concatenate([lax.broadcasted_iota(jnp.int32, (LANES, tq), 1)] * 2, axis=1)
    own = (tq // MOBA_BLOCK) * i + qidx // MOBA_BLOCK
    past = bid < own
    qidx_r = jnp.concatenate([lax.broadcasted_iota(jnp.int32, (nb_pad, tq), 1)] * 2, axis=1)
    past_real = lax.broadcasted_iota(jnp.int32, (nb_pad, rows), 0) < (tq // MOBA_BLOCK) * i + qidx_r // MOBA_BLOCK
    chunks = [slice(c * LANES, (c + 1) * LANES) for c in range(MOBA_CHUNKS)]

    for c, cl in enumerate(chunks):
        q = q_ref[0, :, cl]
        zero = jnp.zeros_like(q)
        q2 = jnp.concatenate([jnp.where(lane < HEAD_DIM, q, zero), jnp.where(lane >= HEAD_DIM, q, zero)], axis=0)
        gate3 = lax.dot_general(_bf16_pieces(cent_ref[0, :, cl]), q2, NT_DIMS, preferred_element_type=F32)
        gate_t = gate3[:LANES] + gate3[LANES:2 * LANES] + gate3[2 * LANES:]
        sel_t = _pick_topk_rows(jnp.where(past_real, gate_t[:nb_pad], -jnp.inf), MOBA_TOPK)
        sel_t = jnp.concatenate([sel_t, jnp.zeros((LANES - nb_pad, rows), F32)], axis=0)
        drop = jnp.where((past & (sel_t > 0.5)) | (bid == own), 0.0, -BIG).T.astype(BF16)
        lhs_ref[c] = jnp.concatenate([q2, drop], axis=1)
        m_ref[c] = jnp.full((rows, LANES), NEG, F32)
        l_ref[c] = jnp.zeros((rows, LANES), F32)
        acc_ref[c] = jnp.zeros((rows, LANES), F32)

    def tile(kt, bias):
        k0 = pl.multiple_of(kt * tq, tq)
        for c, cl in enumerate(chunks):
            rhs = jnp.concatenate([k_ref[0, pl.ds(k0, tq), cl], hot_ref[pl.ds(k0, tq), :]], axis=1)
            s_ref[c] = lax.dot_general(lhs_ref[c], rhs, NT_DIMS, preferred_element_type=F32)
        for c, cl in enumerate(chunks):
            m_old = m_ref[c]
            m_new, p_sum = _stream_softmax(s_ref.at[c], p_ref.at[c], m_old, bias, tq)
            alpha = jnp.exp2(m_old - m_new)
            m_ref[c] = m_new
            l_ref[c] = alpha * l_ref[c] + p_sum
            acc_ref[c] = alpha * acc_ref[c] + jnp.dot(p_ref[c], v_ref[0, pl.ds(k0, tq), cl],
                                                      preferred_element_type=F32)

    @pl.when(i >= 0)
    def _():
        qpos = lax.broadcasted_iota(jnp.int32, (tq, tq), 0)
        kpos = lax.broadcasted_iota(jnp.int32, (tq, tq), 1)
        tile(i, jnp.where(kpos <= qpos, 0.0, NEG))

    @pl.loop(0, i)
    def _(kt):
        tile(kt, None)

    for c, cl in enumerate(chunks):
        out = acc_ref[c] / jnp.sum(l_ref[c], axis=1, keepdims=True)
        o_ref[0, :, cl] = jnp.where(lane < HEAD_DIM, out[:tq], out[tq:]).astype(BF16)


def _moba(q, k, v, cent):
    bsz, s, w = q.shape
    tq = MOBA_TQ
    nb = s // MOBA_BLOCK
    assert nb <= LANES and s % tq == 0
    cent = jnp.pad(cent, ((0, 0), (0, LANES - nb), (0, 0)))
    hot = _block_one_hot(s, MOBA_BLOCK)
    nc, cw = MOBA_CHUNKS, MOBA_CHUNKS * LANES
    return pl.pallas_call(
        _moba_kernel,
        out_shape=jax.ShapeDtypeStruct((bsz, s, w), BF16),
        grid=(bsz, w // cw, s // tq),
        in_specs=[
            pl.BlockSpec((1, tq, cw), lambda b, hp, i: (b, i, hp)),
            pl.BlockSpec((1, s, cw), lambda b, hp, i: (b, 0, hp), pipeline_mode=pl.Buffered(1)),
            pl.BlockSpec((1, s, cw), lambda b, hp, i: (b, 0, hp), pipeline_mode=pl.Buffered(1)),
            pl.BlockSpec((1, LANES, cw), lambda b, hp, i: (b, 0, hp)),
            _resident(hot.shape),
        ],
        out_specs=pl.BlockSpec((1, tq, cw), lambda b, hp, i: (b, i, hp)),
        scratch_shapes=[pltpu.VMEM((nc, 2 * tq, 2 * LANES), BF16), pltpu.VMEM((nc, 2 * tq, tq), F32),
                        pltpu.VMEM((nc, 2 * tq, tq), BF16)] + [pltpu.VMEM((nc, 2 * tq, LANES), F32)] * 3,
        compiler_params=_params(("parallel", "parallel", "parallel")),
        name="moba_attn",
    )(q, k, v, cent, hot)


def _inproj1_kernel(x_ref, mod_ref, g_ref, w_ref, cos_ref, sa_ref, sb_ref,
                    qn_ref, qr_ref, kvc_ref, kv4_ref, gate_ref):
    m = mod_ref[0]
    h = _norm_mod(x_ref[0], g_ref[...], m[3:4], m[4:5]).astype(BF16)
    u = jnp.dot(h, w_ref[...], preferred_element_type=F32)
    cos, sa, sb = cos_ref[...], sa_ref[...], sb_ref[...]
    q = u[:, :NSA_QD] * Q_PRESCALE
    qn_ref[0] = q.astype(BF16)
    qr_ref[0] = _rope(q, cos, sa, sb).astype(BF16)
    o = NSA_QD
    kvc_ref[0, 0] = u[:, o:o + NSA_KVD]
    kvc_ref[0, 1] = u[:, o + NSA_KVD:o + 2 * NSA_KVD]
    o += 2 * NSA_KVD
    ks = _rope(u[:, o:o + NSA_KVD], cos, sa, sb)
    vs = u[:, o + NSA_KVD:o + 2 * NSA_KVD]
    kw = _rope(u[:, o + 2 * NSA_KVD:o + 3 * NSA_KVD], cos, sa, sb)
    vw = u[:, o + 3 * NSA_KVD:o + 4 * NSA_KVD]
    kv4_ref[0] = jnp.concatenate([ks, vs, kw, vw], axis=1).astype(BF16)
    gate_ref[0] = jax.nn.sigmoid(u[:, o + 4 * NSA_KVD:])


def _inproj1(x, mod_l, g, w, rope_tabs, *, tm):
    bsz, s, d = x.shape
    n = w.shape[1]
    row = lambda b, i: (b, i, 0)
    tab = pl.BlockSpec((tm, LANES), lambda b, i: (i, 0))
    widths = (NSA_QD, NSA_QD, None, 4 * NSA_KVD, LANES)
    dtypes = (BF16, BF16, F32, BF16, F32)
    shape = lambda wd: (bsz, s, wd) if wd else (bsz, 2, s, NSA_KVD)
    spec = lambda wd: pl.BlockSpec((1, tm, wd), row) if wd else pl.BlockSpec((1, 2, tm, NSA_KVD),
                                                                             lambda b, i: (b, 0, i, 0))
    return pl.pallas_call(
        _inproj1_kernel,
        out_shape=tuple(jax.ShapeDtypeStruct(shape(wd), dt) for wd, dt in zip(widths, dtypes)),
        grid=(bsz, s // tm),
        in_specs=[
            pl.BlockSpec((1, tm, d), row),
            pl.BlockSpec((1, 9, d), lambda b, i: (b, 0, 0)),
            _resident((1, d)),
            _resident((d, n)),
            tab, tab, tab,
        ],
        out_specs=tuple(spec(wd) for wd in widths),
        compiler_params=_params(("parallel", "parallel")),
        name="inproj1",
    )(x, mod_l, g.reshape(1, d), w, *rope_tabs)


def _compress_kernel(kvc_ref, pos_ref, w1_ref, w2_ref, o_ref):
    nrow = o_ref.shape[2]
    for kv in range(2):
        top = jnp.zeros((nrow, w1_ref.shape[3]), F32)
        bot = jnp.zeros((nrow, w1_ref.shape[3]), F32)
        for r in range(CMP_STRIDE):
            x = kvc_ref[0, kv, pl.ds(r, nrow, stride=CMP_STRIDE), :]
            top += jnp.dot((x + pos_ref[kv, r:r + 1]).astype(BF16), w1_ref[kv, r], preferred_element_type=F32)
            bot += jnp.dot((x + pos_ref[kv, CMP_STRIDE + r:CMP_STRIDE + r + 1]).astype(BF16),
                           w1_ref[kv, CMP_STRIDE + r], preferred_element_type=F32)
        pre = top + pltpu.roll(bot, nrow - 1, axis=0)
        o_ref[0, kv] = jnp.dot(_gelu_tanh(pre).astype(BF16), w2_ref[kv], preferred_element_type=F32).astype(BF16)


def _compress(kvc, pos, w1, w2):
    bsz, _, s, _ = kvc.shape
    nrow = s // CMP_STRIDE
    eye = jnp.eye(NSA_KV_GROUPS, dtype=F32)
    pos2 = jnp.tile(pos, (1, 1, NSA_KV_GROUPS))
    hidden = w1.shape[-1]
    w1r = w1.reshape(2, CMP_LEN, HEAD_DIM, hidden)
    w1bd = jnp.einsum('gh,kric->krgihc', eye, w1r).reshape(2, CMP_LEN, NSA_KVD, NSA_KV_GROUPS * hidden)
    w2bd = jnp.einsum('gh,kcd->kgchd', eye, w2).reshape(2, NSA_KV_GROUPS * hidden, NSA_KVD)
    return pl.pallas_call(
        _compress_kernel,
        out_shape=jax.ShapeDtypeStruct((bsz, 2, nrow, NSA_KVD), BF16),
        grid=(bsz,),
        in_specs=[
            pl.BlockSpec((1, 2, s, NSA_KVD), lambda b: (b, 0, 0, 0)),
            _resident(pos2.shape),
            _resident(w1bd.shape),
            _resident(w2bd.shape),
        ],
        out_specs=pl.BlockSpec((1, 2, nrow, NSA_KVD), lambda b: (b, 0, 0, 0)),
        compiler_params=_params(("parallel",)),
        name="nsa_compress",
    )(kvc, pos2, w1bd.astype(BF16), w2bd.astype(BF16))


def _nsa_kernel(qn_ref, qr_ref, kv4_ref, cmp_ref, gate_ref, ovl_ref, hot_ref, gexp_ref, o_ref,
                lhs_ref, s_ref, p_ref, m_ref, acc_ref, *, seq, tk):
    tq = NSA_TQ
    nrow = cmp_ref.shape[2]
    ns = seq // SEL_BLOCK
    n_g, n_r = NSA_KV_GROUPS, NSA_HPG
    n_h = n_g * n_r
    rows = n_h * tq
    t0 = pl.program_id(1) * tq
    lane = lax.broadcasted_iota(jnp.int32, (tq, LANES), 1)
    gates = gate_ref[0]

    def stack(q_ref):
        parts = []
        for g in range(n_g):
            in_grp = (lane >= g * HEAD_DIM) & (lane < (g + 1) * HEAD_DIM)
            for r in range(n_r):
                ch = q_ref[0, :, r * LANES:(r + 1) * LANES]
                parts.append(jnp.where(in_grp, ch, jnp.zeros_like(ch)))
        return jnp.concatenate(parts, axis=0)

    half = rows // n_g
    neg_stats = jnp.full((rows, LANES), NEG, F32)

    def times_values(p_view, v, g, extra=None):
        vl = lax.broadcasted_iota(jnp.int32, v.shape, 1)
        vg = jnp.where((vl >= g * HEAD_DIM) & (vl < (g + 1) * HEAD_DIM), v, jnp.ones_like(v))
        if extra is not None:
            vg = jnp.concatenate([vg, extra], axis=1)
        return jnp.dot(p_view[g * half:(g + 1) * half, :], vg, preferred_element_type=F32)

    sc_ref, pc_ref = s_ref.at[:, 0:nrow], p_ref.at[:, 0:nrow]
    tq_c = t0 + lax.broadcasted_iota(jnp.int32, (tq, nrow), 0)
    cend = lax.broadcasted_iota(jnp.int32, (tq, nrow), 1) * CMP_STRIDE + (CMP_LEN - 1)
    sc_ref[...] = lax.dot_general(stack(qn_ref), cmp_ref[0, 0], NT_DIMS, preferred_element_type=F32)
    _stream_softmax(sc_ref, pc_ref, neg_stats, jnp.where(cend <= tq_c, 0.0, NEG), tq)
    both = jnp.concatenate([times_values(pc_ref, cmp_ref[0, 1], g, extra=ovl_ref[...]) for g in range(n_g)], axis=0)
    acc_c, imp_h = both[:, 0:LANES], both[:, LANES:]
    lane_r = lax.broadcasted_iota(jnp.int32, (rows, LANES), 1)
    grp_r = lax.broadcasted_iota(jnp.int32, (rows, LANES), 0) // half
    den_c = jnp.where(lane_r // HEAD_DIM == grp_r, pltpu.roll(acc_c, HEAD_DIM, axis=1), acc_c)
    has_block = jnp.where(t0 + lax.broadcasted_iota(jnp.int32, (tq, LANES), 0) >= CMP_LEN - 1, 1.0, 0.0)
    inv_c = jnp.concatenate([has_block] * n_h, axis=0) / den_c
    o_c = acc_c * inv_c

    imp_h = imp_h * inv_c
    imp_t = jnp.concatenate([functools.reduce(jnp.add, [imp_h[(g * n_r + r) * tq:(g * n_r + r + 1) * tq]
                                                        for r in range(n_r)]) for g in range(n_g)], axis=0).T
    jn = lax.broadcasted_iota(jnp.int32, (LANES, n_g * tq), 0)
    tpos = t0 + lax.broadcasted_iota(jnp.int32, (LANES, tq), 1)
    blk_t = jnp.concatenate([tpos] * n_g, axis=1) // SEL_BLOCK
    valid = jn <= blk_t
    forced = (jn == 0) | (jn == blk_t) | (jn == blk_t - 1)
    others = _pick_topk_rows(jnp.where(valid & jnp.logical_not(forced), imp_t, -jnp.inf), min(SEL_TOPK, ns) - 3)
    keep = valid & (forced | (others > 0.5))

    lhs_ref[:, 0:LANES] = stack(qr_ref)
    wk = WINDOW + tq
    sw_ref, pw_ref = s_ref.at[:, 0:wk], p_ref.at[:, 0:wk]
    kstart = pl.multiple_of(jnp.clip(t0 - WINDOW, 0, seq - wk), tq)
    diff = (t0 + lax.broadcasted_iota(jnp.int32, (tq, wk), 0)) - (
        kstart + lax.broadcasted_iota(jnp.int32, (tq, wk), 1))
    sw_ref[...] = lax.dot_general(lhs_ref[:, 0:LANES], kv4_ref[0, pl.ds(kstart, wk), 2 * LANES:3 * LANES], NT_DIMS,
                                  preferred_element_type=F32)
    _stream_softmax(sw_ref, pw_ref, neg_stats, jnp.where((diff >= 0) & (diff < WINDOW), 0.0, NEG), tq)
    acc_w = jnp.concatenate([times_values(pw_ref, kv4_ref[0, pl.ds(kstart, wk), 3 * LANES:4 * LANES], g)
                             for g in range(n_g)], axis=0)

    drop = jnp.where(keep, 0.0, -BIG).T.astype(BF16)
    lhs_ref[:, LANES:2 * LANES] = jnp.concatenate(
        [drop[g * tq:(g + 1) * tq] for g in range(n_g) for _ in range(n_r)], axis=0)

    m_ref[...] = neg_stats
    acc_ref[...] = jnp.zeros(acc_ref.shape, F32)

    def sel_tile(k0, width, bias):
        rhs = jnp.concatenate([kv4_ref[0, pl.ds(k0, width), 0:LANES], hot_ref[pl.ds(k0, width), :]], axis=1)
        views = []
        for g in range(n_g):
            gs = slice(g * half, (g + 1) * half)
            sg_ref = s_ref.at[gs, 0:width]
            sg_ref[...] = lax.dot_general(lhs_ref[gs, :], rhs, NT_DIMS, preferred_element_type=F32)
            views.append((gs, sg_ref, p_ref.at[gs, 0:width]))
        for g, (gs, sg_ref, pg_ref) in enumerate(views):
            m_old = m_ref[gs, :]
            m_new, _ = _stream_softmax(sg_ref, pg_ref, m_old, bias, tq)
            m_ref[gs, :] = m_new
            vl = lax.broadcasted_iota(jnp.int32, (width, LANES), 1)
            v = kv4_ref[0, pl.ds(k0, width), LANES:2 * LANES]
            vg = jnp.where((vl >= g * HEAD_DIM) & (vl < (g + 1) * HEAD_DIM), v, jnp.ones_like(v))
            acc_ref[gs, :] = jnp.exp2(m_old - m_new) * acc_ref[gs, :] + jnp.dot(pg_ref[...], vg,
                                                                                preferred_element_type=F32)

    last = t0 // tk

    @pl.loop(0, last)
    def _(kt):
        sel_tile(pl.multiple_of(kt * tk, tk), tk, None)

    k_last = pl.multiple_of(last * tk, tk)

    def last_tile(width):
        kpos = k_last + lax.broadcasted_iota(jnp.int32, (tq, width), 1)
        tq_s = t0 + lax.broadcasted_iota(jnp.int32, (tq, width), 0)
        sel_tile(k_last, width, jnp.where(kpos <= tq_s, 0.0, NEG))

    in_first_half = t0 - k_last + tq <= tk // 2
    pl.when(in_first_half)(lambda: last_tile(tk // 2))
    pl.when(jnp.logical_not(in_first_half))(lambda: last_tile(tk))

    acc_s = acc_ref[...]

    gates_b = gates.astype(BF16)
    spread = [jnp.dot(gates_b, gexp_ref[c], preferred_element_type=F32) for c in range(3)]
    first_half = lane < HEAD_DIM
    for r in range(n_r):
        rows_g = [slice((g * n_r + r) * tq, (g * n_r + r + 1) * tq) for g in range(n_g)]
        y = spread[0][:, r * LANES:(r + 1) * LANES] * jnp.where(first_half, o_c[rows_g[0]], o_c[rows_g[1]])
        for c, acc in ((1, acc_s), (2, acc_w)):
            vals = jnp.where(first_half, acc[rows_g[0]], acc[rows_g[1]])
            dens = pltpu.roll(jnp.where(first_half, acc[rows_g[1]], acc[rows_g[0]]), HEAD_DIM, axis=1)
            y = y + spread[c][:, r * LANES:(r + 1) * LANES] * (vals / dens)
        o_ref[0, :, r * LANES:(r + 1) * LANES] = y.astype(BF16)


def _nsa(qn, qr, kv4, cmp, gates, *, tk):
    bsz, s, _ = qn.shape
    tq = NSA_TQ
    ns = s // SEL_BLOCK
    assert ns <= LANES
    assert min(SEL_TOPK, ns) >= 3
    ovl = jnp.pad(_overlap(cmp.shape[2], ns), ((0, 0), (0, LANES - ns))).astype(BF16)
    hot = _block_one_hot(s, SEL_BLOCK)
    head_of_p = _nsa_head_perm() // HEAD_DIM
    gexp = (jnp.arange(LANES)[None, :, None] == 3 * head_of_p[None, None, :] + jnp.arange(3)[:, None, None]).astype(BF16)
    rows = NSA_HEADS * tq
    width = max(cmp.shape[2], WINDOW + tq, tk)
    row = lambda b, i: (b, i, 0)
    return pl.pallas_call(
        functools.partial(_nsa_kernel, seq=s, tk=tk),
        out_shape=jax.ShapeDtypeStruct((bsz, s, NSA_QD), BF16),
        grid=(bsz, s // tq),
        in_specs=[
            pl.BlockSpec((1, tq, NSA_QD), row),
            pl.BlockSpec((1, tq, NSA_QD), row),
            pl.BlockSpec((1, s, 4 * NSA_KVD), lambda b, i: (b, 0, 0), pipeline_mode=pl.Buffered(1)),
            pl.BlockSpec((1, 2) + cmp.shape[2:], lambda b, i: (b, 0, 0, 0), pipeline_mode=pl.Buffered(1)),
            pl.BlockSpec((1, tq, LANES), row),
            _resident(ovl.shape),
            _resident(hot.shape),
            _resident(gexp.shape),
        ],
        out_specs=pl.BlockSpec((1, tq, NSA_QD), row),
        scratch_shapes=[pltpu.VMEM((rows, 2 * LANES), BF16), pltpu.VMEM((rows, width), F32),
                        pltpu.VMEM((rows, width), BF16),
                        pltpu.VMEM((rows, LANES), F32), pltpu.VMEM((rows, LANES), F32)],
        compiler_params=_params(("parallel", "parallel")),
        name="nsa_attn",
    )(qn, qr, kv4, cmp, gates, ovl, hot, gexp)


def _rope_tables(s):
    half = ROT_DIM // 2
    pos = jnp.arange(s, dtype=F32)
    freqs = ROPE_THETA ** (-jnp.arange(half, dtype=F32) * 2.0 / ROT_DIM)
    ang = pos[:, None] * freqs[None, :]
    cos, sin = jnp.cos(ang), jnp.sin(ang)
    ones = jnp.ones((s, HEAD_DIM - ROT_DIM), F32)
    zeros = jnp.zeros((s, HEAD_DIM - ROT_DIM), F32)
    zh = jnp.zeros((s, half), F32)
    cos_t = jnp.concatenate([cos, cos, ones], axis=1)
    sa_t = jnp.concatenate([-sin, zh, zeros], axis=1)
    sb_t = jnp.concatenate([zh, sin, zeros], axis=1)
    return tuple(jnp.tile(t, (1, LANES // HEAD_DIM)) for t in (cos_t, sa_t, sb_t))


def _block_one_hot(s, block):
    return (jnp.arange(s)[:, None] // block == jnp.arange(LANES)[None, :]).astype(BF16)


def _block_diag(w):
    n, a, b = w.shape
    eye = jnp.eye(n, dtype=w.dtype)
    return (eye[:, None, :, None] * w[:, :, None, :]).reshape(n * a, n * b)


def _nsa_head_perm():
    r = jnp.arange(NSA_HPG)[:, None, None]
    g = jnp.arange(NSA_KV_GROUPS)[None, :, None]
    dd = jnp.arange(HEAD_DIM)[None, None, :]
    return (HEAD_DIM * (NSA_HPG * g + r) + dd).reshape(-1)


def _overlap(nrow, ns):
    ci = jnp.arange(nrow)[:, None] * CMP_STRIDE
    sj = jnp.arange(ns)[None, :] * SEL_BLOCK
    ov = jnp.clip(jnp.minimum(ci + CMP_LEN, sj + SEL_BLOCK) - jnp.maximum(ci, sj), 0, None)
    return ov.astype(F32) / CMP_LEN


def kernel(x, c, mod_w, mod_b, norm_g, ffn_w1, ffn_w2, mix0_in_w, lru_conv_w, lru_conv_b, lru_wa, lru_ba,
           lru_wx, lru_bx, lru_lambda, mix0_out_w, mix1_in_w, cmp_pos, cmp_w1, cmp_w2, mix1_out_w, final_norm_g):
    bsz, s, d = x.shape
    depth = mod_w.shape[0]
    tm = 512
    assert d == D_MODEL and s % (2 * tm) == 0 and s >= WINDOW + NSA_TQ

    mod = _modulation(c, mod_w, mod_b)
    tabs = _rope_tables(s)
    w1 = ffn_w1.astype(BF16)
    w2 = ffn_w2.astype(BF16)

    for l in range(depth):
        j = l // 2
        x = _ffn(x, mod[l], norm_g[l, 0], w1, w2, (l, 0), final_norm_g, row0=0, final=False, tm=tm)
        if l % 2 == 0:
            xg, q, k, v, cent = _inproj0(x, mod[l], norm_g[l, 1], mix0_in_w[j].astype(BF16), tabs, tm=tm)
            y_lru = _lru(xg, lru_conv_w[j], lru_conv_b[j],
                         _block_diag(lru_wa[j]).astype(BF16), lru_ba[j].reshape(-1),
                         _block_diag(lru_wx[j]).astype(BF16), lru_bx[j].reshape(-1), lru_lambda[j], tm=256)
            y_att = _moba(q, k, v, cent.reshape(bsz, -1, MOBA_WIDTH))
            wo = mix0_out_w[j].astype(BF16)
            mix = ((y_lru, wo[:LRU_WIDTH]), (y_att, wo[LRU_WIDTH:]))
        else:
            perm = _nsa_head_perm()
            w_in = mix1_in_w[j]
            w_in = jnp.concatenate([w_in[:, :NSA_QD][:, perm], w_in[:, NSA_QD:]], axis=1)
            w_in = jnp.pad(w_in, ((0, 0), (0, IN1_PAD - w_in.shape[1]))).astype(BF16)
            qn, qr, kvc, kv4, gates = _inproj1(x, mod[l], norm_g[l, 1], w_in, tabs, tm=2 * tm)
            cmp = _compress(kvc, cmp_pos[j], cmp_w1[j], cmp_w2[j])
            y = _nsa(qn, qr, kv4, cmp, gates, tk=512)
            mix = ((y, mix1_out_w[j][perm].astype(BF16)),)
        x = _ffn(x, mod[l], norm_g[l, 2], w1, w2, (l, 1), final_norm_g,
                 row0=6, final=(l == depth - 1), tm=tm, mix=mix)
    return x
```

```python
import functools
import math

import jax
import jax.numpy as jnp
from jax import lax
from jax.experimental import pallas as pl
from jax.experimental.pallas import tpu as pltpu

F32 = jnp.float32
BF16 = jnp.bfloat16

D_MODEL = 1024
EPS = 1e-6
ROPE_THETA = 500000.0
HEAD_DIM = 64
ROT_DIM = HEAD_DIM // 4
NEG = -1e30
BIG = 1e30
LANES = 128

LRU_WIDTH = D_MODEL // 2
LRU_BLOCKS = 8
CONV_WIDTH = 4
LRU_C = 8.0
MOBA_WIDTH = D_MODEL // 2
MOBA_BLOCK = 256
MOBA_TOPK = 3
MOBA_TQ = 2 * MOBA_BLOCK
MOBA_CHUNKS = 4

NSA_HEADS = D_MODEL // HEAD_DIM
NSA_KV_GROUPS = 2
NSA_HPG = NSA_HEADS // NSA_KV_GROUPS
CMP_LEN = 32
CMP_STRIDE = 16
SEL_BLOCK = 64
SEL_TOPK = 16
WINDOW = 512
NSA_QD = NSA_HEADS * HEAD_DIM
NSA_KVD = NSA_KV_GROUPS * HEAD_DIM
IN1_PAD = NSA_QD + 6 * NSA_KVD + LANES
NSA_TQ = 128

VMEM_LIMIT = 52 * 1024 * 1024

NT_DIMS = (((1,), (1,)), ((), ()))
Q_PRESCALE = HEAD_DIM ** -0.5 * math.log2(math.e)


def _params(sem):
    return pltpu.CompilerParams(dimension_semantics=sem, vmem_limit_bytes=VMEM_LIMIT)


def _resident(shape):
    nd = len(shape)
    return pl.BlockSpec(shape, lambda *_: (0,) * nd, pipeline_mode=pl.Buffered(1))


def _sigmoid(z):
    return 0.5 * jnp.tanh(0.5 * z) + 0.5


def _silu(a):
    return a * jax.nn.sigmoid(a)


def _gelu_tanh(x):
    cdf = 0.5 * (1.0 + jnp.tanh(math.sqrt(2.0 / math.pi) * (x + 0.044715 * (x * x * x))))
    return x * cdf


def _rmsnorm(x, g):
    return x * lax.rsqrt(jnp.mean(x * x, axis=-1, keepdims=True) + EPS) * g


def _norm_mod(x, g, shift, scale):
    return _rmsnorm(x, g) * (1.0 + scale) + shift


def _rope(u, cos, sa, sb):
    outs = []
    for j in range(u.shape[1] // LANES):
        c = u[:, j * LANES:(j + 1) * LANES]
        up = pltpu.roll(c, LANES - ROT_DIM // 2, axis=1)
        dn = pltpu.roll(c, ROT_DIM // 2, axis=1)
        outs.append(c * cos + up * sa + dn * sb)
    return jnp.concatenate(outs, axis=1)


def _bf16_pieces(x):
    hi = x.astype(BF16)
    r1 = x - hi.astype(F32)
    mid = r1.astype(BF16)
    lo = (r1 - mid.astype(F32)).astype(BF16)
    return jnp.concatenate([hi, mid, lo], axis=0)


SOFTMAX_CHUNK = 64


def _stream_softmax(s_ref, p_ref, m_old, bias, tq):
    rows, n = s_ref.shape
    n_chunks, n_l = rows // SOFTMAX_CHUNK, n // LANES

    def chunk(c):
        sl = slice(c * SOFTMAX_CHUNK, (c + 1) * SOFTMAX_CHUNK)
        sc = s_ref[sl, :]
        if bias is not None:
            q0 = (c * SOFTMAX_CHUNK) % tq
            sc = sc + bias[q0:q0 + SOFTMAX_CHUNK]
        return sl, [sc[:, j * LANES:(j + 1) * LANES] for j in range(n_l)]

    partial_max = jnp.concatenate([functools.reduce(jnp.maximum, chunk(c)[1]) for c in range(n_chunks)], axis=0)
    m_new = jnp.maximum(m_old, jnp.broadcast_to(jnp.max(partial_max, axis=1, keepdims=True), (rows, LANES)))
    sums = []
    for c in range(n_chunks):
        sl, cols = chunk(c)
        m_c = m_new[sl]
        ps = [jnp.exp2(col - m_c) for col in cols]
        for j, pj in enumerate(ps):
            p_ref[sl, j * LANES:(j + 1) * LANES] = pj.astype(BF16)
        sums.append(functools.reduce(jnp.add, ps))
    return m_new, jnp.concatenate(sums, axis=0)


def _pick_topk_rows(g, k):
    n_ids = g.shape[0]
    ids = lax.broadcasted_iota(jnp.int32, g.shape, 0).astype(F32)
    picked = jnp.zeros(g.shape, F32)
    for _ in range(k):
        mx = jnp.max(g, axis=0, keepdims=True)
        first = jnp.min(jnp.where(g == mx, ids, float(n_ids)), axis=0, keepdims=True)
        hit = ids == first
        picked = jnp.where(hit, 1.0, picked)
        g = jnp.where(hit, -jnp.inf, g)
    return picked


def _mod_kernel(c_ref, w_ref, b_ref, o_ref):
    cond = _silu(c_ref[...])
    o_ref[0] = jnp.dot(cond.astype(BF16), w_ref[0].astype(BF16), preferred_element_type=F32) + b_ref[0]


def _modulation(c, mod_w, mod_b):
    depth, d, n = mod_w.shape
    bsz = c.shape[0]
    bp = -(-bsz // 8) * 8
    cp = jnp.pad(c, ((0, bp - bsz), (0, 0)))
    tn = 1152
    out = pl.pallas_call(
        _mod_kernel,
        out_shape=jax.ShapeDtypeStruct((depth, bp, n), F32),
        grid=(depth, n // tn),
        in_specs=[
            pl.BlockSpec((bp, d), lambda l, j: (0, 0)),
            pl.BlockSpec((1, d, tn), lambda l, j: (l, 0, j)),
            pl.BlockSpec((1, 1, tn), lambda l, j: (l, 0, j)),
        ],
        out_specs=pl.BlockSpec((1, bp, tn), lambda l, j: (l, 0, j)),
        compiler_params=_params(("parallel", "parallel")),
        name="adaln_mod",
    )(cp, mod_w, mod_b.reshape(depth, 1, n))
    return out[:, :bsz].reshape(depth, bsz, 9, d)


def _ffn_kernel(x_ref, mod_ref, g_ref, w1_ref, w2_ref, gf_ref, *rest, row0, final, n_mix):
    o_ref = rest[-1]
    x = x_ref[0]
    m = mod_ref[0]
    if n_mix:
        mixed = sum(jnp.dot(y_ref[0], w_ref[...], preferred_element_type=F32)
                    for y_ref, w_ref in zip(rest[:n_mix], rest[n_mix:2 * n_mix]))
        x = x + m[5:6] * mixed
    h = _norm_mod(x, g_ref[...], m[row0:row0 + 1], m[row0 + 1:row0 + 2]).astype(BF16)
    d_ff = w2_ref.shape[2]
    a = jnp.dot(h, w1_ref[0, 0, :, 0:d_ff], preferred_element_type=F32)
    b = jnp.dot(h, w1_ref[0, 0, :, d_ff:2 * d_ff], preferred_element_type=F32)
    act = (_silu(a) * b).astype(BF16)
    y = jnp.dot(act, w2_ref[0, 0], preferred_element_type=F32)
    out = x + (0.5 * m[row0 + 2:row0 + 3]) * y
    if final:
        out = _rmsnorm(out, gf_ref[...])
    o_ref[0] = out


def _ffn(x, mod_l, g, w1, w2, which, gf, *, row0, final, tm, mix=()):
    bsz, s, d = x.shape
    picked = lambda shape: pl.BlockSpec((1, 1) + shape[2:], lambda b, i: which + (0, 0),
                                        pipeline_mode=pl.Buffered(1))
    kern = functools.partial(_ffn_kernel, row0=row0, final=final, n_mix=len(mix))
    row = lambda b, i: (b, i, 0)
    return pl.pallas_call(
        kern,
        out_shape=jax.ShapeDtypeStruct(x.shape, F32),
        grid=(bsz, s // tm),
        in_specs=[
            pl.BlockSpec((1, tm, d), row),
            pl.BlockSpec((1, 9, d), lambda b, i: (b, 0, 0)),
            _resident((1, d)),
            picked(w1.shape),
            picked(w2.shape),
            _resident((1, d)),
        ] + [pl.BlockSpec((1, tm, y.shape[2]), row) for y, _ in mix] + [_resident(w.shape) for _, w in mix],
        out_specs=pl.BlockSpec((1, tm, d), row),
        compiler_params=_params(("parallel", "parallel")),
        name="ffn_final" if final else ("mix_ffn" if mix else "ffn"),
    )(x, mod_l, g.reshape(1, d), w1, w2, gf.reshape(1, d), *[y for y, _ in mix], *[w for _, w in mix])


def _inproj0_kernel(x_ref, mod_ref, g_ref, w_ref, cos_ref, sa_ref, sb_ref,
                    xg_ref, q_ref, k_ref, v_ref, cent_ref):
    m = mod_ref[0]
    h = _norm_mod(x_ref[0], g_ref[...], m[3:4], m[4:5]).astype(BF16)
    u = jnp.dot(h, w_ref[...], preferred_element_type=F32)
    o1 = 2 * LRU_WIDTH
    xg_ref[0] = u[:, :o1]
    cos, sa, sb = cos_ref[...], sa_ref[...], sb_ref[...]
    q = _rope(u[:, o1:o1 + MOBA_WIDTH], cos, sa, sb)
    k = _rope(u[:, o1 + MOBA_WIDTH:o1 + 2 * MOBA_WIDTH], cos, sa, sb)
    q_ref[0] = (q * Q_PRESCALE).astype(BF16)
    k_ref[0] = k.astype(BF16)
    v_ref[0] = u[:, o1 + 2 * MOBA_WIDTH:].astype(BF16)
    for j in range(k.shape[0] // MOBA_BLOCK):
        cent_ref[0, j] = jnp.mean(k[j * MOBA_BLOCK:(j + 1) * MOBA_BLOCK], axis=0, keepdims=True)


def _inproj0(x, mod_l, g, w, rope_tabs, *, tm):
    bsz, s, d = x.shape
    assert tm % MOBA_BLOCK == 0
    nb = s // MOBA_BLOCK
    n = w.shape[1]
    row = lambda b, i: (b, i, 0)
    tab = pl.BlockSpec((tm, LANES), lambda b, i: (i, 0))
    return pl.pallas_call(
        _inproj0_kernel,
        out_shape=(
            jax.ShapeDtypeStruct((bsz, s, 2 * LRU_WIDTH), F32),
            jax.ShapeDtypeStruct((bsz, s, MOBA_WIDTH), BF16),
            jax.ShapeDtypeStruct((bsz, s, MOBA_WIDTH), BF16),
            jax.ShapeDtypeStruct((bsz, s, MOBA_WIDTH), BF16),
            jax.ShapeDtypeStruct((bsz, nb, 1, MOBA_WIDTH), F32),
        ),
        grid=(bsz, s // tm),
        in_specs=[
            pl.BlockSpec((1, tm, d), row),
            pl.BlockSpec((1, 9, d), lambda b, i: (b, 0, 0)),
            _resident((1, d)),
            _resident((d, n)),
            tab, tab, tab,
        ],
        out_specs=(
            pl.BlockSpec((1, tm, 2 * LRU_WIDTH), row),
            pl.BlockSpec((1, tm, MOBA_WIDTH), row),
            pl.BlockSpec((1, tm, MOBA_WIDTH), row),
            pl.BlockSpec((1, tm, MOBA_WIDTH), row),
            pl.BlockSpec((1, tm // MOBA_BLOCK, 1, MOBA_WIDTH), lambda b, i: (b, i, 0, 0)),
        ),
        compiler_params=_params(("parallel", "parallel")),
        name="inproj0",
    )(x, mod_l, g.reshape(1, d), w, *rope_tabs)


def _lru_kernel(xg_ref, cw_ref, cb_ref, wa_ref, ba_ref, wx_ref, bx_ref, lam_ref, y_ref, xbuf, hprev, *, tm):
    c = LRU_WIDTH
    halo = 8

    @pl.when(pl.program_id(1) == 0)
    def _():
        xbuf[0:halo, :] = jnp.zeros((halo, c), F32)
        hprev[...] = jnp.zeros(hprev.shape, F32)

    x = xg_ref[0, :, 0:c]
    gate_in = xg_ref[0, :, c:2 * c]
    xbuf[halo:halo + tm, :] = x
    cw = cw_ref[...]
    xc = cb_ref[...] + cw[CONV_WIDTH - 1:CONV_WIDTH] * x
    for kk in range(CONV_WIDTH - 1):
        back = CONV_WIDTH - 1 - kk
        xc = xc + cw[kk:kk + 1] * xbuf[halo - back:halo - back + tm, :]
    xbuf[0:halo, :] = x[tm - halo:tm, :]

    xcb = xc.astype(BF16)
    r = _sigmoid(jnp.dot(xcb, wa_ref[...], preferred_element_type=F32) + ba_ref[...])
    ig = _sigmoid(jnp.dot(xcb, wx_ref[...], preferred_element_type=F32) + bx_ref[...])
    nlam = -lam_ref[...]
    softplus = jnp.maximum(nlam, 0.0) + jnp.log1p(jnp.exp(-jnp.abs(nlam)))
    log_a = (-LRU_C * r) * softplus
    a = jnp.exp(log_a)
    gain2 = -jnp.tanh(log_a) * (a * a + 1.0)
    mult = jnp.where(gain2 > 0.0, gain2 * lax.rsqrt(gain2), 0.0)
    bt = mult * (ig * xc)

    row = lax.broadcasted_iota(jnp.int32, (tm, c), 0)
    step = 1
    while step < tm:
        a_sh = pltpu.roll(a, step, axis=0)
        b_sh = pltpu.roll(bt, step, axis=0)
        valid = row >= step
        bt = jnp.where(valid, a * b_sh + bt, bt)
        a = jnp.where(valid, a * a_sh, a)
        step *= 2
    h = bt + a * hprev[0:1, :]
    hprev[0:1, :] = h[tm - 1:tm, :]
    y_ref[0] = (h * _gelu_tanh(gate_in)).astype(BF16)


def _lru(xg, cw, cb, wa_bd, ba, wx_bd, bx, lam, *, tm):
    bsz, s, _ = xg.shape
    c = LRU_WIDTH
    vec = _resident((1, c))
    return pl.pallas_call(
        functools.partial(_lru_kernel, tm=tm),
        out_shape=jax.ShapeDtypeStruct((bsz, s, c), BF16),
        grid=(bsz, s // tm),
        in_specs=[
            pl.BlockSpec((1, tm, 2 * c), lambda b, i: (b, i, 0)),
            _resident((CONV_WIDTH, c)), vec,
            _resident((c, c)), vec,
            _resident((c, c)), vec,
            vec,
        ],
        out_specs=pl.BlockSpec((1, tm, c), lambda b, i: (b, i, 0)),
        scratch_shapes=[pltpu.VMEM((tm + 8, c), F32), pltpu.VMEM((8, c), F32)],
        compiler_params=_params(("parallel", "arbitrary")),
        name="rg_lru",
    )(xg, cw, cb.reshape(1, c), wa_bd, ba.reshape(1, c), wx_bd, bx.reshape(1, c), lam.reshape(1, c))


def _moba_kernel(q_ref, k_ref, v_ref, cent_ref, hot_ref, o_ref, lhs_ref, s_ref, p_ref, m_ref, l_ref, acc_ref):
    tq = MOBA_TQ
    rows = 2 * tq
    nb_pad = -(-(k_ref.shape[1] // MOBA_BLOCK) // 8) * 8
    i = pl.program_id(2)
    lane = lax.broadcasted_iota(jnp.int32, (tq, LANES), 1)
    bid = lax.broadcasted_iota(jnp.int32, (nb_pad, rows), 0)
    qidx = jnp.concatenate([lax.broadcasted_iota(jnp.int32, (nb_pad, tq), 1)] * 2, axis=1)
    own = (tq // MOBA_BLOCK) * i + qidx // MOBA_BLOCK
    past = bid < own
    chunks = [slice(c * LANES, (c + 1) * LANES) for c in range(MOBA_CHUNKS)]

    for c, cl in enumerate(chunks):
        q = q_ref[0, :, cl]
        zero = jnp.zeros_like(q)
        q2 = jnp.concatenate([jnp.where(lane < HEAD_DIM, q, zero), jnp.where(lane >= HEAD_DIM, q, zero)], axis=0)
        gate3 = lax.dot_general(_bf16_pieces(cent_ref[0, 0:nb_pad, cl]), q2, NT_DIMS, preferred_element_type=F32)
        gate_t = gate3[:nb_pad] + gate3[nb_pad:2 * nb_pad] + gate3[2 * nb_pad:]
        sel_t = _pick_topk_rows(jnp.where(past, gate_t, -jnp.inf), MOBA_TOPK)
        drop = jnp.where((past & (sel_t > 0.5)) | (bid == own), 0.0, -BIG).T.astype(BF16)
        lhs_ref[c, :, 0:LANES] = q2
        lhs_ref[c, :, LANES:LANES + nb_pad] = drop
        lhs_ref[c, :, LANES + nb_pad:2 * LANES] = jnp.zeros((rows, LANES - nb_pad), BF16)
        m_ref[c] = jnp.full((rows, LANES), NEG, F32)
        l_ref[c] = jnp.zeros((rows, LANES), F32)
        acc_ref[c] = jnp.zeros((rows, LANES), F32)

    def tile(kt, bias):
        k0 = pl.multiple_of(kt * tq, tq)
        for c, cl in enumerate(chunks):
            rhs = jnp.concatenate([k_ref[0, pl.ds(k0, tq), cl], hot_ref[pl.ds(k0, tq), :]], axis=1)
            s_ref[c] = lax.dot_general(lhs_ref[c], rhs, NT_DIMS, preferred_element_type=F32)
        for c, cl in enumerate(chunks):
            m_old = m_ref[c]
            m_new, p_sum = _stream_softmax(s_ref.at[c], p_ref.at[c], m_old, bias, tq)
            alpha = jnp.exp2(m_old - m_new)
            m_ref[c] = m_new
            l_ref[c] = alpha * l_ref[c] + p_sum
            acc_ref[c] = alpha * acc_ref[c] + jnp.dot(p_ref[c], v_ref[0, pl.ds(k0, tq), cl],
                                                      preferred_element_type=F32)

    @pl.when(i >= 0)
    def _():
        qpos = lax.broadcasted_iota(jnp.int32, (tq, tq), 0)
        kpos = lax.broadcasted_iota(jnp.int32, (tq, tq), 1)
        tile(i, jnp.where(kpos <= qpos, 0.0, NEG))

    @pl.loop(0, i)
    def _(kt):
        tile(kt, None)

    for c, cl in enumerate(chunks):
        out = acc_ref[c] / jnp.sum(l_ref[c], axis=1, keepdims=True)
        o_ref[0, :, cl] = jnp.where(lane < HEAD_DIM, out[:tq], out[tq:]).astype(BF16)


def _moba(q, k, v, cent):
    bsz, s, w = q.shape
    tq = MOBA_TQ
    nb = s // MOBA_BLOCK
    assert nb <= LANES and s % tq == 0
    cent = jnp.pad(cent, ((0, 0), (0, LANES - nb), (0, 0)))
    hot = _block_one_hot(s, MOBA_BLOCK)
    nc, cw = MOBA_CHUNKS, MOBA_CHUNKS * LANES
    return pl.pallas_call(
        _moba_kernel,
        out_shape=jax.ShapeDtypeStruct((bsz, s, w), BF16),
        grid=(bsz, w // cw, s // tq),
        in_specs=[
            pl.BlockSpec((1, tq, cw), lambda b, hp, i: (b, i, hp)),
            pl.BlockSpec((1, s, cw), lambda b, hp, i: (b, 0, hp), pipeline_mode=pl.Buffered(1)),
            pl.BlockSpec((1, s, cw), lambda b, hp, i: (b, 0, hp), pipeline_mode=pl.Buffered(1)),
            pl.BlockSpec((1, LANES, cw), lambda b, hp, i: (b, 0, hp)),
            _resident(hot.shape),
        ],
        out_specs=pl.BlockSpec((1, tq, cw), lambda b, hp, i: (b, i, hp)),
        scratch_shapes=[pltpu.VMEM((nc, 2 * tq, 2 * LANES), BF16), pltpu.VMEM((nc, 2 * tq, tq), F32),
                        pltpu.VMEM((nc, 2 * tq, tq), BF16)] + [pltpu.VMEM((nc, 2 * tq, LANES), F32)] * 3,
        compiler_params=_params(("parallel", "parallel", "parallel")),
        name="moba_attn",
    )(q, k, v, cent, hot)


def _inproj1_kernel(x_ref, mod_ref, g_ref, w_ref, cos_ref, sa_ref, sb_ref,
                    qn_ref, qr_ref, kvc_ref, kv4_ref, gate_ref):
    m = mod_ref[0]
    h = _norm_mod(x_ref[0], g_ref[...], m[3:4], m[4:5]).astype(BF16)
    u = jnp.dot(h, w_ref[...], preferred_element_type=F32)
    cos, sa, sb = cos_ref[...], sa_ref[...], sb_ref[...]
    q = u[:, :NSA_QD] * Q_PRESCALE
    qn_ref[0] = q.astype(BF16)
    qr_ref[0] = _rope(q, cos, sa, sb).astype(BF16)
    o = NSA_QD
    kvc_ref[0, 0] = u[:, o:o + NSA_KVD]
    kvc_ref[0, 1] = u[:, o + NSA_KVD:o + 2 * NSA_KVD]
    o += 2 * NSA_KVD
    ks = _rope(u[:, o:o + NSA_KVD], cos, sa, sb)
    vs = u[:, o + NSA_KVD:o + 2 * NSA_KVD]
    kw = _rope(u[:, o + 2 * NSA_KVD:o + 3 * NSA_KVD], cos, sa, sb)
    vw = u[:, o + 3 * NSA_KVD:o + 4 * NSA_KVD]
    kv4_ref[0] = jnp.concatenate([ks, vs, kw, vw], axis=1).astype(BF16)
    gate_ref[0] = jax.nn.sigmoid(u[:, o + 4 * NSA_KVD:])


def _inproj1(x, mod_l, g, w, rope_tabs, *, tm):
    bsz, s, d = x.shape
    n = w.shape[1]
    row = lambda b, i: (b, i, 0)
    tab = pl.BlockSpec((tm, LANES), lambda b, i: (i, 0))
    widths = (NSA_QD, NSA_QD, None, 4 * NSA_KVD, LANES)
    dtypes = (BF16, BF16, F32, BF16, F32)
    shape = lambda wd: (bsz, s, wd) if wd else (bsz, 2, s, NSA_KVD)
    spec = lambda wd: pl.BlockSpec((1, tm, wd), row) if wd else pl.BlockSpec((1, 2, tm, NSA_KVD),
                                                                             lambda b, i: (b, 0, i, 0))
    return pl.pallas_call(
        _inproj1_kernel,
        out_shape=tuple(jax.ShapeDtypeStruct(shape(wd), dt) for wd, dt in zip(widths, dtypes)),
        grid=(bsz, s // tm),
        in_specs=[
            pl.BlockSpec((1, tm, d), row),
            pl.BlockSpec((1, 9, d), lambda b, i: (b, 0, 0)),
            _resident((1, d)),
            _resident((d, n)),
            tab, tab, tab,
        ],
        out_specs=tuple(spec(wd) for wd in widths),
        compiler_params=_params(("parallel", "parallel")),
        name="inproj1",
    )(x, mod_l, g.reshape(1, d), w, *rope_tabs)


def _compress_kernel(kvc_ref, pos_ref, w1_ref, w2_ref, o_ref):
    nrow = o_ref.shape[2]
    for kv in range(2):
        top = jnp.zeros((nrow, w1_ref.shape[3]), F32)
        bot = jnp.zeros((nrow, w1_ref.shape[3]), F32)
        for r in range(CMP_STRIDE):
            x = kvc_ref[0, kv, pl.ds(r, nrow, stride=CMP_STRIDE), :]
            top += jnp.dot((x + pos_ref[kv, r:r + 1]).astype(BF16), w1_ref[kv, r], preferred_element_type=F32)
            bot += jnp.dot((x + pos_ref[kv, CMP_STRIDE + r:CMP_STRIDE + r + 1]).astype(BF16),
                           w1_ref[kv, CMP_STRIDE + r], preferred_element_type=F32)
        pre = top + pltpu.roll(bot, nrow - 1, axis=0)
        o_ref[0, kv] = jnp.dot(_gelu_tanh(pre).astype(BF16), w2_ref[kv], preferred_element_type=F32).astype(BF16)


def _compress(kvc, pos, w1, w2):
    bsz, _, s, _ = kvc.shape
    nrow = s // CMP_STRIDE
    eye = jnp.eye(NSA_KV_GROUPS, dtype=F32)
    pos2 = jnp.tile(pos, (1, 1, NSA_KV_GROUPS))
    hidden = w1.shape[-1]
    w1r = w1.reshape(2, CMP_LEN, HEAD_DIM, hidden)
    w1bd = jnp.einsum('gh,kric->krgihc', eye, w1r).reshape(2, CMP_LEN, NSA_KVD, NSA_KV_GROUPS * hidden)
    w2bd = jnp.einsum('gh,kcd->kgchd', eye, w2).reshape(2, NSA_KV_GROUPS * hidden, NSA_KVD)
    return pl.pallas_call(
        _compress_kernel,
        out_shape=jax.ShapeDtypeStruct((bsz, 2, nrow, NSA_KVD), BF16),
        grid=(bsz,),
        in_specs=[
            pl.BlockSpec((1, 2, s, NSA_KVD), lambda b: (b, 0, 0, 0)),
            _resident(pos2.shape),
            _resident(w1bd.shape),
            _resident(w2bd.shape),
        ],
        out_specs=pl.BlockSpec((1, 2, nrow, NSA_KVD), lambda b: (b, 0, 0, 0)),
        compiler_params=_params(("parallel",)),
        name="nsa_compress",
    )(kvc, pos2, w1bd.astype(BF16), w2bd.astype(BF16))


def _nsa_kernel(qn_ref, qr_ref, kv4_ref, cmp_ref, gate_ref, ovl_ref, hot_ref, gexp_ref, o_ref,
                lhs_ref, s_ref, p_ref, m_ref, acc_ref, *, seq, tk):
    tq = NSA_TQ
    nrow = cmp_ref.shape[2]
    ns = seq // SEL_BLOCK
    n_g, n_r = NSA_KV_GROUPS, NSA_HPG
    n_h = n_g * n_r
    rows = n_h * tq
    t0 = pl.program_id(1) * tq
    lane = lax.broadcasted_iota(jnp.int32, (tq, LANES), 1)
    gates = gate_ref[0]

    def stack(q_ref):
        parts = []
        for g in range(n_g):
            in_grp = (lane >= g * HEAD_DIM) & (lane < (g + 1) * HEAD_DIM)
            for r in range(n_r):
                ch = q_ref[0, :, r * LANES:(r + 1) * LANES]
                parts.append(jnp.where(in_grp, ch, jnp.zeros_like(ch)))
        return jnp.concatenate(parts, axis=0)

    half = rows // n_g
    neg_stats = jnp.full((rows, LANES), NEG, F32)

    def times_values(p_view, v, g, extra=None):
        vl = lax.broadcasted_iota(jnp.int32, v.shape, 1)
        vg = jnp.where((vl >= g * HEAD_DIM) & (vl < (g + 1) * HEAD_DIM), v, jnp.ones_like(v))
        if extra is not None:
            vg = jnp.concatenate([vg, extra], axis=1)
        return jnp.dot(p_view[g * half:(g + 1) * half, :], vg, preferred_element_type=F32)

    sc_ref, pc_ref = s_ref.at[:, 0:nrow], p_ref.at[:, 0:nrow]
    tq_c = t0 + lax.broadcasted_iota(jnp.int32, (tq, nrow), 0)
    cend = lax.broadcasted_iota(jnp.int32, (tq, nrow), 1) * CMP_STRIDE + (CMP_LEN - 1)
    sc_ref[...] = lax.dot_general(stack(qn_ref), cmp_ref[0, 0], NT_DIMS, preferred_element_type=F32)
    _stream_softmax(sc_ref, pc_ref, neg_stats, jnp.where(cend <= tq_c, 0.0, NEG), tq)
    both = jnp.concatenate([times_values(pc_ref, cmp_ref[0, 1], g, extra=ovl_ref[...]) for g in range(n_g)], axis=0)
    acc_c, imp_h = both[:, 0:LANES], both[:, LANES:]
    lane_r = lax.broadcasted_iota(jnp.int32, (rows, LANES), 1)
    grp_r = lax.broadcasted_iota(jnp.int32, (rows, LANES), 0) // half
    den_c = jnp.where(lane_r // HEAD_DIM == grp_r, pltpu.roll(acc_c, HEAD_DIM, axis=1), acc_c)
    has_block = jnp.where(t0 + lax.broadcasted_iota(jnp.int32, (tq, LANES), 0) >= CMP_LEN - 1, 1.0, 0.0)
    inv_c = jnp.concatenate([has_block] * n_h, axis=0) / den_c
    o_c = acc_c * inv_c

    imp_h = imp_h * inv_c
    imp_t = jnp.concatenate([functools.reduce(jnp.add, [imp_h[(g * n_r + r) * tq:(g * n_r + r + 1) * tq]
                                                        for r in range(n_r)]) for g in range(n_g)], axis=0).T
    jn = lax.broadcasted_iota(jnp.int32, (LANES, n_g * tq), 0)
    tpos = t0 + lax.broadcasted_iota(jnp.int32, (LANES, tq), 1)
    blk_t = jnp.concatenate([tpos] * n_g, axis=1) // SEL_BLOCK
    valid = jn <= blk_t
    forced = (jn == 0) | (jn == blk_t) | (jn == blk_t - 1)
    others = _pick_topk_rows(jnp.where(valid & jnp.logical_not(forced), imp_t, -jnp.inf), min(SEL_TOPK, ns) - 3)
    keep = valid & (forced | (others > 0.5))

    lhs_ref[:, 0:LANES] = stack(qr_ref)
    wk = WINDOW + tq
    sw_ref, pw_ref = s_ref.at[:, 0:wk], p_ref.at[:, 0:wk]
    kstart = pl.multiple_of(jnp.clip(t0 - WINDOW, 0, seq - wk), tq)
    diff = (t0 + lax.broadcasted_iota(jnp.int32, (tq, wk), 0)) - (
        kstart + lax.broadcasted_iota(jnp.int32, (tq, wk), 1))
    sw_ref[...] = lax.dot_general(lhs_ref[:, 0:LANES], kv4_ref[0, pl.ds(kstart, wk), 2 * LANES:3 * LANES], NT_DIMS,
                                  preferred_element_type=F32)
    _stream_softmax(sw_ref, pw_ref, neg_stats, jnp.where((diff >= 0) & (diff < WINDOW), 0.0, NEG), tq)
    acc_w = jnp.concatenate([times_values(pw_ref, kv4_ref[0, pl.ds(kstart, wk), 3 * LANES:4 * LANES], g)
                             for g in range(n_g)], axis=0)

    drop = jnp.where(keep, 0.0, -BIG).T.astype(BF16)
    lhs_ref[:, LANES:2 * LANES] = jnp.concatenate(
        [drop[g * tq:(g + 1) * tq] for g in range(n_g) for _ in range(n_r)], axis=0)

    m_ref[...] = neg_stats
    acc_ref[...] = jnp.zeros(acc_ref.shape, F32)

    def sel_tile(k0, width, bias):
        rhs = jnp.concatenate([kv4_ref[0, pl.ds(k0, width), 0:LANES], hot_ref[pl.ds(k0, width), :]], axis=1)
        views = []
        for g in range(n_g):
            gs = slice(g * half, (g + 1) * half)
            sg_ref = s_ref.at[gs, 0:width]
            sg_ref[...] = lax.dot_general(lhs_ref[gs, :], rhs, NT_DIMS, preferred_element_type=F32)
            views.append((gs, sg_ref, p_ref.at[gs, 0:width]))
        for g, (gs, sg_ref, pg_ref) in enumerate(views):
            m_old = m_ref[gs, :]
            m_new, _ = _stream_softmax(sg_ref, pg_ref, m_old, bias, tq)
            m_ref[gs, :] = m_new
            vl = lax.broadcasted_iota(jnp.int32, (width, LANES), 1)
            v = kv4_ref[0, pl.ds(k0, width), LANES:2 * LANES]
            vg = jnp.where((vl >= g * HEAD_DIM) & (vl < (g + 1) * HEAD_DIM), v, jnp.ones_like(v))
            acc_ref[gs, :] = jnp.exp2(m_old - m_new) * acc_ref[gs, :] + jnp.dot(pg_ref[...], vg,
                                                                                preferred_element_type=F32)

    last = t0 // tk

    @pl.loop(0, last)
    def _(kt):
        sel_tile(pl.multiple_of(kt * tk, tk), tk, None)

    k_last = pl.multiple_of(last * tk, tk)

    def last_tile(width):
        kpos = k_last + lax.broadcasted_iota(jnp.int32, (tq, width), 1)
        tq_s = t0 + lax.broadcasted_iota(jnp.int32, (tq, width), 0)
        sel_tile(k_last, width, jnp.where(kpos <= tq_s, 0.0, NEG))

    in_first_half = t0 - k_last + tq <= tk // 2
    pl.when(in_first_half)(lambda: last_tile(tk // 2))
    pl.when(jnp.logical_not(in_first_half))(lambda: last_tile(tk))

    acc_s = acc_ref[...]

    gates_b = gates.astype(BF16)
    spread = [jnp.dot(gates_b, gexp_ref[c], preferred_element_type=F32) for c in range(3)]
    first_half = lane < HEAD_DIM
    for r in range(n_r):
        rows_g = [slice((g * n_r + r) * tq, (g * n_r + r + 1) * tq) for g in range(n_g)]
        y = spread[0][:, r * LANES:(r + 1) * LANES] * jnp.where(first_half, o_c[rows_g[0]], o_c[rows_g[1]])
        for c, acc in ((1, acc_s), (2, acc_w)):
            vals = jnp.where(first_half, acc[rows_g[0]], acc[rows_g[1]])
            dens = pltpu.roll(jnp.where(first_half, acc[rows_g[1]], acc[rows_g[0]]), HEAD_DIM, axis=1)
            y = y + spread[c][:, r * LANES:(r + 1) * LANES] * (vals / dens)
        o_ref[0, :, r * LANES:(r + 1) * LANES] = y.astype(BF16)


def _nsa(qn, qr, kv4, cmp, gates, *, tk):
    bsz, s, _ = qn.shape
    tq = NSA_TQ
    ns = s // SEL_BLOCK
    assert ns <= LANES
    assert min(SEL_TOPK, ns) >= 3
    ovl = jnp.pad(_overlap(cmp.shape[2], ns), ((0, 0), (0, LANES - ns))).astype(BF16)
    hot = _block_one_hot(s, SEL_BLOCK)
    head_of_p = _nsa_head_perm() // HEAD_DIM
    gexp = (jnp.arange(LANES)[None, :, None] == 3 * head_of_p[None, None, :] + jnp.arange(3)[:, None, None]).astype(BF16)
    rows = NSA_HEADS * tq
    width = max(cmp.shape[2], WINDOW + tq, tk)
    row = lambda b, i: (b, i, 0)
    return pl.pallas_call(
        functools.partial(_nsa_kernel, seq=s, tk=tk),
        out_shape=jax.ShapeDtypeStruct((bsz, s, NSA_QD), BF16),
        grid=(bsz, s // tq),
        in_specs=[
            pl.BlockSpec((1, tq, NSA_QD), row),
            pl.BlockSpec((1, tq, NSA_QD), row),
            pl.BlockSpec((1, s, 4 * NSA_KVD), lambda b, i: (b, 0, 0), pipeline_mode=pl.Buffered(1)),
            pl.BlockSpec((1, 2) + cmp.shape[2:], lambda b, i: (b, 0, 0, 0), pipeline_mode=pl.Buffered(1)),
            pl.BlockSpec((1, tq, LANES), row),
            _resident(ovl.shape),
            _resident(hot.shape),
            _resident(gexp.shape),
        ],
        out_specs=pl.BlockSpec((1, tq, NSA_QD), row),
        scratch_shapes=[pltpu.VMEM((rows, 2 * LANES), BF16), pltpu.VMEM((rows, width), F32),
                        pltpu.VMEM((rows, width), BF16),
                        pltpu.VMEM((rows, LANES), F32), pltpu.VMEM((rows, LANES), F32)],
        compiler_params=_params(("parallel", "parallel")),
        name="nsa_attn",
    )(qn, qr, kv4, cmp, gates, ovl, hot, gexp)


def _rope_tables(s):
    half = ROT_DIM // 2
    pos = jnp.arange(s, dtype=F32)
    freqs = ROPE_THETA ** (-jnp.arange(half, dtype=F32) * 2.0 / ROT_DIM)
    ang = pos[:, None] * freqs[None, :]
    cos, sin = jnp.cos(ang), jnp.sin(ang)
    ones = jnp.ones((s, HEAD_DIM - ROT_DIM), F32)
    zeros = jnp.zeros((s, HEAD_DIM - ROT_DIM), F32)
    zh = jnp.zeros((s, half), F32)
    cos_t = jnp.concatenate([cos, cos, ones], axis=1)
    sa_t = jnp.concatenate([-sin, zh, zeros], axis=1)
    sb_t = jnp.concatenate([zh, sin, zeros], axis=1)
    return tuple(jnp.tile(t, (1, LANES // HEAD_DIM)) for t in (cos_t, sa_t, sb_t))


def _block_one_hot(s, block):
    return (jnp.arange(s)[:, None] // block == jnp.arange(LANES)[None, :]).astype(BF16)


def _block_diag(w):
    n, a, b = w.shape
    eye = jnp.eye(n, dtype=w.dtype)
    return (eye[:, None, :, None] * w[:, :, None, :]).reshape(n * a, n * b)


def _nsa_head_perm():
    r = jnp.arange(NSA_HPG)[:, None, None]
    g = jnp.arange(NSA_KV_GROUPS)[None, :, None]
    dd = jnp.arange(HEAD_DIM)[None, None, :]
    return (HEAD_DIM * (NSA_HPG * g + r) + dd).reshape(-1)


def _overlap(nrow, ns):
    ci = jnp.arange(nrow)[:, None] * CMP_STRIDE
    sj = jnp.arange(ns)[None, :] * SEL_BLOCK
    ov = jnp.clip(jnp.minimum(ci + CMP_LEN, sj + SEL_BLOCK) - jnp.maximum(ci, sj), 0, None)
    return ov.astype(F32) / CMP_LEN


def kernel(x, c, mod_w, mod_b, norm_g, ffn_w1, ffn_w2, mix0_in_w, lru_conv_w, lru_conv_b, lru_wa, lru_ba,
           lru_wx, lru_bx, lru_lambda, mix0_out_w, mix1_in_w, cmp_pos, cmp_w1, cmp_w2, mix1_out_w, final_norm_g):
    bsz, s, d = x.shape
    depth = mod_w.shape[0]
    tm = 512
    assert d == D_MODEL and s % (2 * tm) == 0 and s >= WINDOW + NSA_TQ

    mod = _modulation(c, mod_w, mod_b)
    tabs = _rope_tables(s)
    w1 = ffn_w1.astype(BF16)
    w2 = ffn_w2.astype(BF16)

    for l in range(depth):
        j = l // 2
        x = _ffn(x, mod[l], norm_g[l, 0], w1, w2, (l, 0), final_norm_g, row0=0, final=False, tm=tm)
        if l % 2 == 0:
            xg, q, k, v, cent = _inproj0(x, mod[l], norm_g[l, 1], mix0_in_w[j].astype(BF16), tabs, tm=tm)
            y_lru = _lru(xg, lru_conv_w[j], lru_conv_b[j],
                         _block_diag(lru_wa[j]).astype(BF16), lru_ba[j].reshape(-1),
                         _block_diag(lru_wx[j]).astype(BF16), lru_bx[j].reshape(-1), lru_lambda[j], tm=256)
            y_att = _moba(q, k, v, cent.reshape(bsz, -1, MOBA_WIDTH))
            wo = mix0_out_w[j].astype(BF16)
            mix = ((y_lru, wo[:LRU_WIDTH]), (y_att, wo[LRU_WIDTH:]))
        else:
            perm = _nsa_head_perm()
            w_in = mix1_in_w[j]
            w_in = jnp.concatenate([w_in[:, :NSA_QD][:, perm], w_in[:, NSA_QD:]], axis=1)
            w_in = jnp.pad(w_in, ((0, 0), (0, IN1_PAD - w_in.shape[1]))).astype(BF16)
            qn, qr, kvc, kv4, gates = _inproj1(x, mod[l], norm_g[l, 1], w_in, tabs, tm=2 * tm)
            cmp = _compress(kvc, cmp_pos[j], cmp_w1[j], cmp_w2[j])
            y = _nsa(qn, qr, kv4, cmp, gates, tk=512)
            mix = ((y, mix1_out_w[j][perm].astype(BF16)),)
        x = _ffn(x, mod[l], norm_g[l, 2], w1, w2, (l, 1), final_norm_g,
                 row0=6, final=(l == depth - 1), tm=tm, mix=mix)
    return x
```

```python
import functools
import math

import jax
import jax.numpy as jnp
from jax import lax
from jax.experimental import pallas as pl
from jax.experimental.pallas import tpu as pltpu

F32 = jnp.float32
BF16 = jnp.bfloat16

D_MODEL = 1024
EPS = 1e-6
ROPE_THETA = 500000.0
HEAD_DIM = 64
ROT_DIM = HEAD_DIM // 4
NEG = -1e30
BIG = 1e30
LANES = 128

LRU_WIDTH = D_MODEL // 2
LRU_BLOCKS = 8
CONV_WIDTH = 4
LRU_C = 8.0
MOBA_WIDTH = D_MODEL // 2
MOBA_BLOCK = 256
MOBA_TOPK = 3
MOBA_TQ = 2 * MOBA_BLOCK
MOBA_CHUNKS = 4

NSA_HEADS = D_MODEL // HEAD_DIM
NSA_KV_GROUPS = 2
NSA_HPG = NSA_HEADS // NSA_KV_GROUPS
CMP_LEN = 32
CMP_STRIDE = 16
SEL_BLOCK = 64
SEL_TOPK = 16
WINDOW = 512
NSA_QD = NSA_HEADS * HEAD_DIM
NSA_KVD = NSA_KV_GROUPS * HEAD_DIM
IN1_PAD = NSA_QD + 6 * NSA_KVD + LANES
NSA_TQ = 128

VMEM_LIMIT = 52 * 1024 * 1024

NT_DIMS = (((1,), (1,)), ((), ()))
Q_PRESCALE = HEAD_DIM ** -0.5 * math.log2(math.e)


def _params(sem):
    return pltpu.CompilerParams(dimension_semantics=sem, vmem_limit_bytes=VMEM_LIMIT)


def _resident(shape):
    nd = len(shape)
    return pl.BlockSpec(shape, lambda *_: (0,) * nd, pipeline_mode=pl.Buffered(1))


def _sigmoid(z):
    return 0.5 * jnp.tanh(0.5 * z) + 0.5


def _silu(a):
    return a * jax.nn.sigmoid(a)


def _gelu_tanh(x):
    cdf = 0.5 * (1.0 + jnp.tanh(math.sqrt(2.0 / math.pi) * (x + 0.044715 * (x * x * x))))
    return x * cdf


def _rmsnorm(x, g):
    return x * lax.rsqrt(jnp.mean(x * x, axis=-1, keepdims=True) + EPS) * g


def _norm_mod(x, g, shift, scale):
    return _rmsnorm(x, g) * (1.0 + scale) + shift


def _rope(u, cos, sa, sb):
    outs = []
    for j in range(u.shape[1] // LANES):
        c = u[:, j * LANES:(j + 1) * LANES]
        up = pltpu.roll(c, LANES - ROT_DIM // 2, axis=1)
        dn = pltpu.roll(c, ROT_DIM // 2, axis=1)
        outs.append(c * cos + up * sa + dn * sb)
    return jnp.concatenate(outs, axis=1)


def _bf16_pieces(x):
    hi = x.astype(BF16)
    r1 = x - hi.astype(F32)
    mid = r1.astype(BF16)
    lo = (r1 - mid.astype(F32)).astype(BF16)
    return jnp.concatenate([hi, mid, lo], axis=0)


SOFTMAX_CHUNK = 64


def _stream_softmax(s_ref, p_ref, m_old, bias, tq):
    rows, n = s_ref.shape
    n_chunks, n_l = rows // SOFTMAX_CHUNK, n // LANES

    def chunk(c):
        sl = slice(c * SOFTMAX_CHUNK, (c + 1) * SOFTMAX_CHUNK)
        sc = s_ref[sl, :]
        if bias is not None:
            q0 = (c * SOFTMAX_CHUNK) % tq
            sc = sc + bias[q0:q0 + SOFTMAX_CHUNK]
        return sl, [sc[:, j * LANES:(j + 1) * LANES] for j in range(n_l)]

    partial_max = jnp.concatenate([functools.reduce(jnp.maximum, chunk(c)[1]) for c in range(n_chunks)], axis=0)
    m_new = jnp.maximum(m_old, jnp.broadcast_to(jnp.max(partial_max, axis=1, keepdims=True), (rows, LANES)))
    sums = []
    for c in range(n_chunks):
        sl, cols = chunk(c)
        m_c = m_new[sl]
        ps = [jnp.exp2(col - m_c) for col in cols]
        for j, pj in enumerate(ps):
            p_ref[sl, j * LANES:(j + 1) * LANES] = pj.astype(BF16)
        sums.append(functools.reduce(jnp.add, ps))
    return m_new, jnp.concatenate(sums, axis=0)


def _pick_topk_rows(g, k):
    n_ids = g.shape[0]
    ids = lax.broadcasted_iota(jnp.int32, g.shape, 0).astype(F32)
    picked = jnp.zeros(g.shape, F32)
    for _ in range(k):
        mx = jnp.max(g, axis=0, keepdims=True)
        first = jnp.min(jnp.where(g == mx, ids, float(n_ids)), axis=0, keepdims=True)
        hit = ids == first
        picked = jnp.where(hit, 1.0, picked)
        g = jnp.where(hit, -jnp.inf, g)
    return picked


def _mod_kernel(c_ref, w_ref, b_ref, o_ref):
    cond = _silu(c_ref[...])
    o_ref[0] = jnp.dot(cond.astype(BF16), w_ref[0].astype(BF16), preferred_element_type=F32) + b_ref[0]


def _modulation(c, mod_w, mod_b):
    depth, d, n = mod_w.shape
    bsz = c.shape[0]
    bp = -(-bsz // 8) * 8
    cp = jnp.pad(c, ((0, bp - bsz), (0, 0)))
    tn = 1152
    out = pl.pallas_call(
        _mod_kernel,
        out_shape=jax.ShapeDtypeStruct((depth, bp, n), F32),
        grid=(depth, n // tn),
        in_specs=[
            pl.BlockSpec((bp, d), lambda l, j: (0, 0)),
            pl.BlockSpec((1, d, tn), lambda l, j: (l, 0, j)),
            pl.BlockSpec((1, 1, tn), lambda l, j: (l, 0, j)),
        ],
        out_specs=pl.BlockSpec((1, bp, tn), lambda l, j: (l, 0, j)),
        compiler_params=_params(("parallel", "parallel")),
        name="adaln_mod",
    )(cp, mod_w, mod_b.reshape(depth, 1, n))
    return out[:, :bsz].reshape(depth, bsz, 9, d)


def _ffn_kernel(x_ref, mod_ref, g_ref, w1_ref, w2_ref, gf_ref, *rest, row0, final, n_mix):
    o_ref = rest[-1]
    x = x_ref[0]
    m = mod_ref[0]
    if n_mix:
        mixed = sum(jnp.dot(y_ref[0], w_ref[...], preferred_element_type=F32)
                    for y_ref, w_ref in zip(rest[:n_mix], rest[n_mix:2 * n_mix]))
        x = x + m[5:6] * mixed
    h = _norm_mod(x, g_ref[...], m[row0:row0 + 1], m[row0 + 1:row0 + 2]).astype(BF16)
    d_ff = w2_ref.shape[2]
    a = jnp.dot(h, w1_ref[0, 0, :, 0:d_ff], preferred_element_type=F32)
    b = jnp.dot(h, w1_ref[0, 0, :, d_ff:2 * d_ff], preferred_element_type=F32)
    act = (_silu(a) * b).astype(BF16)
    y = jnp.dot(act, w2_ref[0, 0], preferred_element_type=F32)
    out = x + (0.5 * m[row0 + 2:row0 + 3]) * y
    if final:
        out = _rmsnorm(out, gf_ref[...])
    o_ref[0] = out


def _ffn(x, mod_l, g, w1, w2, which, gf, *, row0, final, tm, mix=()):
    bsz, s, d = x.shape
    picked = lambda shape: pl.BlockSpec((1, 1) + shape[2:], lambda b, i: which + (0, 0),
                                        pipeline_mode=pl.Buffered(1))
    kern = functools.partial(_ffn_kernel, row0=row0, final=final, n_mix=len(mix))
    row = lambda b, i: (b, i, 0)
    return pl.pallas_call(
        kern,
        out_shape=jax.ShapeDtypeStruct(x.shape, F32),
        grid=(bsz, s // tm),
        in_specs=[
            pl.BlockSpec((1, tm, d), row),
            pl.BlockSpec((1, 9, d), lambda b, i: (b, 0, 0)),
            _resident((1, d)),
            picked(w1.shape),
            picked(w2.shape),
            _resident((1, d)),
        ] + [pl.BlockSpec((1, tm, y.shape[2]), row) for y, _ in mix] + [_resident(w.shape) for _, w in mix],
        out_specs=pl.BlockSpec((1, tm, d), row),
        compiler_params=_params(("parallel", "parallel")),
        name="ffn_final" if final else ("mix_ffn" if mix else "ffn"),
    )(x, mod_l, g.reshape(1, d), w1, w2, gf.reshape(1, d), *[y for y, _ in mix], *[w for _, w in mix])


def _inproj0_kernel(x_ref, mod_ref, g_ref, w_ref, cos_ref, sa_ref, sb_ref,
                    cw_ref, cb_ref, wa_ref, ba_ref, wx_ref, bx_ref, lam_ref,
                    y_ref, q_ref, k_ref, v_ref, cent_ref, xbuf, hprev):
    @pl.when(pl.program_id(1) == 0)
    def _():
        xbuf[0:LRU_HALO, :] = jnp.zeros((LRU_HALO, LRU_WIDTH), F32)
        hprev[...] = jnp.zeros(hprev.shape, F32)

    m = mod_ref[0]
    h = _norm_mod(x_ref[0], g_ref[...], m[3:4], m[4:5]).astype(BF16)
    u = jnp.dot(h, w_ref[...], preferred_element_type=F32)
    o1 = 2 * LRU_WIDTH
    sub = xbuf.shape[0] - LRU_HALO
    for j in range(u.shape[0] // sub):
        rs = slice(j * sub, (j + 1) * sub)
        y_ref[0, rs, :] = _lru_rows(u[rs, 0:LRU_WIDTH], u[rs, LRU_WIDTH:o1], cw_ref, cb_ref, wa_ref, ba_ref,
                                    wx_ref, bx_ref, lam_ref, xbuf, hprev)
    cos, sa, sb = cos_ref[...], sa_ref[...], sb_ref[...]
    q = _rope(u[:, o1:o1 + MOBA_WIDTH], cos, sa, sb)
    k = _rope(u[:, o1 + MOBA_WIDTH:o1 + 2 * MOBA_WIDTH], cos, sa, sb)
    q_ref[0] = (q * Q_PRESCALE).astype(BF16)
    k_ref[0] = k.astype(BF16)
    v_ref[0] = u[:, o1 + 2 * MOBA_WIDTH:].astype(BF16)
    for j in range(k.shape[0] // MOBA_BLOCK):
        cent_ref[0, j] = jnp.mean(k[j * MOBA_BLOCK:(j + 1) * MOBA_BLOCK], axis=0, keepdims=True)


def _inproj0(x, mod_l, g, w, rope_tabs, lru, *, tm, lru_rows):
    bsz, s, d = x.shape
    assert tm % MOBA_BLOCK == 0
    assert tm % lru_rows == 0
    nb = s // MOBA_BLOCK
    n = w.shape[1]
    c = LRU_WIDTH
    cw, cb, wa_bd, ba, wx_bd, bx, lam = lru
    vec = _resident((1, c))
    row = lambda b, i: (b, i, 0)
    tab = pl.BlockSpec((tm, LANES), lambda b, i: (i, 0))
    return pl.pallas_call(
        _inproj0_kernel,
        out_shape=(
            jax.ShapeDtypeStruct((bsz, s, LRU_WIDTH), BF16),
            jax.ShapeDtypeStruct((bsz, s, MOBA_WIDTH), BF16),
            jax.ShapeDtypeStruct((bsz, s, MOBA_WIDTH), BF16),
            jax.ShapeDtypeStruct((bsz, s, MOBA_WIDTH), BF16),
            jax.ShapeDtypeStruct((bsz, nb, 1, MOBA_WIDTH), F32),
        ),
        grid=(bsz, s // tm),
        in_specs=[
            pl.BlockSpec((1, tm, d), row),
            pl.BlockSpec((1, 9, d), lambda b, i: (b, 0, 0)),
            _resident((1, d)),
            _resident((d, n)),
            tab, tab, tab,
            _resident((CONV_WIDTH, c)), vec,
            _resident((c, c)), vec,
            _resident((c, c)), vec,
            vec,
        ],
        out_specs=(
            pl.BlockSpec((1, tm, LRU_WIDTH), row),
            pl.BlockSpec((1, tm, MOBA_WIDTH), row),
            pl.BlockSpec((1, tm, MOBA_WIDTH), row),
            pl.BlockSpec((1, tm, MOBA_WIDTH), row),
            pl.BlockSpec((1, tm // MOBA_BLOCK, 1, MOBA_WIDTH), lambda b, i: (b, i, 0, 0)),
        ),
        scratch_shapes=[pltpu.VMEM((LRU_HALO + lru_rows, c), F32), pltpu.VMEM((8, c), F32)],
        compiler_params=_params(("parallel", "arbitrary")),
        name="inproj0_lru",
    )(x, mod_l, g.reshape(1, d), w, *rope_tabs, cw, cb.reshape(1, c), wa_bd, ba.reshape(1, c),
      wx_bd, bx.reshape(1, c), lam.reshape(1, c))


LRU_HALO = 8


def _lru_rows(x, gate_in, cw_ref, cb_ref, wa_ref, ba_ref, wx_ref, bx_ref, lam_ref, xbuf, hprev):
    tm, c = x.shape
    halo = LRU_HALO
    xbuf[halo:halo + tm, :] = x
    cw = cw_ref[...]
    xc = cb_ref[...] + cw[CONV_WIDTH - 1:CONV_WIDTH] * x
    for kk in range(CONV_WIDTH - 1):
        back = CONV_WIDTH - 1 - kk
        xc = xc + cw[kk:kk + 1] * xbuf[halo - back:halo - back + tm, :]
    xbuf[0:halo, :] = x[tm - halo:tm, :]

    xcb = xc.astype(BF16)
    r = _sigmoid(jnp.dot(xcb, wa_ref[...], preferred_element_type=F32) + ba_ref[...])
    ig = _sigmoid(jnp.dot(xcb, wx_ref[...], preferred_element_type=F32) + bx_ref[...])
    nlam = -lam_ref[...]
    softplus = jnp.maximum(nlam, 0.0) + jnp.log1p(jnp.exp(-jnp.abs(nlam)))
    log_a = (-LRU_C * r) * softplus
    a = jnp.exp(log_a)
    gain2 = -jnp.tanh(log_a) * (a * a + 1.0)
    mult = jnp.where(gain2 > 0.0, gain2 * lax.rsqrt(gain2), 0.0)
    bt = mult * (ig * xc)

    row = lax.broadcasted_iota(jnp.int32, (tm, c), 0)
    step = 1
    while step < tm:
        a_sh = pltpu.roll(a, step, axis=0)
        b_sh = pltpu.roll(bt, step, axis=0)
        valid = row >= step
        bt = jnp.where(valid, a * b_sh + bt, bt)
        a = jnp.where(valid, a * a_sh, a)
        step *= 2
    h = bt + a * hprev[0:1, :]
    hprev[0:1, :] = h[tm - 1:tm, :]
    return (h * _gelu_tanh(gate_in)).astype(BF16)


def _moba_kernel(q_ref, k_ref, v_ref, cent_ref, hot_ref, o_ref, lhs_ref, s_ref, p_ref, m_ref, l_ref, acc_ref):
    tq = MOBA_TQ
    rows = 2 * tq
    nb_pad = -(-(k_ref.shape[1] // MOBA_BLOCK) // 8) * 8
    i = pl.program_id(2)
    lane = lax.broadcasted_iota(jnp.int32, (tq, LANES), 1)
    bid = lax.broadcasted_iota(jnp.int32, (LANES, rows), 0)
    qidx = jnp.concatenate([lax.broadcasted_iota(jnp.int32, (LANES, tq), 1)] * 2, axis=1)
    own = (tq // MOBA_BLOCK) * i + qidx // MOBA_BLOCK
    past = bid < own
    qidx_r = jnp.concatenate([lax.broadcasted_iota(jnp.int32, (nb_pad, tq), 1)] * 2, axis=1)
    past_real = lax.broadcasted_iota(jnp.int32, (nb_pad, rows), 0) < (tq // MOBA_BLOCK) * i + qidx_r // MOBA_BLOCK
    chunks = [slice(c * LANES, (c + 1) * LANES) for c in range(MOBA_CHUNKS)]

    for c, cl in enumerate(chunks):
        q = q_ref[0, :, cl]
        zero = jnp.zeros_like(q)
        q2 = jnp.concatenate([jnp.where(lane < HEAD_DIM, q, zero), jnp.where(lane >= HEAD_DIM, q, zero)], axis=0)
        gate3 = lax.dot_general(_bf16_pieces(cent_ref[0, :, cl]), q2, NT_DIMS, preferred_element_type=F32)
        gate_t = gate3[:LANES] + gate3[LANES:2 * LANES] + gate3[2 * LANES:]
        sel_t = _pick_topk_rows(jnp.where(past_real, gate_t[:nb_pad], -jnp.inf), MOBA_TOPK)
        sel_t = jnp.concatenate([sel_t, jnp.zeros((LANES - nb_pad, rows), F32)], axis=0)
        drop = jnp.where((past & (sel_t > 0.5)) | (bid == own), 0.0, -BIG).T.astype(BF16)
        lhs_ref[c] = jnp.concatenate([q2, drop], axis=1)
        m_ref[c] = jnp.full((rows, LANES), NEG, F32)
        l_ref[c] = jnp.zeros((rows, LANES), F32)
        acc_ref[c] = jnp.zeros((rows, LANES), F32)

    def tile(kt, bias):
        k0 = pl.multiple_of(kt * tq, tq)
        for c, cl in enumerate(chunks):
            rhs = jnp.concatenate([k_ref[0, pl.ds(k0, tq), cl], hot_ref[pl.ds(k0, tq), :]], axis=1)
            s_ref[c] = lax.dot_general(lhs_ref[c], rhs, NT_DIMS, preferred_element_type=F32)
        for c, cl in enumerate(chunks):
            m_old = m_ref[c]
            m_new, p_sum = _stream_softmax(s_ref.at[c], p_ref.at[c], m_old, bias, tq)
            alpha = jnp.exp2(m_old - m_new)
            m_ref[c] = m_new
            l_ref[c] = alpha * l_ref[c] + p_sum
            acc_ref[c] = alpha * acc_ref[c] + jnp.dot(p_ref[c], v_ref[0, pl.ds(k0, tq), cl],
                                                      preferred_element_type=F32)

    @pl.when(i >= 0)
    def _():
        qpos = lax.broadcasted_iota(jnp.int32, (tq, tq), 0)
        kpos = lax.broadcasted_iota(jnp.int32, (tq, tq), 1)
        tile(i, jnp.where(kpos <= qpos, 0.0, NEG))

    @pl.loop(0, i)
    def _(kt):
        tile(kt, None)

    for c, cl in enumerate(chunks):
        out = acc_ref[c] / jnp.sum(l_ref[c], axis=1, keepdims=True)
        o_ref[0, :, cl] = jnp.where(lane < HEAD_DIM, out[:tq], out[tq:]).astype(BF16)


def _moba(q, k, v, cent):
    bsz, s, w = q.shape
    tq = MOBA_TQ
    nb = s // MOBA_BLOCK
    assert nb <= LANES and s % tq == 0
    cent = jnp.pad(cent, ((0, 0), (0, LANES - nb), (0, 0)))
    hot = _block_one_hot(s, MOBA_BLOCK)
    nc, cw = MOBA_CHUNKS, MOBA_CHUNKS * LANES
    return pl.pallas_call(
        _moba_kernel,
        out_shape=jax.ShapeDtypeStruct((bsz, s, w), BF16),
        grid=(bsz, w // cw, s // tq),
        in_specs=[
            pl.BlockSpec((1, tq, cw), lambda b, hp, i: (b, i, hp)),
            pl.BlockSpec((1, s, cw), lambda b, hp, i: (b, 0, hp), pipeline_mode=pl.Buffered(1)),
            pl.BlockSpec((1, s, cw), lambda b, hp, i: (b, 0, hp), pipeline_mode=pl.Buffered(1)),
            pl.BlockSpec((1, LANES, cw), lambda b, hp, i: (b, 0, hp)),
            _resident(hot.shape),
        ],
        out_specs=pl.BlockSpec((1, tq, cw), lambda b, hp, i: (b, i, hp)),
        scratch_shapes=[pltpu.VMEM((nc, 2 * tq, 2 * LANES), BF16), pltpu.VMEM((nc, 2 * tq, tq), F32),
                        pltpu.VMEM((nc, 2 * tq, tq), BF16)] + [pltpu.VMEM((nc, 2 * tq, LANES), F32)] * 3,
        compiler_params=_params(("parallel", "parallel", "parallel")),
        name="moba_attn",
    )(q, k, v, cent, hot)


def _inproj1_kernel(x_ref, mod_ref, g_ref, w_ref, cos_ref, sa_ref, sb_ref,
                    qn_ref, qr_ref, kvc_ref, kv4_ref, gate_ref):
    m = mod_ref[0]
    h = _norm_mod(x_ref[0], g_ref[...], m[3:4], m[4:5]).astype(BF16)
    u = jnp.dot(h, w_ref[...], preferred_element_type=F32)
    cos, sa, sb = cos_ref[...], sa_ref[...], sb_ref[...]
    q = u[:, :NSA_QD] * Q_PRESCALE
    qn_ref[0] = q.astype(BF16)
    qr_ref[0] = _rope(q, cos, sa, sb).astype(BF16)
    o = NSA_QD
    kvc_ref[0, 0] = u[:, o:o + NSA_KVD]
    kvc_ref[0, 1] = u[:, o + NSA_KVD:o + 2 * NSA_KVD]
    o += 2 * NSA_KVD
    ks = _rope(u[:, o:o + NSA_KVD], cos, sa, sb)
    vs = u[:, o + NSA_KVD:o + 2 * NSA_KVD]
    kw = _rope(u[:, o + 2 * NSA_KVD:o + 3 * NSA_KVD], cos, sa, sb)
    vw = u[:, o + 3 * NSA_KVD:o + 4 * NSA_KVD]
    kv4_ref[0] = jnp.concatenate([ks, vs, kw, vw], axis=1).astype(BF16)
    gate_ref[0] = jax.nn.sigmoid(u[:, o + 4 * NSA_KVD:])


def _inproj1(x, mod_l, g, w, rope_tabs, *, tm):
    bsz, s, d = x.shape
    n = w.shape[1]
    row = lambda b, i: (b, i, 0)
    tab = pl.BlockSpec((tm, LANES), lambda b, i: (i, 0))
    widths = (NSA_QD, NSA_QD, None, 4 * NSA_KVD, LANES)
    dtypes = (BF16, BF16, F32, BF16, F32)
    shape = lambda wd: (bsz, s, wd) if wd else (bsz, 2, s, NSA_KVD)
    spec = lambda wd: pl.BlockSpec((1, tm, wd), row) if wd else pl.BlockSpec((1, 2, tm, NSA_KVD),
                                                                             lambda b, i: (b, 0, i, 0))
    return pl.pallas_call(
        _inproj1_kernel,
        out_shape=tuple(jax.ShapeDtypeStruct(shape(wd), dt) for wd, dt in zip(widths, dtypes)),
        grid=(bsz, s // tm),
        in_specs=[
            pl.BlockSpec((1, tm, d), row),
            pl.BlockSpec((1, 9, d), lambda b, i: (b, 0, 0)),
            _resident((1, d)),
            _resident((d, n)),
            tab, tab, tab,
        ],
        out_specs=tuple(spec(wd) for wd in widths),
        compiler_params=_params(("parallel", "parallel")),
        name="inproj1",
    )(x, mod_l, g.reshape(1, d), w, *rope_tabs)


def _compress_kernel(kvc_ref, pos_ref, w1_ref, w2_ref, o_ref):
    nrow = o_ref.shape[2]
    for kv in range(2):
        top = jnp.zeros((nrow, w1_ref.shape[3]), F32)
        bot = jnp.zeros((nrow, w1_ref.shape[3]), F32)
        for r in range(CMP_STRIDE):
            x = kvc_ref[0, kv, pl.ds(r, nrow, stride=CMP_STRIDE), :]
            top += jnp.dot((x + pos_ref[kv, r:r + 1]).astype(BF16), w1_ref[kv, r], preferred_element_type=F32)
            bot += jnp.dot((x + pos_ref[kv, CMP_STRIDE + r:CMP_STRIDE + r + 1]).astype(BF16),
                           w1_ref[kv, CMP_STRIDE + r], preferred_element_type=F32)
        pre = top + pltpu.roll(bot, nrow - 1, axis=0)
        o_ref[0, kv] = jnp.dot(_gelu_tanh(pre).astype(BF16), w2_ref[kv], preferred_element_type=F32).astype(BF16)


def _compress(kvc, pos, w1, w2):
    bsz, _, s, _ = kvc.shape
    nrow = s // CMP_STRIDE
    eye = jnp.eye(NSA_KV_GROUPS, dtype=F32)
    pos2 = jnp.tile(pos, (1, 1, NSA_KV_GROUPS))
    hidden = w1.shape[-1]
    w1r = w1.reshape(2, CMP_LEN, HEAD_DIM, hidden)
    w1bd = jnp.einsum('gh,kric->krgihc', eye, w1r).reshape(2, CMP_LEN, NSA_KVD, NSA_KV_GROUPS * hidden)
    w2bd = jnp.einsum('gh,kcd->kgchd', eye, w2).reshape(2, NSA_KV_GROUPS * hidden, NSA_KVD)
    return pl.pallas_call(
        _compress_kernel,
        out_shape=jax.ShapeDtypeStruct((bsz, 2, nrow, NSA_KVD), BF16),
        grid=(bsz,),
        in_specs=[
            pl.BlockSpec((1, 2, s, NSA_KVD), lambda b: (b, 0, 0, 0)),
            _resident(pos2.shape),
            _resident(w1bd.shape),
            _resident(w2bd.shape),
        ],
        out_specs=pl.BlockSpec((1, 2, nrow, NSA_KVD), lambda b: (b, 0, 0, 0)),
        compiler_params=_params(("parallel",)),
        name="nsa_compress",
    )(kvc, pos2, w1bd.astype(BF16), w2bd.astype(BF16))


def _nsa_kernel(qn_ref, qr_ref, kv4_ref, cmp_ref, gate_ref, ovl_ref, hot_ref, gexp_ref, o_ref,
                lhs_ref, s_ref, p_ref, m_ref, acc_ref, *, seq, tk):
    tq = NSA_TQ
    nrow = cmp_ref.shape[2]
    ns = seq // SEL_BLOCK
    n_g, n_r = NSA_KV_GROUPS, NSA_HPG
    n_h = n_g * n_r
    rows = n_h * tq
    t0 = pl.program_id(1) * tq
    lane = lax.broadcasted_iota(jnp.int32, (tq, LANES), 1)
    gates = gate_ref[0]

    def stack(q_ref):
        parts = []
        for g in range(n_g):
            in_grp = (lane >= g * HEAD_DIM) & (lane < (g + 1) * HEAD_DIM)
            for r in range(n_r):
                ch = q_ref[0, :, r * LANES:(r + 1) * LANES]
                parts.append(jnp.where(in_grp, ch, jnp.zeros_like(ch)))
        return jnp.concatenate(parts, axis=0)

    half = rows // n_g
    neg_stats = jnp.full((rows, LANES), NEG, F32)

    def times_values(p_view, v, g, extra=None):
        vl = lax.broadcasted_iota(jnp.int32, v.shape, 1)
        vg = jnp.where((vl >= g * HEAD_DIM) & (vl < (g + 1) * HEAD_DIM), v, jnp.ones_like(v))
        if extra is not None:
            vg = jnp.concatenate([vg, extra], axis=1)
        return jnp.dot(p_view[g * half:(g + 1) * half, :], vg, preferred_element_type=F32)

    sc_ref, pc_ref = s_ref.at[:, 0:nrow], p_ref.at[:, 0:nrow]
    tq_c = t0 + lax.broadcasted_iota(jnp.int32, (tq, nrow), 0)
    cend = lax.broadcasted_iota(jnp.int32, (tq, nrow), 1) * CMP_STRIDE + (CMP_LEN - 1)
    sc_ref[...] = lax.dot_general(stack(qn_ref), cmp_ref[0, 0], NT_DIMS, preferred_element_type=F32)
    _stream_softmax(sc_ref, pc_ref, neg_stats, jnp.where(cend <= tq_c, 0.0, NEG), tq)
    both = jnp.concatenate([times_values(pc_ref, cmp_ref[0, 1], g, extra=ovl_ref[...]) for g in range(n_g)], axis=0)
    acc_c, imp_h = both[:, 0:LANES], both[:, LANES:]
    lane_r = lax.broadcasted_iota(jnp.int32, (rows, LANES), 1)
    grp_r = lax.broadcasted_iota(jnp.int32, (rows, LANES), 0) // half
    den_c = jnp.where(lane_r // HEAD_DIM == grp_r, pltpu.roll(acc_c, HEAD_DIM, axis=1), acc_c)
    has_block = jnp.where(t0 + lax.broadcasted_iota(jnp.int32, (tq, LANES), 0) >= CMP_LEN - 1, 1.0, 0.0)
    inv_c = jnp.concatenate([has_block] * n_h, axis=0) / den_c
    o_c = acc_c * inv_c

    imp_h = imp_h * inv_c
    imp_t = jnp.concatenate([functools.reduce(jnp.add, [imp_h[(g * n_r + r) * tq:(g * n_r + r + 1) * tq]
                                                        for r in range(n_r)]) for g in range(n_g)], axis=0).T
    jn = lax.broadcasted_iota(jnp.int32, (LANES, n_g * tq), 0)
    tpos = t0 + lax.broadcasted_iota(jnp.int32, (LANES, tq), 1)
    blk_t = jnp.concatenate([tpos] * n_g, axis=1) // SEL_BLOCK
    valid = jn <= blk_t
    forced = (jn == 0) | (jn == blk_t) | (jn == blk_t - 1)
    others = _pick_topk_rows(jnp.where(valid & jnp.logical_not(forced), imp_t, -jnp.inf), min(SEL_TOPK, ns) - 3)
    keep = valid & (forced | (others > 0.5))

    lhs_ref[:, 0:LANES] = stack(qr_ref)
    wk = WINDOW + tq
    sw_ref, pw_ref = s_ref.at[:, 0:wk], p_ref.at[:, 0:wk]
    kstart = pl.multiple_of(jnp.clip(t0 - WINDOW, 0, seq - wk), tq)
    diff = (t0 + lax.broadcasted_iota(jnp.int32, (tq, wk), 0)) - (
        kstart + lax.broadcasted_iota(jnp.int32, (tq, wk), 1))
    sw_ref[...] = lax.dot_general(lhs_ref[:, 0:LANES], kv4_ref[0, pl.ds(kstart, wk), 2 * LANES:3 * LANES], NT_DIMS,
                                  preferred_element_type=F32)
    _stream_softmax(sw_ref, pw_ref, neg_stats, jnp.where((diff >= 0) & (diff < WINDOW), 0.0, NEG), tq)
    acc_w = jnp.concatenate([times_values(pw_ref, kv4_ref[0, pl.ds(kstart, wk), 3 * LANES:4 * LANES], g)
                             for g in range(n_g)], axis=0)

    drop = jnp.where(keep, 0.0, -BIG).T.astype(BF16)
    lhs_ref[:, LANES:2 * LANES] = jnp.concatenate(
        [drop[g * tq:(g + 1) * tq] for g in range(n_g) for _ in range(n_r)], axis=0)

    m_ref[...] = neg_stats
    acc_ref[...] = jnp.zeros(acc_ref.shape, F32)

    def sel_tile(k0, width, bias):
        rhs = jnp.concatenate([kv4_ref[0, pl.ds(k0, width), 0:LANES], hot_ref[pl.ds(k0, width), :]], axis=1)
        views = []
        for g in range(n_g):
            gs = slice(g * half, (g + 1) * half)
            sg_ref = s_ref.at[gs, 0:width]
            sg_ref[...] = lax.dot_general(lhs_ref[gs, :], rhs, NT_DIMS, preferred_element_type=F32)
            views.append((gs, sg_ref, p_ref.at[gs, 0:width]))
        for g, (gs, sg_ref, pg_ref) in enumerate(views):
            m_old = m_ref[gs, :]
            m_new, _ = _stream_softmax(sg_ref, pg_ref, m_old, bias, tq)
            m_ref[gs, :] = m_new
            vl = lax.broadcasted_iota(jnp.int32, (width, LANES), 1)
            v = kv4_ref[0, pl.ds(k0, width), LANES:2 * LANES]
            vg = jnp.where((vl >= g * HEAD_DIM) & (vl < (g + 1) * HEAD_DIM), v, jnp.ones_like(v))
            acc_ref[gs, :] = jnp.exp2(m_old - m_new) * acc_ref[gs, :] + jnp.dot(pg_ref[...], vg,
                                                                                preferred_element_type=F32)

    last = t0 // tk

    @pl.loop(0, last)
    def _(kt):
        sel_tile(pl.multiple_of(kt * tk, tk), tk, None)

    k_last = pl.multiple_of(last * tk, tk)

    def last_tile(width):
        kpos = k_last + lax.broadcasted_iota(jnp.int32, (tq, width), 1)
        tq_s = t0 + lax.broadcasted_iota(jnp.int32, (tq, width), 0)
        sel_tile(k_last, width, jnp.where(kpos <= tq_s, 0.0, NEG))

    in_first_half = t0 - k_last + tq <= tk // 2
    pl.when(in_first_half)(lambda: last_tile(tk // 2))
    pl.when(jnp.logical_not(in_first_half))(lambda: last_tile(tk))

    acc_s = acc_ref[...]

    gates_b = gates.astype(BF16)
    spread = [jnp.dot(gates_b, gexp_ref[c], preferred_element_type=F32) for c in range(3)]
    first_half = lane < HEAD_DIM
    for r in range(n_r):
        rows_g = [slice((g * n_r + r) * tq, (g * n_r + r + 1) * tq) for g in range(n_g)]
        y = spread[0][:, r * LANES:(r + 1) * LANES] * jnp.where(first_half, o_c[rows_g[0]], o_c[rows_g[1]])
        for c, acc in ((1, acc_s), (2, acc_w)):
            vals = jnp.where(first_half, acc[rows_g[0]], acc[rows_g[1]])
            dens = pltpu.roll(jnp.where(first_half, acc[rows_g[1]], acc[rows_g[0]]), HEAD_DIM, axis=1)
            y = y + spread[c][:, r * LANES:(r + 1) * LANES] * (vals / dens)
        o_ref[0, :, r * LANES:(r + 1) * LANES] = y.astype(BF16)


def _nsa(qn, qr, kv4, cmp, gates, *, tk):
    bsz, s, _ = qn.shape
    tq = NSA_TQ
    ns = s // SEL_BLOCK
    assert ns <= LANES
    assert min(SEL_TOPK, ns) >= 3
    ovl = jnp.pad(_overlap(cmp.shape[2], ns), ((0, 0), (0, LANES - ns))).astype(BF16)
    hot = _block_one_hot(s, SEL_BLOCK)
    head_of_p = _nsa_head_perm() // HEAD_DIM
    gexp = (jnp.arange(LANES)[None, :, None] == 3 * head_of_p[None, None, :] + jnp.arange(3)[:, None, None]).astype(BF16)
    rows = NSA_HEADS * tq
    width = max(cmp.shape[2], WINDOW + tq, tk)
    row = lambda b, i: (b, i, 0)
    return pl.pallas_call(
        functools.partial(_nsa_kernel, seq=s, tk=tk),
        out_shape=jax.ShapeDtypeStruct((bsz, s, NSA_QD), BF16),
        grid=(bsz, s // tq),
        in_specs=[
            pl.BlockSpec((1, tq, NSA_QD), row),
            pl.BlockSpec((1, tq, NSA_QD), row),
            pl.BlockSpec((1, s, 4 * NSA_KVD), lambda b, i: (b, 0, 0), pipeline_mode=pl.Buffered(1)),
            pl.BlockSpec((1, 2) + cmp.shape[2:], lambda b, i: (b, 0, 0, 0), pipeline_mode=pl.Buffered(1)),
            pl.BlockSpec((1, tq, LANES), row),
            _resident(ovl.shape),
            _resident(hot.shape),
            _resident(gexp.shape),
        ],
        out_specs=pl.BlockSpec((1, tq, NSA_QD), row),
        scratch_shapes=[pltpu.VMEM((rows, 2 * LANES), BF16), pltpu.VMEM((rows, width), F32),
                        pltpu.VMEM((rows, width), BF16),
                        pltpu.VMEM((rows, LANES), F32), pltpu.VMEM((rows, LANES), F32)],
        compiler_params=_params(("parallel", "parallel")),
        name="nsa_attn",
    )(qn, qr, kv4, cmp, gates, ovl, hot, gexp)


def _rope_tables(s):
    half = ROT_DIM // 2
    pos = jnp.arange(s, dtype=F32)
    freqs = ROPE_THETA ** (-jnp.arange(half, dtype=F32) * 2.0 / ROT_DIM)
    ang = pos[:, None] * freqs[None, :]
    cos, sin = jnp.cos(ang), jnp.sin(ang)
    ones = jnp.ones((s, HEAD_DIM - ROT_DIM), F32)
    zeros = jnp.zeros((s, HEAD_DIM - ROT_DIM), F32)
    zh = jnp.zeros((s, half), F32)
    cos_t = jnp.concatenate([cos, cos, ones], axis=1)
    sa_t = jnp.concatenate([-sin, zh, zeros], axis=1)
    sb_t = jnp.concatenate([zh, sin, zeros], axis=1)
    return tuple(jnp.tile(t, (1, LANES // HEAD_DIM)) for t in (cos_t, sa_t, sb_t))


def _block_one_hot(s, block):
    return (jnp.arange(s)[:, None] // block == jnp.arange(LANES)[None, :]).astype(BF16)


def _block_diag(w):
    n, a, b = w.shape
    eye = jnp.eye(n, dtype=w.dtype)
    return (eye[:, None, :, None] * w[:, :, None, :]).reshape(n * a, n * b)


def _nsa_head_perm():
    r = jnp.arange(NSA_HPG)[:, None, None]
    g = jnp.arange(NSA_KV_GROUPS)[None, :, None]
    dd = jnp.arange(HEAD_DIM)[None, None, :]
    return (HEAD_DIM * (NSA_HPG * g + r) + dd).reshape(-1)


def _overlap(nrow, ns):
    ci = jnp.arange(nrow)[:, None] * CMP_STRIDE
    sj = jnp.arange(ns)[None, :] * SEL_BLOCK
    ov = jnp.clip(jnp.minimum(ci + CMP_LEN, sj + SEL_BLOCK) - jnp.maximum(ci, sj), 0, None)
    return ov.astype(F32) / CMP_LEN


def kernel(x, c, mod_w, mod_b, norm_g, ffn_w1, ffn_w2, mix0_in_w, lru_conv_w, lru_conv_b, lru_wa, lru_ba,
           lru_wx, lru_bx, lru_lambda, mix0_out_w, mix1_in_w, cmp_pos, cmp_w1, cmp_w2, mix1_out_w, final_norm_g):
    bsz, s, d = x.shape
    depth = mod_w.shape[0]
    tm = 512
    assert d == D_MODEL and s % (2 * tm) == 0 and s >= WINDOW + NSA_TQ

    mod = _modulation(c, mod_w, mod_b)
    tabs = _rope_tables(s)
    w1 = ffn_w1.astype(BF16)
    w2 = ffn_w2.astype(BF16)

    for l in range(depth):
        j = l // 2
        x = _ffn(x, mod[l], norm_g[l, 0], w1, w2, (l, 0), final_norm_g, row0=0, final=False, tm=tm)
        if l % 2 == 0:
            lru = (lru_conv_w[j], lru_conv_b[j], _block_diag(lru_wa[j]).astype(BF16), lru_ba[j].reshape(-1),
                   _block_diag(lru_wx[j]).astype(BF16), lru_bx[j].reshape(-1), lru_lambda[j])
            y_lru, q, k, v, cent = _inproj0(x, mod[l], norm_g[l, 1], mix0_in_w[j].astype(BF16), tabs, lru,
                                            tm=tm, lru_rows=256)
            y_att = _moba(q, k, v, cent.reshape(bsz, -1, MOBA_WIDTH))
            wo = mix0_out_w[j].astype(BF16)
            mix = ((y_lru, wo[:LRU_WIDTH]), (y_att, wo[LRU_WIDTH:]))
        else:
            perm = _nsa_head_perm()
            w_in = mix1_in_w[j]
            w_in = jnp.concatenate([w_in[:, :NSA_QD][:, perm], w_in[:, NSA_QD:]], axis=1)
            w_in = jnp.pad(w_in, ((0, 0), (0, IN1_PAD - w_in.shape[1]))).astype(BF16)
            qn, qr, kvc, kv4, gates = _inproj1(x, mod[l], norm_g[l, 1], w_in, tabs, tm=2 * tm)
            cmp = _compress(kvc, cmp_pos[j], cmp_w1[j], cmp_w2[j])
            y = _nsa(qn, qr, kv4, cmp, gates, tk=512)
            mix = ((y, mix1_out_w[j][perm].astype(BF16)),)
        x = _ffn(x, mod[l], norm_g[l, 2], w1, w2, (l, 1), final_norm_g,
                 row0=6, final=(l == depth - 1), tm=tm, mix=mix)
    return x
```

```python
import functools
import math

import jax
import jax.numpy as jnp
from jax import lax
from jax.experimental import pallas as pl
from jax.experimental.pallas import tpu as pltpu

F32 = jnp.float32
BF16 = jnp.bfloat16

D_MODEL = 1024
EPS = 1e-6
ROPE_THETA = 500000.0
HEAD_DIM = 64
ROT_DIM = HEAD_DIM // 4
NEG = -1e30
BIG = 1e30
LANES = 128

LRU_WIDTH = D_MODEL // 2
LRU_BLOCKS = 8
CONV_WIDTH = 4
LRU_C = 8.0
MOBA_WIDTH = D_MODEL // 2
MOBA_BLOCK = 256
MOBA_TOPK = 3
MOBA_TQ = 2 * MOBA_BLOCK
MOBA_CHUNKS = 4

NSA_HEADS = D_MODEL // HEAD_DIM
NSA_KV_GROUPS = 2
NSA_HPG = NSA_HEADS // NSA_KV_GROUPS
CMP_LEN = 32
CMP_STRIDE = 16
SEL_BLOCK = 64
SEL_TOPK = 16
WINDOW = 512
NSA_QD = NSA_HEADS * HEAD_DIM
NSA_KVD = NSA_KV_GROUPS * HEAD_DIM
IN1_PAD = NSA_QD + 6 * NSA_KVD + LANES
NSA_TQ = 256

VMEM_LIMIT = 52 * 1024 * 1024

NT_DIMS = (((1,), (1,)), ((), ()))
Q_PRESCALE = HEAD_DIM ** -0.5 * math.log2(math.e)


def _params(sem):
    return pltpu.CompilerParams(dimension_semantics=sem, vmem_limit_bytes=VMEM_LIMIT)


def _resident(shape):
    nd = len(shape)
    return pl.BlockSpec(shape, lambda *_: (0,) * nd, pipeline_mode=pl.Buffered(1))


def _sigmoid(z):
    return 0.5 * jnp.tanh(0.5 * z) + 0.5


def _silu(a):
    return a * jax.nn.sigmoid(a)


def _gelu_tanh(x):
    cdf = 0.5 * (1.0 + jnp.tanh(math.sqrt(2.0 / math.pi) * (x + 0.044715 * (x * x * x))))
    return x * cdf


def _rmsnorm(x, g):
    return x * lax.rsqrt(jnp.mean(x * x, axis=-1, keepdims=True) + EPS) * g


def _norm_mod(x, g, shift, scale):
    return _rmsnorm(x, g) * (1.0 + scale) + shift


def _rope(u, cos, sa, sb):
    outs = []
    for j in range(u.shape[1] // LANES):
        c = u[:, j * LANES:(j + 1) * LANES]
        up = pltpu.roll(c, LANES - ROT_DIM // 2, axis=1)
        dn = pltpu.roll(c, ROT_DIM // 2, axis=1)
        outs.append(c * cos + up * sa + dn * sb)
    return jnp.concatenate(outs, axis=1)


def _bf16_pieces(x):
    hi = x.astype(BF16)
    r1 = x - hi.astype(F32)
    mid = r1.astype(BF16)
    lo = (r1 - mid.astype(F32)).astype(BF16)
    return jnp.concatenate([hi, mid, lo], axis=0)


SOFTMAX_CHUNK = 64


def _stream_softmax(s_ref, p_ref, m_old, bias, tq):
    rows, n = s_ref.shape
    n_chunks, n_l = rows // SOFTMAX_CHUNK, n // LANES

    def chunk(c):
        sl = slice(c * SOFTMAX_CHUNK, (c + 1) * SOFTMAX_CHUNK)
        sc = s_ref[sl, :]
        if bias is not None:
            q0 = (c * SOFTMAX_CHUNK) % tq
            sc = sc + bias[q0:q0 + SOFTMAX_CHUNK]
        return sl, [sc[:, j * LANES:(j + 1) * LANES] for j in range(n_l)]

    partial_max = jnp.concatenate([functools.reduce(jnp.maximum, chunk(c)[1]) for c in range(n_chunks)], axis=0)
    m_new = jnp.maximum(m_old, jnp.broadcast_to(jnp.max(partial_max, axis=1, keepdims=True), (rows, LANES)))
    sums = []
    for c in range(n_chunks):
        sl, cols = chunk(c)
        m_c = m_new[sl]
        ps = [jnp.exp2(col - m_c) for col in cols]
        for j, pj in enumerate(ps):
            p_ref[sl, j * LANES:(j + 1) * LANES] = pj.astype(BF16)
        sums.append(functools.reduce(jnp.add, ps))
    return m_new, jnp.concatenate(sums, axis=0)


def _pick_topk_rows(g, k):
    n_ids = g.shape[0]
    ids = lax.broadcasted_iota(jnp.int32, g.shape, 0).astype(F32)
    picked = jnp.zeros(g.shape, F32)
    for _ in range(k):
        mx = jnp.max(g, axis=0, keepdims=True)
        first = jnp.min(jnp.where(g == mx, ids, float(n_ids)), axis=0, keepdims=True)
        hit = ids == first
        picked = jnp.where(hit, 1.0, picked)
        g = jnp.where(hit, -jnp.inf, g)
    return picked


def _mod_kernel(c_ref, w_ref, b_ref, o_ref):
    cond = _silu(c_ref[...])
    o_ref[0] = jnp.dot(cond.astype(BF16), w_ref[0].astype(BF16), preferred_element_type=F32) + b_ref[0]


def _modulation(c, mod_w, mod_b):
    depth, d, n = mod_w.shape
    bsz = c.shape[0]
    bp = -(-bsz // 8) * 8
    cp = jnp.pad(c, ((0, bp - bsz), (0, 0)))
    tn = 1152
    out = pl.pallas_call(
        _mod_kernel,
        out_shape=jax.ShapeDtypeStruct((depth, bp, n), F32),
        grid=(depth, n // tn),
        in_specs=[
            pl.BlockSpec((bp, d), lambda l, j: (0, 0)),
            pl.BlockSpec((1, d, tn), lambda l, j: (l, 0, j)),
            pl.BlockSpec((1, 1, tn), lambda l, j: (l, 0, j)),
        ],
        out_specs=pl.BlockSpec((1, bp, tn), lambda l, j: (l, 0, j)),
        compiler_params=_params(("parallel", "parallel")),
        name="adaln_mod",
    )(cp, mod_w, mod_b.reshape(depth, 1, n))
    return out[:, :bsz].reshape(depth, bsz, 9, d)


def _ffn_kernel(x_ref, mod_ref, g_ref, w1_ref, w2_ref, gf_ref, *rest, row0, final, n_mix):
    o_ref = rest[-1]
    x = x_ref[0]
    m = mod_ref[0]
    if n_mix:
        mixed = sum(jnp.dot(y_ref[0], w_ref[...], preferred_element_type=F32)
                    for y_ref, w_ref in zip(rest[:n_mix], rest[n_mix:2 * n_mix]))
        x = x + m[5:6] * mixed
    h = _norm_mod(x, g_ref[...], m[row0:row0 + 1], m[row0 + 1:row0 + 2]).astype(BF16)
    d_ff = w2_ref.shape[2]
    a = jnp.dot(h, w1_ref[0, 0, :, 0:d_ff], preferred_element_type=F32)
    b = jnp.dot(h, w1_ref[0, 0, :, d_ff:2 * d_ff], preferred_element_type=F32)
    act = (_silu(a) * b).astype(BF16)
    y = jnp.dot(act, w2_ref[0, 0], preferred_element_type=F32)
    out = x + (0.5 * m[row0 + 2:row0 + 3]) * y
    if final:
        out = _rmsnorm(out, gf_ref[...])
    o_ref[0] = out


def _ffn(x, mod_l, g, w1, w2, which, gf, *, row0, final, tm, mix=()):
    bsz, s, d = x.shape
    picked = lambda shape: pl.BlockSpec((1, 1) + shape[2:], lambda b, i: which + (0, 0),
                                        pipeline_mode=pl.Buffered(1))
    kern = functools.partial(_ffn_kernel, row0=row0, final=final, n_mix=len(mix))
    row = lambda b, i: (b, i, 0)
    return pl.pallas_call(
        kern,
        out_shape=jax.ShapeDtypeStruct(x.shape, F32),
        grid=(bsz, s // tm),
        in_specs=[
            pl.BlockSpec((1, tm, d), row),
            pl.BlockSpec((1, 9, d), lambda b, i: (b, 0, 0)),
            _resident((1, d)),
            picked(w1.shape),
            picked(w2.shape),
            _resident((1, d)),
        ] + [pl.BlockSpec((1, tm, y.shape[2]), row) for y, _ in mix] + [_resident(w.shape) for _, w in mix],
        out_specs=pl.BlockSpec((1, tm, d), row),
        compiler_params=_params(("parallel", "parallel")),
        name="ffn_final" if final else ("mix_ffn" if mix else "ffn"),
    )(x, mod_l, g.reshape(1, d), w1, w2, gf.reshape(1, d), *[y for y, _ in mix], *[w for _, w in mix])


def _inproj0_kernel(x_ref, mod_ref, g_ref, w_ref, cos_ref, sa_ref, sb_ref,
                    cw_ref, cb_ref, wa_ref, ba_ref, wx_ref, bx_ref, lam_ref,
                    y_ref, q_ref, k_ref, v_ref, cent_ref, xbuf, hprev):
    @pl.when(pl.program_id(1) == 0)
    def _():
        xbuf[0:LRU_HALO, :] = jnp.zeros((LRU_HALO, LRU_WIDTH), F32)
        hprev[...] = jnp.zeros(hprev.shape, F32)

    m = mod_ref[0]
    h = _norm_mod(x_ref[0], g_ref[...], m[3:4], m[4:5]).astype(BF16)
    u = jnp.dot(h, w_ref[...], preferred_element_type=F32)
    o1 = 2 * LRU_WIDTH
    sub = xbuf.shape[0] - LRU_HALO
    for j in range(u.shape[0] // sub):
        rs = slice(j * sub, (j + 1) * sub)
        y_ref[0, rs, :] = _lru_rows(u[rs, 0:LRU_WIDTH], u[rs, LRU_WIDTH:o1], cw_ref, cb_ref, wa_ref, ba_ref,
                                    wx_ref, bx_ref, lam_ref, xbuf, hprev)
    cos, sa, sb = cos_ref[...], sa_ref[...], sb_ref[...]
    q = _rope(u[:, o1:o1 + MOBA_WIDTH], cos, sa, sb)
    k = _rope(u[:, o1 + MOBA_WIDTH:o1 + 2 * MOBA_WIDTH], cos, sa, sb)
    q_ref[0] = (q * Q_PRESCALE).astype(BF16)
    k_ref[0] = k.astype(BF16)
    v_ref[0] = u[:, o1 + 2 * MOBA_WIDTH:].astype(BF16)
    for j in range(k.shape[0] // MOBA_BLOCK):
        cent_ref[0, j] = jnp.mean(k[j * MOBA_BLOCK:(j + 1) * MOBA_BLOCK], axis=0, keepdims=True)


def _inproj0(x, mod_l, g, w, rope_tabs, lru, *, tm, lru_rows):
    bsz, s, d = x.shape
    assert tm % MOBA_BLOCK == 0
    assert tm % lru_rows == 0
    nb = s // MOBA_BLOCK
    n = w.shape[1]
    c = LRU_WIDTH
    cw, cb, wa_bd, ba, wx_bd, bx, lam = lru
    vec = _resident((1, c))
    row = lambda b, i: (b, i, 0)
    tab = pl.BlockSpec((tm, LANES), lambda b, i: (i, 0))
    return pl.pallas_call(
        _inproj0_kernel,
        out_shape=(
            jax.ShapeDtypeStruct((bsz, s, LRU_WIDTH), BF16),
            jax.ShapeDtypeStruct((bsz, s, MOBA_WIDTH), BF16),
            jax.ShapeDtypeStruct((bsz, s, MOBA_WIDTH), BF16),
            jax.ShapeDtypeStruct((bsz, s, MOBA_WIDTH), BF16),
            jax.ShapeDtypeStruct((bsz, nb, 1, MOBA_WIDTH), F32),
        ),
        grid=(bsz, s // tm),
        in_specs=[
            pl.BlockSpec((1, tm, d), row),
            pl.BlockSpec((1, 9, d), lambda b, i: (b, 0, 0)),
            _resident((1, d)),
            _resident((d, n)),
            tab, tab, tab,
            _resident((CONV_WIDTH, c)), vec,
            _resident((c, c)), vec,
            _resident((c, c)), vec,
            vec,
        ],
        out_specs=(
            pl.BlockSpec((1, tm, LRU_WIDTH), row),
            pl.BlockSpec((1, tm, MOBA_WIDTH), row),
            pl.BlockSpec((1, tm, MOBA_WIDTH), row),
            pl.BlockSpec((1, tm, MOBA_WIDTH), row),
            pl.BlockSpec((1, tm // MOBA_BLOCK, 1, MOBA_WIDTH), lambda b, i: (b, i, 0, 0)),
        ),
        scratch_shapes=[pltpu.VMEM((LRU_HALO + lru_rows, c), F32), pltpu.VMEM((8, c), F32)],
        compiler_params=_params(("parallel", "arbitrary")),
        name="inproj0_lru",
    )(x, mod_l, g.reshape(1, d), w, *rope_tabs, cw, cb.reshape(1, c), wa_bd, ba.reshape(1, c),
      wx_bd, bx.reshape(1, c), lam.reshape(1, c))


LRU_HALO = 8


def _lru_rows(x, gate_in, cw_ref, cb_ref, wa_ref, ba_ref, wx_ref, bx_ref, lam_ref, xbuf, hprev):
    tm, c = x.shape
    halo = LRU_HALO
    xbuf[halo:halo + tm, :] = x
    cw = cw_ref[...]
    xc = cb_ref[...] + cw[CONV_WIDTH - 1:CONV_WIDTH] * x
    for kk in range(CONV_WIDTH - 1):
        back = CONV_WIDTH - 1 - kk
        xc = xc + cw[kk:kk + 1] * xbuf[halo - back:halo - back + tm, :]
    xbuf[0:halo, :] = x[tm - halo:tm, :]

    xcb = xc.astype(BF16)
    r = _sigmoid(jnp.dot(xcb, wa_ref[...], preferred_element_type=F32) + ba_ref[...])
    ig = _sigmoid(jnp.dot(xcb, wx_ref[...], preferred_element_type=F32) + bx_ref[...])
    nlam = -lam_ref[...]
    softplus = jnp.maximum(nlam, 0.0) + jnp.log1p(jnp.exp(-jnp.abs(nlam)))
    log_a = (-LRU_C * r) * softplus
    a = jnp.exp(log_a)
    gain2 = -jnp.tanh(log_a) * (a * a + 1.0)
    mult = jnp.where(gain2 > 0.0, gain2 * lax.rsqrt(gain2), 0.0)
    bt = mult * (ig * xc)

    row = lax.broadcasted_iota(jnp.int32, (tm, c), 0)
    step = 1
    while step < tm:
        a_sh = pltpu.roll(a, step, axis=0)
        b_sh = pltpu.roll(bt, step, axis=0)
        valid = row >= step
        bt = jnp.where(valid, a * b_sh + bt, bt)
        a = jnp.where(valid, a * a_sh, a)
        step *= 2
    h = bt + a * hprev[0:1, :]
    hprev[0:1, :] = h[tm - 1:tm, :]
    return (h * _gelu_tanh(gate_in)).astype(BF16)


def _moba_kernel(q_ref, k_ref, v_ref, cent_ref, hot_ref, o_ref, lhs_ref, s_ref, p_ref, m_ref, l_ref, acc_ref):
    tq = MOBA_TQ
    rows = 2 * tq
    nb_pad = -(-(k_ref.shape[1] // MOBA_BLOCK) // 8) * 8
    i = pl.program_id(2)
    lane = lax.broadcasted_iota(jnp.int32, (tq, LANES), 1)
    bid = lax.broadcasted_iota(jnp.int32, (LANES, rows), 0)
    qidx = jnp.concatenate([lax.broadcasted_iota(jnp.int32, (LANES, tq), 1)] * 2, axis=1)
    own = (tq // MOBA_BLOCK) * i + qidx // MOBA_BLOCK
    past = bid < own
    qidx_r = jnp.concatenate([lax.broadcasted_iota(jnp.int32, (nb_pad, tq), 1)] * 2, axis=1)
    past_real = lax.broadcasted_iota(jnp.int32, (nb_pad, rows), 0) < (tq // MOBA_BLOCK) * i + qidx_r // MOBA_BLOCK
    chunks = [slice(c * LANES, (c + 1) * LANES) for c in range(MOBA_CHUNKS)]

    for c, cl in enumerate(chunks):
        q = q_ref[0, :, cl]
        zero = jnp.zeros_like(q)
        q2 = jnp.concatenate([jnp.where(lane < HEAD_DIM, q, zero), jnp.where(lane >= HEAD_DIM, q, zero)], axis=0)
        gate3 = lax.dot_general(_bf16_pieces(cent_ref[0, :, cl]), q2, NT_DIMS, preferred_element_type=F32)
        gate_t = gate3[:LANES] + gate3[LANES:2 * LANES] + gate3[2 * LANES:]
        sel_t = _pick_topk_rows(jnp.where(past_real, gate_t[:nb_pad], -jnp.inf), MOBA_TOPK)
        sel_t = jnp.concatenate([sel_t, jnp.zeros((LANES - nb_pad, rows), F32)], axis=0)
        drop = jnp.where((past & (sel_t > 0.5)) | (bid == own), 0.0, -BIG).T.astype(BF16)
        lhs_ref[c] = jnp.concatenate([q2, drop], axis=1)
        m_ref[c] = jnp.full((rows, LANES), NEG, F32)
        l_ref[c] = jnp.zeros((rows, LANES), F32)
        acc_ref[c] = jnp.zeros((rows, LANES), F32)

    def tile(kt, bias):
        k0 = pl.multiple_of(kt * tq, tq)
        for c, cl in enumerate(chunks):
            rhs = jnp.concatenate([k_ref[0, pl.ds(k0, tq), cl], hot_ref[pl.ds(k0, tq), :]], axis=1)
            s_ref[c] = lax.dot_general(lhs_ref[c], rhs, NT_DIMS, preferred_element_type=F32)
        for c, cl in enumerate(chunks):
            m_old = m_ref[c]
            m_new, p_sum = _stream_softmax(s_ref.at[c], p_ref.at[c], m_old, bias, tq)
            alpha = jnp.exp2(m_old - m_new)
            m_ref[c] = m_new
            l_ref[c] = alpha * l_ref[c] + p_sum
            acc_ref[c] = alpha * acc_ref[c] + jnp.dot(p_ref[c], v_ref[0, pl.ds(k0, tq), cl],
                                                      preferred_element_type=F32)

    @pl.when(i >= 0)
    def _():
        qpos = lax.broadcasted_iota(jnp.int32, (tq, tq), 0)
        kpos = lax.broadcasted_iota(jnp.int32, (tq, tq), 1)
        tile(i, jnp.where(kpos <= qpos, 0.0, NEG))

    @pl.loop(0, i)
    def _(kt):
        tile(kt, None)

    for c, cl in enumerate(chunks):
        out = acc_ref[c] / jnp.sum(l_ref[c], axis=1, keepdims=True)
        o_ref[0, :, cl] = jnp.where(lane < HEAD_DIM, out[:tq], out[tq:]).astype(BF16)


def _moba(q, k, v, cent):
    bsz, s, w = q.shape
    tq = MOBA_TQ
    nb = s // MOBA_BLOCK
    assert nb <= LANES and s % tq == 0
    cent = jnp.pad(cent, ((0, 0), (0, LANES - nb), (0, 0)))
    hot = _block_one_hot(s, MOBA_BLOCK)
    nc, cw = MOBA_CHUNKS, MOBA_CHUNKS * LANES
    return pl.pallas_call(
        _moba_kernel,
        out_shape=jax.ShapeDtypeStruct((bsz, s, w), BF16),
        grid=(bsz, w // cw, s // tq),
        in_specs=[
            pl.BlockSpec((1, tq, cw), lambda b, hp, i: (b, i, hp)),
            pl.BlockSpec((1, s, cw), lambda b, hp, i: (b, 0, hp), pipeline_mode=pl.Buffered(1)),
            pl.BlockSpec((1, s, cw), lambda b, hp, i: (b, 0, hp), pipeline_mode=pl.Buffered(1)),
            pl.BlockSpec((1, LANES, cw), lambda b, hp, i: (b, 0, hp)),
            _resident(hot.shape),
        ],
        out_specs=pl.BlockSpec((1, tq, cw), lambda b, hp, i: (b, i, hp)),
        scratch_shapes=[pltpu.VMEM((nc, 2 * tq, 2 * LANES), BF16), pltpu.VMEM((nc, 2 * tq, tq), F32),
                        pltpu.VMEM((nc, 2 * tq, tq), BF16)] + [pltpu.VMEM((nc, 2 * tq, LANES), F32)] * 3,
        compiler_params=_params(("parallel", "parallel", "parallel")),
        name="moba_attn",
    )(q, k, v, cent, hot)


def _inproj1_kernel(x_ref, mod_ref, g_ref, w_ref, cos_ref, sa_ref, sb_ref,
                    qn_ref, qr_ref, kvc_ref, kv4_ref, gate_ref):
    m = mod_ref[0]
    h = _norm_mod(x_ref[0], g_ref[...], m[3:4], m[4:5]).astype(BF16)
    u = jnp.dot(h, w_ref[...], preferred_element_type=F32)
    cos, sa, sb = cos_ref[...], sa_ref[...], sb_ref[...]
    q = u[:, :NSA_QD] * Q_PRESCALE
    qn_ref[0] = q.astype(BF16)
    qr_ref[0] = _rope(q, cos, sa, sb).astype(BF16)
    o = NSA_QD
    kvc_ref[0, 0] = u[:, o:o + NSA_KVD]
    kvc_ref[0, 1] = u[:, o + NSA_KVD:o + 2 * NSA_KVD]
    o += 2 * NSA_KVD
    ks = _rope(u[:, o:o + NSA_KVD], cos, sa, sb)
    vs = u[:, o + NSA_KVD:o + 2 * NSA_KVD]
    kw = _rope(u[:, o + 2 * NSA_KVD:o + 3 * NSA_KVD], cos, sa, sb)
    vw = u[:, o + 3 * NSA_KVD:o + 4 * NSA_KVD]
    kv4_ref[0] = jnp.concatenate([ks, vs, kw, vw], axis=1).astype(BF16)
    gate_ref[0] = jax.nn.sigmoid(u[:, o + 4 * NSA_KVD:])


def _inproj1(x, mod_l, g, w, rope_tabs, *, tm):
    bsz, s, d = x.shape
    n = w.shape[1]
    row = lambda b, i: (b, i, 0)
    tab = pl.BlockSpec((tm, LANES), lambda b, i: (i, 0))
    widths = (NSA_QD, NSA_QD, None, 4 * NSA_KVD, LANES)
    dtypes = (BF16, BF16, F32, BF16, F32)
    shape = lambda wd: (bsz, s, wd) if wd else (bsz, 2, s, NSA_KVD)
    spec = lambda wd: pl.BlockSpec((1, tm, wd), row) if wd else pl.BlockSpec((1, 2, tm, NSA_KVD),
                                                                             lambda b, i: (b, 0, i, 0))
    return pl.pallas_call(
        _inproj1_kernel,
        out_shape=tuple(jax.ShapeDtypeStruct(shape(wd), dt) for wd, dt in zip(widths, dtypes)),
        grid=(bsz, s // tm),
        in_specs=[
            pl.BlockSpec((1, tm, d), row),
            pl.BlockSpec((1, 9, d), lambda b, i: (b, 0, 0)),
            _resident((1, d)),
            _resident((d, n)),
            tab, tab, tab,
        ],
        out_specs=tuple(spec(wd) for wd in widths),
        compiler_params=_params(("parallel", "parallel")),
        name="inproj1",
    )(x, mod_l, g.reshape(1, d), w, *rope_tabs)


def _compress_kernel(kvc_ref, pos_ref, w1_ref, w2_ref, o_ref):
    nrow = o_ref.shape[2]
    for kv in range(2):
        top = jnp.zeros((nrow, w1_ref.shape[3]), F32)
        bot = jnp.zeros((nrow, w1_ref.shape[3]), F32)
        for r in range(CMP_STRIDE):
            x = kvc_ref[0, kv, pl.ds(r, nrow, stride=CMP_STRIDE), :]
            top += jnp.dot((x + pos_ref[kv, r:r + 1]).astype(BF16), w1_ref[kv, r], preferred_element_type=F32)
            bot += jnp.dot((x + pos_ref[kv, CMP_STRIDE + r:CMP_STRIDE + r + 1]).astype(BF16),
                           w1_ref[kv, CMP_STRIDE + r], preferred_element_type=F32)
        pre = top + pltpu.roll(bot, nrow - 1, axis=0)
        o_ref[0, kv] = jnp.dot(_gelu_tanh(pre).astype(BF16), w2_ref[kv], preferred_element_type=F32).astype(BF16)


def _compress(kvc, pos, w1, w2):
    bsz, _, s, _ = kvc.shape
    nrow = s // CMP_STRIDE
    eye = jnp.eye(NSA_KV_GROUPS, dtype=F32)
    pos2 = jnp.tile(pos, (1, 1, NSA_KV_GROUPS))
    hidden = w1.shape[-1]
    w1r = w1.reshape(2, CMP_LEN, HEAD_DIM, hidden)
    w1bd = jnp.einsum('gh,kric->krgihc', eye, w1r).reshape(2, CMP_LEN, NSA_KVD, NSA_KV_GROUPS * hidden)
    w2bd = jnp.einsum('gh,kcd->kgchd', eye, w2).reshape(2, NSA_KV_GROUPS * hidden, NSA_KVD)
    return pl.pallas_call(
        _compress_kernel,
        out_shape=jax.ShapeDtypeStruct((bsz, 2, nrow, NSA_KVD), BF16),
        grid=(bsz,),
        in_specs=[
            pl.BlockSpec((1, 2, s, NSA_KVD), lambda b: (b, 0, 0, 0)),
            _resident(pos2.shape),
            _resident(w1bd.shape),
            _resident(w2bd.shape),
        ],
        out_specs=pl.BlockSpec((1, 2, nrow, NSA_KVD), lambda b: (b, 0, 0, 0)),
        compiler_params=_params(("parallel",)),
        name="nsa_compress",
    )(kvc, pos2, w1bd.astype(BF16), w2bd.astype(BF16))


def _nsa_kernel(qn_ref, qr_ref, kv4_ref, cmp_ref, gate_ref, ovl_ref, hot_ref, gexp_ref, o_ref,
                lhs_ref, s_ref, p_ref, m_ref, acc_ref, *, seq, tk):
    tq = NSA_TQ
    nrow = cmp_ref.shape[2]
    ns = seq // SEL_BLOCK
    n_g, n_r = NSA_KV_GROUPS, NSA_HPG
    n_h = n_g * n_r
    rows = n_h * tq
    t0 = pl.program_id(1) * tq
    lane = lax.broadcasted_iota(jnp.int32, (tq, LANES), 1)
    gates = gate_ref[0]

    def stack(q_ref):
        parts = []
        for g in range(n_g):
            in_grp = (lane >= g * HEAD_DIM) & (lane < (g + 1) * HEAD_DIM)
            for r in range(n_r):
                ch = q_ref[0, :, r * LANES:(r + 1) * LANES]
                parts.append(jnp.where(in_grp, ch, jnp.zeros_like(ch)))
        return jnp.concatenate(parts, axis=0)

    half = rows // n_g
    neg_stats = jnp.full((rows, LANES), NEG, F32)

    def times_values(p_view, v, g, extra=None):
        vl = lax.broadcasted_iota(jnp.int32, v.shape, 1)
        vg = jnp.where((vl >= g * HEAD_DIM) & (vl < (g + 1) * HEAD_DIM), v, jnp.ones_like(v))
        if extra is not None:
            vg = jnp.concatenate([vg, extra], axis=1)
        return jnp.dot(p_view[g * half:(g + 1) * half, :], vg, preferred_element_type=F32)

    sc_ref, pc_ref = s_ref.at[:, 0:nrow], p_ref.at[:, 0:nrow]
    tq_c = t0 + lax.broadcasted_iota(jnp.int32, (tq, nrow), 0)
    cend = lax.broadcasted_iota(jnp.int32, (tq, nrow), 1) * CMP_STRIDE + (CMP_LEN - 1)
    sc_ref[...] = lax.dot_general(stack(qn_ref), cmp_ref[0, 0], NT_DIMS, preferred_element_type=F32)
    _stream_softmax(sc_ref, pc_ref, neg_stats, jnp.where(cend <= tq_c, 0.0, NEG), tq)
    both = jnp.concatenate([times_values(pc_ref, cmp_ref[0, 1], g, extra=ovl_ref[...]) for g in range(n_g)], axis=0)
    acc_c, imp_h = both[:, 0:LANES], both[:, LANES:]
    lane_r = lax.broadcasted_iota(jnp.int32, (rows, LANES), 1)
    grp_r = lax.broadcasted_iota(jnp.int32, (rows, LANES), 0) // half
    den_c = jnp.where(lane_r // HEAD_DIM == grp_r, pltpu.roll(acc_c, HEAD_DIM, axis=1), acc_c)
    has_block = jnp.where(t0 + lax.broadcasted_iota(jnp.int32, (tq, LANES), 0) >= CMP_LEN - 1, 1.0, 0.0)
    inv_c = jnp.concatenate([has_block] * n_h, axis=0) / den_c
    o_c = acc_c * inv_c

    imp_h = imp_h * inv_c
    imp_t = jnp.concatenate([functools.reduce(jnp.add, [imp_h[(g * n_r + r) * tq:(g * n_r + r + 1) * tq]
                                                        for r in range(n_r)]) for g in range(n_g)], axis=0).T
    jn = lax.broadcasted_iota(jnp.int32, (LANES, n_g * tq), 0)
    tpos = t0 + lax.broadcasted_iota(jnp.int32, (LANES, tq), 1)
    blk_t = jnp.concatenate([tpos] * n_g, axis=1) // SEL_BLOCK
    valid = jn <= blk_t
    forced = (jn == 0) | (jn == blk_t) | (jn == blk_t - 1)
    others = _pick_topk_rows(jnp.where(valid & jnp.logical_not(forced), imp_t, -jnp.inf), min(SEL_TOPK, ns) - 3)
    keep = valid & (forced | (others > 0.5))

    lhs_ref[:, 0:LANES] = stack(qr_ref)
    wk = WINDOW + tq
    sw_ref, pw_ref = s_ref.at[:, 0:wk], p_ref.at[:, 0:wk]
    kstart = pl.multiple_of(jnp.clip(t0 - WINDOW, 0, seq - wk), tq)
    diff = (t0 + lax.broadcasted_iota(jnp.int32, (tq, wk), 0)) - (
        kstart + lax.broadcasted_iota(jnp.int32, (tq, wk), 1))
    sw_ref[...] = lax.dot_general(lhs_ref[:, 0:LANES], kv4_ref[0, pl.ds(kstart, wk), 2 * LANES:3 * LANES], NT_DIMS,
                                  preferred_element_type=F32)
    _stream_softmax(sw_ref, pw_ref, neg_stats, jnp.where((diff >= 0) & (diff < WINDOW), 0.0, NEG), tq)
    acc_w = jnp.concatenate([times_values(pw_ref, kv4_ref[0, pl.ds(kstart, wk), 3 * LANES:4 * LANES], g)
                             for g in range(n_g)], axis=0)

    drop = jnp.where(keep, 0.0, -BIG).T.astype(BF16)
    lhs_ref[:, LANES:2 * LANES] = jnp.concatenate(
        [drop[g * tq:(g + 1) * tq] for g in range(n_g) for _ in range(n_r)], axis=0)

    m_ref[...] = neg_stats
    acc_ref[...] = jnp.zeros(acc_ref.shape, F32)

    def sel_tile(k0, width, bias):
        rhs = jnp.concatenate([kv4_ref[0, pl.ds(k0, width), 0:LANES], hot_ref[pl.ds(k0, width), :]], axis=1)
        views = []
        for g in range(n_g):
            gs = slice(g * half, (g + 1) * half)
            sg_ref = s_ref.at[gs, 0:width]
            sg_ref[...] = lax.dot_general(lhs_ref[gs, :], rhs, NT_DIMS, preferred_element_type=F32)
            views.append((gs, sg_ref, p_ref.at[gs, 0:width]))
        for g, (gs, sg_ref, pg_ref) in enumerate(views):
            m_old = m_ref[gs, :]
            m_new, _ = _stream_softmax(sg_ref, pg_ref, m_old, bias, tq)
            m_ref[gs, :] = m_new
            vl = lax.broadcasted_iota(jnp.int32, (width, LANES), 1)
            v = kv4_ref[0, pl.ds(k0, width), LANES:2 * LANES]
            vg = jnp.where((vl >= g * HEAD_DIM) & (vl < (g + 1) * HEAD_DIM), v, jnp.ones_like(v))
            acc_ref[gs, :] = jnp.exp2(m_old - m_new) * acc_ref[gs, :] + jnp.dot(pg_ref[...], vg,
                                                                                preferred_element_type=F32)

    last = t0 // tk

    @pl.loop(0, last)
    def _(kt):
        sel_tile(pl.multiple_of(kt * tk, tk), tk, None)

    k_last = pl.multiple_of(last * tk, tk)

    def last_tile(width):
        kpos = k_last + lax.broadcasted_iota(jnp.int32, (tq, width), 1)
        tq_s = t0 + lax.broadcasted_iota(jnp.int32, (tq, width), 0)
        sel_tile(k_last, width, jnp.where(kpos <= tq_s, 0.0, NEG))

    in_first_half = t0 - k_last + tq <= tk // 2
    pl.when(in_first_half)(lambda: last_tile(tk // 2))
    pl.when(jnp.logical_not(in_first_half))(lambda: last_tile(tk))

    acc_s = acc_ref[...]

    gates_b = gates.astype(BF16)
    spread = [jnp.dot(gates_b, gexp_ref[c], preferred_element_type=F32) for c in range(3)]
    first_half = lane < HEAD_DIM
    for r in range(n_r):
        rows_g = [slice((g * n_r + r) * tq, (g * n_r + r + 1) * tq) for g in range(n_g)]
        y = spread[0][:, r * LANES:(r + 1) * LANES] * jnp.where(first_half, o_c[rows_g[0]], o_c[rows_g[1]])
        for c, acc in ((1, acc_s), (2, acc_w)):
            vals = jnp.where(first_half, acc[rows_g[0]], acc[rows_g[1]])
            dens = pltpu.roll(jnp.where(first_half, acc[rows_g[1]], acc[rows_g[0]]), HEAD_DIM, axis=1)
            y = y + spread[c][:, r * LANES:(r + 1) * LANES] * (vals / dens)
        o_ref[0, :, r * LANES:(r + 1) * LANES] = y.astype(BF16)


def _nsa(qn, qr, kv4, cmp, gates, *, tk):
    bsz, s, _ = qn.shape
    tq = NSA_TQ
    ns = s // SEL_BLOCK
    assert ns <= LANES
    assert min(SEL_TOPK, ns) >= 3
    ovl = jnp.pad(_overlap(cmp.shape[2], ns), ((0, 0), (0, LANES - ns))).astype(BF16)
    hot = _block_one_hot(s, SEL_BLOCK)
    head_of_p = _nsa_head_perm() // HEAD_DIM
    gexp = (jnp.arange(LANES)[None, :, None] == 3 * head_of_p[None, None, :] + jnp.arange(3)[:, None, None]).astype(BF16)
    rows = NSA_HEADS * tq
    width = max(cmp.shape[2], WINDOW + tq, tk)
    row = lambda b, i: (b, i, 0)
    return pl.pallas_call(
        functools.partial(_nsa_kernel, seq=s, tk=tk),
        out_shape=jax.ShapeDtypeStruct((bsz, s, NSA_QD), BF16),
        grid=(bsz, s // tq),
        in_specs=[
            pl.BlockSpec((1, tq, NSA_QD), row),
            pl.BlockSpec((1, tq, NSA_QD), row),
            pl.BlockSpec((1, s, 4 * NSA_KVD), lambda b, i: (b, 0, 0), pipeline_mode=pl.Buffered(1)),
            pl.BlockSpec((1, 2) + cmp.shape[2:], lambda b, i: (b, 0, 0, 0), pipeline_mode=pl.Buffered(1)),
            pl.BlockSpec((1, tq, LANES), row),
            _resident(ovl.shape),
            _resident(hot.shape),
            _resident(gexp.shape),
        ],
        out_specs=pl.BlockSpec((1, tq, NSA_QD), row),
        scratch_shapes=[pltpu.VMEM((rows, 2 * LANES), BF16), pltpu.VMEM((rows, width), F32),
                        pltpu.VMEM((rows, width), BF16),
                        pltpu.VMEM((rows, LANES), F32), pltpu.VMEM((rows, LANES), F32)],
        compiler_params=_params(("parallel", "parallel")),
        name="nsa_attn",
    )(qn, qr, kv4, cmp, gates, ovl, hot, gexp)


def _rope_tables(s):
    half = ROT_DIM // 2
    pos = jnp.arange(s, dtype=F32)
    freqs = ROPE_THETA ** (-jnp.arange(half, dtype=F32) * 2.0 / ROT_DIM)
    ang = pos[:, None] * freqs[None, :]
    cos, sin = jnp.cos(ang), jnp.sin(ang)
    ones = jnp.ones((s, HEAD_DIM - ROT_DIM), F32)
    zeros = jnp.zeros((s, HEAD_DIM - ROT_DIM), F32)
    zh = jnp.zeros((s, half), F32)
    cos_t = jnp.concatenate([cos, cos, ones], axis=1)
    sa_t = jnp.concatenate([-sin, zh, zeros], axis=1)
    sb_t = jnp.concatenate([zh, sin, zeros], axis=1)
    return tuple(jnp.tile(t, (1, LANES // HEAD_DIM)) for t in (cos_t, sa_t, sb_t))


def _block_one_hot(s, block):
    return (jnp.arange(s)[:, None] // block == jnp.arange(LANES)[None, :]).astype(BF16)


def _block_diag(w):
    n, a, b = w.shape
    eye = jnp.eye(n, dtype=w.dtype)
    return (eye[:, None, :, None] * w[:, :, None, :]).reshape(n * a, n * b)


def _nsa_head_perm():
    r = jnp.arange(NSA_HPG)[:, None, None]
    g = jnp.arange(NSA_KV_GROUPS)[None, :, None]
    dd = jnp.arange(HEAD_DIM)[None, None, :]
    return (HEAD_DIM * (NSA_HPG * g + r) + dd).reshape(-1)


def _overlap(nrow, ns):
    ci = jnp.arange(nrow)[:, None] * CMP_STRIDE
    sj = jnp.arange(ns)[None, :] * SEL_BLOCK
    ov = jnp.clip(jnp.minimum(ci + CMP_LEN, sj + SEL_BLOCK) - jnp.maximum(ci, sj), 0, None)
    return ov.astype(F32) / CMP_LEN


def kernel(x, c, mod_w, mod_b, norm_g, ffn_w1, ffn_w2, mix0_in_w, lru_conv_w, lru_conv_b, lru_wa, lru_ba,
           lru_wx, lru_bx, lru_lambda, mix0_out_w, mix1_in_w, cmp_pos, cmp_w1, cmp_w2, mix1_out_w, final_norm_g):
    bsz, s, d = x.shape
    depth = mod_w.shape[0]
    tm = 512
    assert d == D_MODEL and s % (2 * tm) == 0 and s >= WINDOW + NSA_TQ

    mod = _modulation(c, mod_w, mod_b)
    tabs = _rope_tables(s)
    w1 = ffn_w1.astype(BF16)
    w2 = ffn_w2.astype(BF16)

    for l in range(depth):
        j = l // 2
        x = _ffn(x, mod[l], norm_g[l, 0], w1, w2, (l, 0), final_norm_g, row0=0, final=False, tm=tm)
        if l % 2 == 0:
            lru = (lru_conv_w[j], lru_conv_b[j], _block_diag(lru_wa[j]).astype(BF16), lru_ba[j].reshape(-1),
                   _block_diag(lru_wx[j]).astype(BF16), lru_bx[j].reshape(-1), lru_lambda[j])
            y_lru, q, k, v, cent = _inproj0(x, mod[l], norm_g[l, 1], mix0_in_w[j].astype(BF16), tabs, lru,
                                            tm=tm, lru_rows=256)
            y_att = _moba(q, k, v, cent.reshape(bsz, -1, MOBA_WIDTH))
            wo = mix0_out_w[j].astype(BF16)
            mix = ((y_lru, wo[:LRU_WIDTH]), (y_att, wo[LRU_WIDTH:]))
        else:
            perm = _nsa_head_perm()
            w_in = mix1_in_w[j]
            w_in = jnp.concatenate([w_in[:, :NSA_QD][:, perm], w_in[:, NSA_QD:]], axis=1)
            w_in = jnp.pad(w_in, ((0, 0), (0, IN1_PAD - w_in.shape[1]))).astype(BF16)
            qn, qr, kvc, kv4, gates = _inproj1(x, mod[l], norm_g[l, 1], w_in, tabs, tm=2 * tm)
            cmp = _compress(kvc, cmp_pos[j], cmp_w1[j], cmp_w2[j])
            y = _nsa(qn, qr, kv4, cmp, gates, tk=512)
            mix = ((y, mix1_out_w[j][perm].astype(BF16)),)
        x = _ffn(x, mod[l], norm_g[l, 2], w1, w2, (l, 1), final_norm_g,
                 row0=6, final=(l == depth - 1), tm=tm, mix=mix)
    return x
```

```python
import functools
import math

import jax
import jax.numpy as jnp
from jax import lax
from jax.experimental import pallas as pl
from jax.experimental.pallas import tpu as pltpu

F32 = jnp.float32
BF16 = jnp.bfloat16

D_MODEL = 1024
EPS = 1e-6
ROPE_THETA = 500000.0
HEAD_DIM = 64
ROT_DIM = HEAD_DIM // 4
NEG = -1e30
BIG = 1e30
LANES = 128

LRU_WIDTH = D_MODEL // 2
LRU_BLOCKS = 8
CONV_WIDTH = 4
LRU_C = 8.0
MOBA_WIDTH = D_MODEL // 2
MOBA_BLOCK = 256
MOBA_TOPK = 3
MOBA_TQ = 2 * MOBA_BLOCK
MOBA_CHUNKS = 4

NSA_HEADS = D_MODEL // HEAD_DIM
NSA_KV_GROUPS = 2
NSA_HPG = NSA_HEADS // NSA_KV_GROUPS
CMP_LEN = 32
CMP_STRIDE = 16
SEL_BLOCK = 64
SEL_TOPK = 16
WINDOW = 512
NSA_QD = NSA_HEADS * HEAD_DIM
NSA_KVD = NSA_KV_GROUPS * HEAD_DIM
IN1_PAD = NSA_QD + 6 * NSA_KVD + LANES
NSA_TQ = 256

VMEM_LIMIT = 52 * 1024 * 1024

NT_DIMS = (((1,), (1,)), ((), ()))
Q_PRESCALE = HEAD_DIM ** -0.5 * math.log2(math.e)


def _params(sem):
    return pltpu.CompilerParams(dimension_semantics=sem, vmem_limit_bytes=VMEM_LIMIT)


def _resident(shape):
    nd = len(shape)
    return pl.BlockSpec(shape, lambda *_: (0,) * nd, pipeline_mode=pl.Buffered(1))


def _sigmoid(z):
    return 0.5 * jnp.tanh(0.5 * z) + 0.5


def _silu(a):
    return a * jax.nn.sigmoid(a)


def _gelu_tanh(x):
    cdf = 0.5 * (1.0 + jnp.tanh(math.sqrt(2.0 / math.pi) * (x + 0.044715 * (x * x * x))))
    return x * cdf


def _rmsnorm(x, g):
    return x * lax.rsqrt(jnp.mean(x * x, axis=-1, keepdims=True) + EPS) * g


def _norm_mod(x, g, shift, scale):
    return _rmsnorm(x, g) * (1.0 + scale) + shift


def _rope(u, cos, sa, sb):
    outs = []
    for j in range(u.shape[1] // LANES):
        c = u[:, j * LANES:(j + 1) * LANES]
        up = pltpu.roll(c, LANES - ROT_DIM // 2, axis=1)
        dn = pltpu.roll(c, ROT_DIM // 2, axis=1)
        outs.append(c * cos + up * sa + dn * sb)
    return jnp.concatenate(outs, axis=1)


def _bf16_pieces(x):
    hi = x.astype(BF16)
    r1 = x - hi.astype(F32)
    mid = r1.astype(BF16)
    lo = (r1 - mid.astype(F32)).astype(BF16)
    return jnp.concatenate([hi, mid, lo], axis=0)


SOFTMAX_CHUNK = 64


def _stream_softmax(s_ref, p_ref, m_old, bias, tq):
    rows, n = s_ref.shape
    n_chunks, n_l = rows // SOFTMAX_CHUNK, n // LANES

    def chunk(c):
        sl = slice(c * SOFTMAX_CHUNK, (c + 1) * SOFTMAX_CHUNK)
        sc = s_ref[sl, :]
        if bias is not None:
            q0 = (c * SOFTMAX_CHUNK) % tq
            sc = sc + bias[q0:q0 + SOFTMAX_CHUNK]
        return sl, [sc[:, j * LANES:(j + 1) * LANES] for j in range(n_l)]

    partial_max = jnp.concatenate([functools.reduce(jnp.maximum, chunk(c)[1]) for c in range(n_chunks)], axis=0)
    m_new = jnp.maximum(m_old, jnp.broadcast_to(jnp.max(partial_max, axis=1, keepdims=True), (rows, LANES)))
    sums = []
    for c in range(n_chunks):
        sl, cols = chunk(c)
        m_c = m_new[sl]
        ps = [jnp.exp2(col - m_c) for col in cols]
        for j, pj in enumerate(ps):
            p_ref[sl, j * LANES:(j + 1) * LANES] = pj.astype(BF16)
        sums.append(functools.reduce(jnp.add, ps))
    return m_new, jnp.concatenate(sums, axis=0)


def _pick_topk_rows(g, k):
    n_ids = g.shape[0]
    ids = lax.broadcasted_iota(jnp.int32, g.shape, 0).astype(F32)
    picked = jnp.zeros(g.shape, F32)
    for _ in range(k):
        mx = jnp.max(g, axis=0, keepdims=True)
        first = jnp.min(jnp.where(g == mx, ids, float(n_ids)), axis=0, keepdims=True)
        hit = ids == first
        picked = jnp.where(hit, 1.0, picked)
        g = jnp.where(hit, -jnp.inf, g)
    return picked


def _mod_kernel(c_ref, w_ref, b_ref, o_ref):
    cond = _silu(c_ref[...])
    o_ref[0] = jnp.dot(cond.astype(BF16), w_ref[0].astype(BF16), preferred_element_type=F32) + b_ref[0]


def _modulation(c, mod_w, mod_b):
    depth, d, n = mod_w.shape
    bsz = c.shape[0]
    bp = -(-bsz // 8) * 8
    cp = jnp.pad(c, ((0, bp - bsz), (0, 0)))
    tn = 1152
    out = pl.pallas_call(
        _mod_kernel,
        out_shape=jax.ShapeDtypeStruct((depth, bp, n), F32),
        grid=(depth, n // tn),
        in_specs=[
            pl.BlockSpec((bp, d), lambda l, j: (0, 0)),
            pl.BlockSpec((1, d, tn), lambda l, j: (l, 0, j)),
            pl.BlockSpec((1, 1, tn), lambda l, j: (l, 0, j)),
        ],
        out_specs=pl.BlockSpec((1, bp, tn), lambda l, j: (l, 0, j)),
        compiler_params=_params(("parallel", "parallel")),
        name="adaln_mod",
    )(cp, mod_w, mod_b.reshape(depth, 1, n))
    return out[:, :bsz].reshape(depth, bsz, 9, d)


def _ffn_kernel(x_ref, mod_ref, g_ref, w1_ref, w2_ref, gf_ref, *rest, row0, final, n_mix):
    o_ref = rest[-1]
    x = x_ref[0]
    m = mod_ref[0]
    if n_mix:
        mixed = sum(jnp.dot(y_ref[0], w_ref[...], preferred_element_type=F32)
                    for y_ref, w_ref in zip(rest[:n_mix], rest[n_mix:2 * n_mix]))
        x = x + m[5:6] * mixed
    h = _norm_mod(x, g_ref[...], m[row0:row0 + 1], m[row0 + 1:row0 + 2]).astype(BF16)
    d_ff = w2_ref.shape[2]
    a = jnp.dot(h, w1_ref[0, 0, :, 0:d_ff], preferred_element_type=F32)
    b = jnp.dot(h, w1_ref[0, 0, :, d_ff:2 * d_ff], preferred_element_type=F32)
    act = (_silu(a) * b).astype(BF16)
    y = jnp.dot(act, w2_ref[0, 0], preferred_element_type=F32)
    out = x + (0.5 * m[row0 + 2:row0 + 3]) * y
    if final:
        out = _rmsnorm(out, gf_ref[...])
    o_ref[0] = out


def _ffn(x, mod_l, g, w1, w2, which, gf, *, row0, final, tm, mix=()):
    bsz, s, d = x.shape
    picked = lambda shape: pl.BlockSpec((1, 1) + shape[2:], lambda b, i: which + (0, 0),
                                        pipeline_mode=pl.Buffered(1))
    kern = functools.partial(_ffn_kernel, row0=row0, final=final, n_mix=len(mix))
    row = lambda b, i: (b, i, 0)
    return pl.pallas_call(
        kern,
        out_shape=jax.ShapeDtypeStruct(x.shape, F32),
        grid=(bsz, s // tm),
        in_specs=[
            pl.BlockSpec((1, tm, d), row),
            pl.BlockSpec((1, 9, d), lambda b, i: (b, 0, 0)),
            _resident((1, d)),
            picked(w1.shape),
            picked(w2.shape),
            _resident((1, d)),
        ] + [pl.BlockSpec((1, tm, y.shape[2]), row) for y, _ in mix] + [_resident(w.shape) for _, w in mix],
        out_specs=pl.BlockSpec((1, tm, d), row),
        compiler_params=_params(("parallel", "parallel")),
        name="ffn_final" if final else ("mix_ffn" if mix else "ffn"),
    )(x, mod_l, g.reshape(1, d), w1, w2, gf.reshape(1, d), *[y for y, _ in mix], *[w for _, w in mix])


def _inproj0_kernel(x_ref, mod_ref, g_ref, w_ref, cos_ref, sa_ref, sb_ref,
                    cw_ref, cb_ref, wa_ref, ba_ref, wx_ref, bx_ref, lam_ref,
                    y_ref, q_ref, k_ref, v_ref, cent_ref, xbuf, hprev):
    @pl.when(pl.program_id(1) == 0)
    def _():
        xbuf[0:LRU_HALO, :] = jnp.zeros((LRU_HALO, LRU_WIDTH), F32)
        hprev[...] = jnp.zeros(hprev.shape, F32)

    m = mod_ref[0]
    h = _norm_mod(x_ref[0], g_ref[...], m[3:4], m[4:5]).astype(BF16)
    u = jnp.dot(h, w_ref[...], preferred_element_type=F32)
    o1 = 2 * LRU_WIDTH
    sub = xbuf.shape[0] - LRU_HALO
    for j in range(u.shape[0] // sub):
        rs = slice(j * sub, (j + 1) * sub)
        y_ref[0, rs, :] = _lru_rows(u[rs, 0:LRU_WIDTH], u[rs, LRU_WIDTH:o1], cw_ref, cb_ref, wa_ref, ba_ref,
                                    wx_ref, bx_ref, lam_ref, xbuf, hprev)
    cos, sa, sb = cos_ref[...], sa_ref[...], sb_ref[...]
    q = _rope(u[:, o1:o1 + MOBA_WIDTH], cos, sa, sb)
    k = _rope(u[:, o1 + MOBA_WIDTH:o1 + 2 * MOBA_WIDTH], cos, sa, sb)
    q_ref[0] = (q * Q_PRESCALE).astype(BF16)
    k_ref[0] = k.astype(BF16)
    v_ref[0] = u[:, o1 + 2 * MOBA_WIDTH:].astype(BF16)
    for j in range(k.shape[0] // MOBA_BLOCK):
        cent_ref[0, j] = jnp.mean(k[j * MOBA_BLOCK:(j + 1) * MOBA_BLOCK], axis=0, keepdims=True)


def _inproj0(x, mod_l, g, w, rope_tabs, lru, *, tm, lru_rows):
    bsz, s, d = x.shape
    assert tm % MOBA_BLOCK == 0
    assert tm % lru_rows == 0
    nb = s // MOBA_BLOCK
    n = w.shape[1]
    c = LRU_WIDTH
    cw, cb, wa_bd, ba, wx_bd, bx, lam = lru
    vec = _resident((1, c))
    row = lambda b, i: (b, i, 0)
    tab = pl.BlockSpec((tm, LANES), lambda b, i: (i, 0))
    return pl.pallas_call(
        _inproj0_kernel,
        out_shape=(
            jax.ShapeDtypeStruct((bsz, s, LRU_WIDTH), BF16),
            jax.ShapeDtypeStruct((bsz, s, MOBA_WIDTH), BF16),
            jax.ShapeDtypeStruct((bsz, s, MOBA_WIDTH), BF16),
            jax.ShapeDtypeStruct((bsz, s, MOBA_WIDTH), BF16),
            jax.ShapeDtypeStruct((bsz, nb, 1, MOBA_WIDTH), F32),
        ),
        grid=(bsz, s // tm),
        in_specs=[
            pl.BlockSpec((1, tm, d), row),
            pl.BlockSpec((1, 9, d), lambda b, i: (b, 0, 0)),
            _resident((1, d)),
            _resident((d, n)),
            tab, tab, tab,
            _resident((CONV_WIDTH, c)), vec,
            _resident((c, c)), vec,
            _resident((c, c)), vec,
            vec,
        ],
        out_specs=(
            pl.BlockSpec((1, tm, LRU_WIDTH), row),
            pl.BlockSpec((1, tm, MOBA_WIDTH), row),
            pl.BlockSpec((1, tm, MOBA_WIDTH), row),
            pl.BlockSpec((1, tm, MOBA_WIDTH), row),
            pl.BlockSpec((1, tm // MOBA_BLOCK, 1, MOBA_WIDTH), lambda b, i: (b, i, 0, 0)),
        ),
        scratch_shapes=[pltpu.VMEM((LRU_HALO + lru_rows, c), F32), pltpu.VMEM((8, c), F32)],
        compiler_params=_params(("parallel", "arbitrary")),
        name="inproj0_lru",
    )(x, mod_l, g.reshape(1, d), w, *rope_tabs, cw, cb.reshape(1, c), wa_bd, ba.reshape(1, c),
      wx_bd, bx.reshape(1, c), lam.reshape(1, c))


LRU_HALO = 8


def _lru_rows(x, gate_in, cw_ref, cb_ref, wa_ref, ba_ref, wx_ref, bx_ref, lam_ref, xbuf, hprev):
    tm, c = x.shape
    halo = LRU_HALO
    xbuf[halo:halo + tm, :] = x
    cw = cw_ref[...]
    xc = cb_ref[...] + cw[CONV_WIDTH - 1:CONV_WIDTH] * x
    for kk in range(CONV_WIDTH - 1):
        back = CONV_WIDTH - 1 - kk
        xc = xc + cw[kk:kk + 1] * xbuf[halo - back:halo - back + tm, :]
    xbuf[0:halo, :] = x[tm - halo:tm, :]

    xcb = xc.astype(BF16)
    r = _sigmoid(jnp.dot(xcb, wa_ref[...], preferred_element_type=F32) + ba_ref[...])
    ig = _sigmoid(jnp.dot(xcb, wx_ref[...], preferred_element_type=F32) + bx_ref[...])
    nlam = -lam_ref[...]
    softplus = jnp.maximum(nlam, 0.0) + jnp.log1p(jnp.exp(-jnp.abs(nlam)))
    log_a = (-LRU_C * r) * softplus
    a = jnp.exp(log_a)
    gain2 = -jnp.tanh(log_a) * (a * a + 1.0)
    mult = jnp.where(gain2 > 0.0, gain2 * lax.rsqrt(gain2), 0.0)
    bt = mult * (ig * xc)

    row = lax.broadcasted_iota(jnp.int32, (tm, c), 0)
    step = 1
    while step < tm:
        a_sh = pltpu.roll(a, step, axis=0)
        b_sh = pltpu.roll(bt, step, axis=0)
        valid = row >= step
        bt = jnp.where(valid, a * b_sh + bt, bt)
        a = jnp.where(valid, a * a_sh, a)
        step *= 2
    h = bt + a * hprev[0:1, :]
    hprev[0:1, :] = h[tm - 1:tm, :]
    return (h * _gelu_tanh(gate_in)).astype(BF16)


def _moba_kernel(q_ref, k_ref, v_ref, cent_ref, hot_ref, o_ref, lhs_ref, s_ref, p_ref, m_ref, l_ref, acc_ref):
    tq = MOBA_TQ
    rows = 2 * tq
    nb_pad = -(-(k_ref.shape[1] // MOBA_BLOCK) // 8) * 8
    i = pl.program_id(2)
    lane = lax.broadcasted_iota(jnp.int32, (tq, LANES), 1)
    bid = lax.broadcasted_iota(jnp.int32, (LANES, rows), 0)
    qidx = jnp.concatenate([lax.broadcasted_iota(jnp.int32, (LANES, tq), 1)] * 2, axis=1)
    own = (tq // MOBA_BLOCK) * i + qidx // MOBA_BLOCK
    past = bid < own
    qidx_r = jnp.concatenate([lax.broadcasted_iota(jnp.int32, (nb_pad, tq), 1)] * 2, axis=1)
    past_real = lax.broadcasted_iota(jnp.int32, (nb_pad, rows), 0) < (tq // MOBA_BLOCK) * i + qidx_r // MOBA_BLOCK
    chunks = [slice(c * LANES, (c + 1) * LANES) for c in range(MOBA_CHUNKS)]

    for c, cl in enumerate(chunks):
        q = q_ref[0, :, cl]
        zero = jnp.zeros_like(q)
        q2 = jnp.concatenate([jnp.where(lane < HEAD_DIM, q, zero), jnp.where(lane >= HEAD_DIM, q, zero)], axis=0)
        gate3 = lax.dot_general(_bf16_pieces(cent_ref[0, :, cl]), q2, NT_DIMS, preferred_element_type=F32)
        gate_t = gate3[:LANES] + gate3[LANES:2 * LANES] + gate3[2 * LANES:]
        sel_t = _pick_topk_rows(jnp.where(past_real, gate_t[:nb_pad], -jnp.inf), MOBA_TOPK)
        sel_t = jnp.concatenate([sel_t, jnp.zeros((LANES - nb_pad, rows), F32)], axis=0)
        drop = jnp.where((past & (sel_t > 0.5)) | (bid == own), 0.0, -BIG).T.astype(BF16)
        lhs_ref[c] = jnp.concatenate([q2, drop], axis=1)
        m_ref[c] = jnp.full((rows, LANES), NEG, F32)
        l_ref[c] = jnp.zeros((rows, LANES), F32)
        acc_ref[c] = jnp.zeros((rows, LANES), F32)

    def tile(kt, bias):
        k0 = pl.multiple_of(kt * tq, tq)
        for c, cl in enumerate(chunks):
            rhs = jnp.concatenate([k_ref[0, pl.ds(k0, tq), cl], hot_ref[pl.ds(k0, tq), :]], axis=1)
            s_ref[c] = lax.dot_general(lhs_ref[c], rhs, NT_DIMS, preferred_element_type=F32)
        for c, cl in enumerate(chunks):
            m_old = m_ref[c]
            m_new, p_sum = _stream_softmax(s_ref.at[c], p_ref.at[c], m_old, bias, tq)
            alpha = jnp.exp2(m_old - m_new)
            m_ref[c] = m_new
            l_ref[c] = alpha * l_ref[c] + p_sum
            acc_ref[c] = alpha * acc_ref[c] + jnp.dot(p_ref[c], v_ref[0, pl.ds(k0, tq), cl],
                                                      preferred_element_type=F32)

    @pl.when(i >= 0)
    def _():
        qpos = lax.broadcasted_iota(jnp.int32, (tq, tq), 0)
        kpos = lax.broadcasted_iota(jnp.int32, (tq, tq), 1)
        tile(i, jnp.where(kpos <= qpos, 0.0, NEG))

    @pl.loop(0, i)
    def _(kt):
        tile(kt, None)

    for c, cl in enumerate(chunks):
        out = acc_ref[c] / jnp.sum(l_ref[c], axis=1, keepdims=True)
        o_ref[0, :, cl] = jnp.where(lane < HEAD_DIM, out[:tq], out[tq:]).astype(BF16)


def _moba(q, k, v, cent):
    bsz, s, w = q.shape
    tq = MOBA_TQ
    nb = s // MOBA_BLOCK
    assert nb <= LANES and s % tq == 0
    cent = jnp.pad(cent, ((0, 0), (0, LANES - nb), (0, 0)))
    hot = _block_one_hot(s, MOBA_BLOCK)
    nc, cw = MOBA_CHUNKS, MOBA_CHUNKS * LANES
    return pl.pallas_call(
        _moba_kernel,
        out_shape=jax.ShapeDtypeStruct((bsz, s, w), BF16),
        grid=(bsz, w // cw, s // tq),
        in_specs=[
            pl.BlockSpec((1, tq, cw), lambda b, hp, i: (b, i, hp)),
            pl.BlockSpec((1, s, cw), lambda b, hp, i: (b, 0, hp), pipeline_mode=pl.Buffered(1)),
            pl.BlockSpec((1, s, cw), lambda b, hp, i: (b, 0, hp), pipeline_mode=pl.Buffered(1)),
            pl.BlockSpec((1, LANES, cw), lambda b, hp, i: (b, 0, hp)),
            _resident(hot.shape),
        ],
        out_specs=pl.BlockSpec((1, tq, cw), lambda b, hp, i: (b, i, hp)),
        scratch_shapes=[pltpu.VMEM((nc, 2 * tq, 2 * LANES), BF16), pltpu.VMEM((nc, 2 * tq, tq), F32),
                        pltpu.VMEM((nc, 2 * tq, tq), BF16)] + [pltpu.VMEM((nc, 2 * tq, LANES), F32)] * 3,
        compiler_params=_params(("parallel", "parallel", "parallel")),
        name="moba_attn",
    )(q, k, v, cent, hot)


def _inproj1_kernel(x_ref, mod_ref, g_ref, w_ref, cos_ref, sa_ref, sb_ref,
                    qn_ref, qr_ref, kvc_ref, kv4_ref, gate_ref):
    m = mod_ref[0]
    h = _norm_mod(x_ref[0], g_ref[...], m[3:4], m[4:5]).astype(BF16)
    u = jnp.dot(h, w_ref[...], preferred_element_type=F32)
    cos, sa, sb = cos_ref[...], sa_ref[...], sb_ref[...]
    q = u[:, :NSA_QD] * Q_PRESCALE
    qn_ref[0] = q.astype(BF16)
    qr_ref[0] = _rope(q, cos, sa, sb).astype(BF16)
    o = NSA_QD
    kvc_ref[0, 0] = u[:, o:o + NSA_KVD]
    kvc_ref[0, 1] = u[:, o + NSA_KVD:o + 2 * NSA_KVD]
    o += 2 * NSA_KVD
    ks = _rope(u[:, o:o + NSA_KVD], cos, sa, sb)
    vs = u[:, o + NSA_KVD:o + 2 * NSA_KVD]
    kw = _rope(u[:, o + 2 * NSA_KVD:o + 3 * NSA_KVD], cos, sa, sb)
    vw = u[:, o + 3 * NSA_KVD:o + 4 * NSA_KVD]
    kv4_ref[0] = jnp.concatenate([ks, vs, kw, vw], axis=1).astype(BF16)
    gate_ref[0] = jax.nn.sigmoid(u[:, o + 4 * NSA_KVD:])


def _inproj1(x, mod_l, g, w, rope_tabs, *, tm):
    bsz, s, d = x.shape
    n = w.shape[1]
    row = lambda b, i: (b, i, 0)
    tab = pl.BlockSpec((tm, LANES), lambda b, i: (i, 0))
    widths = (NSA_QD, NSA_QD, None, 4 * NSA_KVD, LANES)
    dtypes = (BF16, BF16, F32, BF16, F32)
    shape = lambda wd: (bsz, s, wd) if wd else (bsz, 2, s, NSA_KVD)
    spec = lambda wd: pl.BlockSpec((1, tm, wd), row) if wd else pl.BlockSpec((1, 2, tm, NSA_KVD),
                                                                             lambda b, i: (b, 0, i, 0))
    return pl.pallas_call(
        _inproj1_kernel,
        out_shape=tuple(jax.ShapeDtypeStruct(shape(wd), dt) for wd, dt in zip(widths, dtypes)),
        grid=(bsz, s // tm),
        in_specs=[
            pl.BlockSpec((1, tm, d), row),
            pl.BlockSpec((1, 9, d), lambda b, i: (b, 0, 0)),
            _resident((1, d)),
            _resident((d, n)),
            tab, tab, tab,
        ],
        out_specs=tuple(spec(wd) for wd in widths),
        compiler_params=_params(("parallel", "parallel")),
        name="inproj1",
    )(x, mod_l, g.reshape(1, d), w, *rope_tabs)


def _compress_kernel(kvc_ref, pos_ref, w1_ref, w2_ref, o_ref):
    nrow = o_ref.shape[2]
    for kv in range(2):
        top = jnp.zeros((nrow, w1_ref.shape[3]), F32)
        bot = jnp.zeros((nrow, w1_ref.shape[3]), F32)
        for r in range(CMP_STRIDE):
            x = kvc_ref[0, kv, pl.ds(r, nrow, stride=CMP_STRIDE), :]
            top += jnp.dot((x + pos_ref[kv, r:r + 1]).astype(BF16), w1_ref[kv, r], preferred_element_type=F32)
            bot += jnp.dot((x + pos_ref[kv, CMP_STRIDE + r:CMP_STRIDE + r + 1]).astype(BF16),
                           w1_ref[kv, CMP_STRIDE + r], preferred_element_type=F32)
        pre = top + pltpu.roll(bot, nrow - 1, axis=0)
        o_ref[0, kv] = jnp.dot(_gelu_tanh(pre).astype(BF16), w2_ref[kv], preferred_element_type=F32).astype(BF16)


def _compress(kvc, pos, w1, w2):
    bsz, _, s, _ = kvc.shape
    nrow = s // CMP_STRIDE
    eye = jnp.eye(NSA_KV_GROUPS, dtype=F32)
    pos2 = jnp.tile(pos, (1, 1, NSA_KV_GROUPS))
    hidden = w1.shape[-1]
    w1r = w1.reshape(2, CMP_LEN, HEAD_DIM, hidden)
    w1bd = jnp.einsum('gh,kric->krgihc', eye, w1r).reshape(2, CMP_LEN, NSA_KVD, NSA_KV_GROUPS * hidden)
    w2bd = jnp.einsum('gh,kcd->kgchd', eye, w2).reshape(2, NSA_KV_GROUPS * hidden, NSA_KVD)
    return pl.pallas_call(
        _compress_kernel,
        out_shape=jax.ShapeDtypeStruct((bsz, 2, nrow, NSA_KVD), BF16),
        grid=(bsz,),
        in_specs=[
            pl.BlockSpec((1, 2, s, NSA_KVD), lambda b: (b, 0, 0, 0)),
            _resident(pos2.shape),
            _resident(w1bd.shape),
            _resident(w2bd.shape),
        ],
        out_specs=pl.BlockSpec((1, 2, nrow, NSA_KVD), lambda b: (b, 0, 0, 0)),
        compiler_params=_params(("parallel",)),
        name="nsa_compress",
    )(kvc, pos2, w1bd.astype(BF16), w2bd.astype(BF16))


def _nsa_kernel(qn_ref, qr_ref, kv4_ref, cmp_ref, gate_ref, ovl_ref, hot_ref, gexp_ref, o_ref,
                lhs_ref, s_ref, p_ref, m_ref, acc_ref, *, seq, tk):
    tq = NSA_TQ
    nrow = cmp_ref.shape[2]
    ns = seq // SEL_BLOCK
    n_g, n_r = NSA_KV_GROUPS, NSA_HPG
    n_h = n_g * n_r
    rows = n_h * tq
    t0 = pl.program_id(1) * tq
    lane = lax.broadcasted_iota(jnp.int32, (tq, LANES), 1)
    gates = gate_ref[0]

    def stack(q_ref):
        parts = []
        for g in range(n_g):
            in_grp = (lane >= g * HEAD_DIM) & (lane < (g + 1) * HEAD_DIM)
            for r in range(n_r):
                ch = q_ref[0, :, r * LANES:(r + 1) * LANES]
                parts.append(jnp.where(in_grp, ch, jnp.zeros_like(ch)))
        return jnp.concatenate(parts, axis=0)

    half = rows // n_g
    neg_stats = jnp.full((rows, LANES), NEG, F32)

    def times_values(p_view, v, g, extra=None):
        vl = lax.broadcasted_iota(jnp.int32, v.shape, 1)
        vg = jnp.where((vl >= g * HEAD_DIM) & (vl < (g + 1) * HEAD_DIM), v, jnp.ones_like(v))
        if extra is not None:
            vg = jnp.concatenate([vg, extra], axis=1)
        return jnp.dot(p_view[g * half:(g + 1) * half, :], vg, preferred_element_type=F32)

    sc_ref, pc_ref = s_ref.at[:, 0:nrow], p_ref.at[:, 0:nrow]
    tq_c = t0 + lax.broadcasted_iota(jnp.int32, (tq, nrow), 0)
    cend = lax.broadcasted_iota(jnp.int32, (tq, nrow), 1) * CMP_STRIDE + (CMP_LEN - 1)
    sc_ref[...] = lax.dot_general(stack(qn_ref), cmp_ref[0, 0], NT_DIMS, preferred_element_type=F32)
    _stream_softmax(sc_ref, pc_ref, neg_stats, jnp.where(cend <= tq_c, 0.0, NEG), tq)
    both = jnp.concatenate([times_values(pc_ref, cmp_ref[0, 1], g, extra=ovl_ref[...]) for g in range(n_g)], axis=0)
    acc_c, imp_h = both[:, 0:LANES], both[:, LANES:]
    lane_r = lax.broadcasted_iota(jnp.int32, (rows, LANES), 1)
    grp_r = lax.broadcasted_iota(jnp.int32, (rows, LANES), 0) // half
    den_c = jnp.where(lane_r // HEAD_DIM == grp_r, pltpu.roll(acc_c, HEAD_DIM, axis=1), acc_c)
    has_block = jnp.where(t0 + lax.broadcasted_iota(jnp.int32, (tq, LANES), 0) >= CMP_LEN - 1, 1.0, 0.0)
    inv_c = jnp.concatenate([has_block] * n_h, axis=0) / den_c
    o_c = acc_c * inv_c

    imp_h = imp_h * inv_c
    imp_t = jnp.concatenate([functools.reduce(jnp.add, [imp_h[(g * n_r + r) * tq:(g * n_r + r + 1) * tq]
                                                        for r in range(n_r)]) for g in range(n_g)], axis=0).T
    jn = lax.broadcasted_iota(jnp.int32, (LANES, n_g * tq), 0)
    tpos = t0 + lax.broadcasted_iota(jnp.int32, (LANES, tq), 1)
    blk_t = jnp.concatenate([tpos] * n_g, axis=1) // SEL_BLOCK
    valid = jn <= blk_t
    forced = (jn == 0) | (jn == blk_t) | (jn == blk_t - 1)
    others = _pick_topk_rows(jnp.where(valid & jnp.logical_not(forced), imp_t, -jnp.inf), min(SEL_TOPK, ns) - 3)
    keep = valid & (forced | (others > 0.5))

    lhs_ref[:, 0:LANES] = stack(qr_ref)
    wk = WINDOW + tq
    sw_ref, pw_ref = s_ref.at[:, 0:wk], p_ref.at[:, 0:wk]
    kstart = pl.multiple_of(jnp.clip(t0 - WINDOW, 0, seq - wk), tq)
    diff = (t0 + lax.broadcasted_iota(jnp.int32, (tq, wk), 0)) - (
        kstart + lax.broadcasted_iota(jnp.int32, (tq, wk), 1))
    sw_ref[...] = lax.dot_general(lhs_ref[:, 0:LANES], kv4_ref[0, pl.ds(kstart, wk), 2 * LANES:3 * LANES], NT_DIMS,
                                  preferred_element_type=F32)
    _stream_softmax(sw_ref, pw_ref, neg_stats, jnp.where((diff >= 0) & (diff < WINDOW), 0.0, NEG), tq)
    acc_w = jnp.concatenate([times_values(pw_ref, kv4_ref[0, pl.ds(kstart, wk), 3 * LANES:4 * LANES], g)
                             for g in range(n_g)], axis=0)

    drop = jnp.where(keep, 0.0, -BIG).T.astype(BF16)
    lhs_ref[:, LANES:2 * LANES] = jnp.concatenate(
        [drop[g * tq:(g + 1) * tq] for g in range(n_g) for _ in range(n_r)], axis=0)

    m_ref[...] = neg_stats
    acc_ref[...] = jnp.zeros(acc_ref.shape, F32)

    def sel_tile(k0, width, bias):
        rhs = jnp.concatenate([kv4_ref[0, pl.ds(k0, width), 0:LANES], hot_ref[pl.ds(k0, width), :]], axis=1)
        s_ref[:, 0:width] = lax.dot_general(lhs_ref[...], rhs, NT_DIMS, preferred_element_type=F32)
        views = []
        for g in range(n_g):
            gs = slice(g * half, (g + 1) * half)
            views.append((gs, s_ref.at[gs, 0:width], p_ref.at[gs, 0:width]))
        for g, (gs, sg_ref, pg_ref) in enumerate(views):
            m_old = m_ref[gs, :]
            m_new, _ = _stream_softmax(sg_ref, pg_ref, m_old, bias, tq)
            m_ref[gs, :] = m_new
            vl = lax.broadcasted_iota(jnp.int32, (width, LANES), 1)
            v = kv4_ref[0, pl.ds(k0, width), LANES:2 * LANES]
            vg = jnp.where((vl >= g * HEAD_DIM) & (vl < (g + 1) * HEAD_DIM), v, jnp.ones_like(v))
            acc_ref[gs, :] = jnp.exp2(m_old - m_new) * acc_ref[gs, :] + jnp.dot(pg_ref[...], vg,
                                                                                preferred_element_type=F32)

    last = t0 // tk

    @pl.loop(0, last)
    def _(kt):
        sel_tile(pl.multiple_of(kt * tk, tk), tk, None)

    k_last = pl.multiple_of(last * tk, tk)

    def last_tile(width):
        kpos = k_last + lax.broadcasted_iota(jnp.int32, (tq, width), 1)
        tq_s = t0 + lax.broadcasted_iota(jnp.int32, (tq, width), 0)
        sel_tile(k_last, width, jnp.where(kpos <= tq_s, 0.0, NEG))

    in_first_half = t0 - k_last + tq <= tk // 2
    pl.when(in_first_half)(lambda: last_tile(tk // 2))
    pl.when(jnp.logical_not(in_first_half))(lambda: last_tile(tk))

    acc_s = acc_ref[...]

    gates_b = gates.astype(BF16)
    spread = [jnp.dot(gates_b, gexp_ref[c], preferred_element_type=F32) for c in range(3)]
    first_half = lane < HEAD_DIM
    for r in range(n_r):
        rows_g = [slice((g * n_r + r) * tq, (g * n_r + r + 1) * tq) for g in range(n_g)]
        y = spread[0][:, r * LANES:(r + 1) * LANES] * jnp.where(first_half, o_c[rows_g[0]], o_c[rows_g[1]])
        for c, acc in ((1, acc_s), (2, acc_w)):
            vals = jnp.where(first_half, acc[rows_g[0]], acc[rows_g[1]])
            dens = pltpu.roll(jnp.where(first_half, acc[rows_g[1]], acc[rows_g[0]]), HEAD_DIM, axis=1)
            y = y + spread[c][:, r * LANES:(r + 1) * LANES] * (vals / dens)
        o_ref[0, :, r * LANES:(r + 1) * LANES] = y.astype(BF16)


def _nsa(qn, qr, kv4, cmp, gates, *, tk):
    bsz, s, _ = qn.shape
    tq = NSA_TQ
    ns = s // SEL_BLOCK
    assert ns <= LANES
    assert min(SEL_TOPK, ns) >= 3
    ovl = jnp.pad(_overlap(cmp.shape[2], ns), ((0, 0), (0, LANES - ns))).astype(BF16)
    hot = _block_one_hot(s, SEL_BLOCK)
    head_of_p = _nsa_head_perm() // HEAD_DIM
    gexp = (jnp.arange(LANES)[None, :, None] == 3 * head_of_p[None, None, :] + jnp.arange(3)[:, None, None]).astype(BF16)
    rows = NSA_HEADS * tq
    width = max(cmp.shape[2], WINDOW + tq, tk)
    row = lambda b, i: (b, i, 0)
    return pl.pallas_call(
        functools.partial(_nsa_kernel, seq=s, tk=tk),
        out_shape=jax.ShapeDtypeStruct((bsz, s, NSA_QD), BF16),
        grid=(bsz, s // tq),
        in_specs=[
            pl.BlockSpec((1, tq, NSA_QD), row),
            pl.BlockSpec((1, tq, NSA_QD), row),
            pl.BlockSpec((1, s, 4 * NSA_KVD), lambda b, i: (b, 0, 0), pipeline_mode=pl.Buffered(1)),
            pl.BlockSpec((1, 2) + cmp.shape[2:], lambda b, i: (b, 0, 0, 0), pipeline_mode=pl.Buffered(1)),
            pl.BlockSpec((1, tq, LANES), row),
            _resident(ovl.shape),
            _resident(hot.shape),
            _resident(gexp.shape),
        ],
        out_specs=pl.BlockSpec((1, tq, NSA_QD), row),
        scratch_shapes=[pltpu.VMEM((rows, 2 * LANES), BF16), pltpu.VMEM((rows, width), F32),
                        pltpu.VMEM((rows, width), BF16),
                        pltpu.VMEM((rows, LANES), F32), pltpu.VMEM((rows, LANES), F32)],
        compiler_params=_params(("parallel", "parallel")),
        name="nsa_attn",
    )(qn, qr, kv4, cmp, gates, ovl, hot, gexp)


def _rope_tables(s):
    half = ROT_DIM // 2
    pos = jnp.arange(s, dtype=F32)
    freqs = ROPE_THETA ** (-jnp.arange(half, dtype=F32) * 2.0 / ROT_DIM)
    ang = pos[:, None] * freqs[None, :]
    cos, sin = jnp.cos(ang), jnp.sin(ang)
    ones = jnp.ones((s, HEAD_DIM - ROT_DIM), F32)
    zeros = jnp.zeros((s, HEAD_DIM - ROT_DIM), F32)
    zh = jnp.zeros((s, half), F32)
    cos_t = jnp.concatenate([cos, cos, ones], axis=1)
    sa_t = jnp.concatenate([-sin, zh, zeros], axis=1)
    sb_t = jnp.concatenate([zh, sin, zeros], axis=1)
    return tuple(jnp.tile(t, (1, LANES // HEAD_DIM)) for t in (cos_t, sa_t, sb_t))


def _block_one_hot(s, block):
    return (jnp.arange(s)[:, None] // block == jnp.arange(LANES)[None, :]).astype(BF16)


def _block_diag(w):
    n, a, b = w.shape
    eye = jnp.eye(n, dtype=w.dtype)
    return (eye[:, None, :, None] * w[:, :, None, :]).reshape(n * a, n * b)


def _nsa_head_perm():
    r = jnp.arange(NSA_HPG)[:, None, None]
    g = jnp.arange(NSA_KV_GROUPS)[None, :, None]
    dd = jnp.arange(HEAD_DIM)[None, None, :]
    return (HEAD_DIM * (NSA_HPG * g + r) + dd).reshape(-1)


def _overlap(nrow, ns):
    ci = jnp.arange(nrow)[:, None] * CMP_STRIDE
    sj = jnp.arange(ns)[None, :] * SEL_BLOCK
    ov = jnp.clip(jnp.minimum(ci + CMP_LEN, sj + SEL_BLOCK) - jnp.maximum(ci, sj), 0, None)
    return ov.astype(F32) / CMP_LEN


def kernel(x, c, mod_w, mod_b, norm_g, ffn_w1, ffn_w2, mix0_in_w, lru_conv_w, lru_conv_b, lru_wa, lru_ba,
           lru_wx, lru_bx, lru_lambda, mix0_out_w, mix1_in_w, cmp_pos, cmp_w1, cmp_w2, mix1_out_w, final_norm_g):
    bsz, s, d = x.shape
    depth = mod_w.shape[0]
    tm = 512
    assert d == D_MODEL and s % (2 * tm) == 0 and s >= WINDOW + NSA_TQ

    mod = _modulation(c, mod_w, mod_b)
    tabs = _rope_tables(s)
    w1 = ffn_w1.astype(BF16)
    w2 = ffn_w2.astype(BF16)

    for l in range(depth):
        j = l // 2
        x = _ffn(x, mod[l], norm_g[l, 0], w1, w2, (l, 0), final_norm_g, row0=0, final=False, tm=tm)
        if l % 2 == 0:
            lru = (lru_conv_w[j], lru_conv_b[j], _block_diag(lru_wa[j]).astype(BF16), lru_ba[j].reshape(-1),
                   _block_diag(lru_wx[j]).astype(BF16), lru_bx[j].reshape(-1), lru_lambda[j])
            y_lru, q, k, v, cent = _inproj0(x, mod[l], norm_g[l, 1], mix0_in_w[j].astype(BF16), tabs, lru,
                                            tm=tm, lru_rows=256)
            y_att = _moba(q, k, v, cent.reshape(bsz, -1, MOBA_WIDTH))
            wo = mix0_out_w[j].astype(BF16)
            mix = ((y_lru, wo[:LRU_WIDTH]), (y_att, wo[LRU_WIDTH:]))
        else:
            perm = _nsa_head_perm()
            w_in = mix1_in_w[j]
            w_in = jnp.concatenate([w_in[:, :NSA_QD][:, perm], w_in[:, NSA_QD:]], axis=1)
            w_in = jnp.pad(w_in, ((0, 0), (0, IN1_PAD - w_in.shape[1]))).astype(BF16)
            qn, qr, kvc, kv4, gates = _inproj1(x, mod[l], norm_g[l, 1], w_in, tabs, tm=2 * tm)
            cmp = _compress(kvc, cmp_pos[j], cmp_w1[j], cmp_w2[j])
            y = _nsa(qn, qr, kv4, cmp, gates, tk=512)
            mix = ((y, mix1_out_w[j][perm].astype(BF16)),)
        x = _ffn(x, mod[l], norm_g[l, 2], w1, w2, (l, 1), final_norm_g,
                 row0=6, final=(l == depth - 1), tm=tm, mix=mix)
    return x
```
